```python
import jax, jax.numpy as jnp
from jax import lax
import numpy as np

D_MODEL = 2048
BATCH = 1
SEQ = 8192
DEPTH = 4

GRID_W = 64
CTX_LEN = 256
HEAD_DIM = 128
D_FOURIER = 512
N_FOURIER_GROUPS = 4
FOURIER_GROUP = D_FOURIER // N_FOURIER_GROUPS
N_HEADS = 12
N_KV_HEADS = 4
Q_PER_KV = N_HEADS // N_KV_HEADS
D_ATTN = N_HEADS * HEAD_DIM
D_MIX = D_FOURIER + D_ATTN
D_IN = D_FOURIER + (N_HEADS + 2 * N_KV_HEADS) * HEAD_DIM
WINDOW = 128
BLOCK = 128
ROPE_BASE = 10000.0
N_EXPERTS = 32
N_GROUPS = 8
EXPERTS_PER_GROUP = N_EXPERTS // N_GROUPS
TOP_K = 2
D_EXPERT = 512
D_SHARED = 512
MOE_BLOCK = 128
EPS = 1e-6
NEG = -1e30

kernel_name = 'hybrid_fourier_swa_moe_dit'


def rms_norm(t, g):
    tf = t.astype(jnp.float32)
    y = tf * lax.rsqrt(jnp.mean(tf * tf, axis=-1, keepdims=True) + EPS)
    return (y * g.astype(jnp.float32)).astype(t.dtype)


def axial_rope(n_tokens):
    rows = n_tokens // GRID_W
    row = jnp.repeat(jnp.arange(rows, dtype=jnp.float32), GRID_W)
    col = jnp.tile(jnp.arange(GRID_W, dtype=jnp.float32), rows)
    axis_dim = HEAD_DIM // 2
    inv = ROPE_BASE ** (-jnp.arange(0, axis_dim, 2, dtype=jnp.float32) / axis_dim)
    ar = row[:, None] * inv
    ac = col[:, None] * inv
    ang = jnp.concatenate([ar, ar, ac, ac], axis=-1)
    return jnp.cos(ang), jnp.sin(ang)


def apply_rope(t, cos, sin):
    tf = t.astype(jnp.float32)
    q4 = HEAD_DIM // 4
    t4 = tf.reshape(*tf.shape[:-1], 2, 2, q4)
    rot = jnp.stack([-t4[..., 1, :], t4[..., 0, :]], axis=-2).reshape(tf.shape)
    return (tf * cos[None, :, None, :] + rot * sin[None, :, None, :]).astype(t.dtype)


def project(h, w_in):
    B, N, _ = h.shape
    p = h @ w_in
    o1 = D_FOURIER
    o2 = o1 + D_ATTN
    o3 = o2 + N_KV_HEADS * HEAD_DIM
    u = p[..., :o1]
    q = p[..., o1:o2].reshape(B, N, N_HEADS, HEAD_DIM)
    k = p[..., o2:o3].reshape(B, N, N_KV_HEADS, HEAD_DIM)
    v = p[..., o3:].reshape(B, N, N_KV_HEADS, HEAD_DIM)
    return u, q, k, v


def fourier_mix(u, w_f):
    B, N, _ = u.shape
    ug = u.astype(jnp.float32).reshape(B, N, N_FOURIER_GROUPS, FOURIER_GROUP)
    f = jnp.fft.fftn(ug, axes=(1, 3), norm='ortho').real
    return f.reshape(B, N, D_FOURIER).astype(u.dtype) @ w_f


def latent_attention(q, k, v, k_ctx, v_ctx, sink):
    B, S = q.shape[:2]
    nb = S // BLOCK
    scale = HEAD_DIM ** -0.5
    qb = q.reshape(B, nb, BLOCK, N_KV_HEADS, Q_PER_KV, HEAD_DIM)

    def band(t):
        tp = jnp.pad(t, ((0, 0), (BLOCK, BLOCK), (0, 0), (0, 0)))
        tp = tp.reshape(B, nb + 2, BLOCK, N_KV_HEADS, HEAD_DIM)
        return jnp.concatenate([tp[:, :-2], tp[:, 1:-1], tp[:, 2:]], axis=2)

    kw, vw = band(k), band(v)
    s_loc = jnp.einsum('bnqhgd,bnkhd->bnhgqk', qb, kw, preferred_element_type=jnp.float32) * scale
    qpos = jnp.arange(nb)[:, None, None] * BLOCK + jnp.arange(BLOCK)[None, :, None]
    kpos = jnp.arange(nb)[:, None, None] * BLOCK - BLOCK + jnp.arange(3 * BLOCK)[None, None, :]
    valid = (jnp.abs(qpos - kpos) <= WINDOW) & (kpos >= 0) & (kpos < S)
    s_loc = jnp.where(valid[None, :, None, None], s_loc, NEG)
    s_ctx = jnp.einsum('bnqhgd,bchd->bnhgqc', qb, k_ctx, preferred_element_type=jnp.float32) * scale
    s_sink = jnp.broadcast_to(sink.astype(jnp.float32).reshape(1, 1, N_KV_HEADS, Q_PER_KV, 1, 1),
                              s_loc.shape[:-1] + (1,))
    p = jax.nn.softmax(jnp.concatenate([s_loc, s_ctx, s_sink], axis=-1), axis=-1)
    n_loc = 3 * BLOCK
    n_ctx = k_ctx.shape[1]
    p_loc = p[..., :n_loc].astype(v.dtype)
    p_ctx = p[..., n_loc:n_loc + n_ctx].astype(v.dtype)
    o = (jnp.einsum('bnhgqk,bnkhd->bnqhgd', p_loc, vw)
         + jnp.einsum('bnhgqc,bchd->bnqhgd', p_ctx, v_ctx))
    return o.reshape(B, S, D_ATTN)


def context_attention(q, k, v, sink):
    B, C = q.shape[:2]
    scale = HEAD_DIM ** -0.5
    qg = q.reshape(B, C, N_KV_HEADS, Q_PER_KV, HEAD_DIM)
    s = jnp.einsum('bqhgd,bkhd->bhgqk', qg, k, preferred_element_type=jnp.float32) * scale
    s_sink = jnp.broadcast_to(sink.astype(jnp.float32).reshape(1, N_KV_HEADS, Q_PER_KV, 1, 1),
                              s.shape[:-1] + (1,))
    p = jax.nn.softmax(jnp.concatenate([s, s_sink], axis=-1), axis=-1)[..., :C].astype(v.dtype)
    o = jnp.einsum('bhgqk,bkhd->bqhgd', p, v)
    return o.reshape(B, C, D_ATTN)


def merge_heads(f, a, g_f, g_a, w_out):
    return jnp.concatenate([rms_norm(f, g_f), rms_norm(a, g_a)], axis=-1) @ w_out


def route(h, w_router, b_router):
    T = h.shape[0]
    scores = jax.nn.sigmoid(h.astype(jnp.float32) @ w_router.astype(jnp.float32))
    sel = (scores + b_router.astype(jnp.float32)).reshape(T, N_GROUPS, EXPERTS_PER_GROUP)
    gscore = lax.top_k(sel, TOP_K)[0].sum(-1)
    g_idx = jnp.argmax(gscore, axis=-1)
    in_grp = jnp.take_along_axis(sel, g_idx[:, None, None], axis=1)[:, 0]
    _, local = lax.top_k(in_grp, TOP_K)
    idx = g_idx[:, None] * EXPERTS_PER_GROUP + local
    w = jnp.take_along_axis(scores, idx, axis=-1)
    w = w / jnp.sum(w, axis=-1, keepdims=True)
    return idx, w


def routed_experts(h, idx, w, w_g, w_u, w_d):
    T, D = h.shape
    A = T * TOP_K
    e_flat = idx.reshape(A).astype(jnp.int32)
    tok_flat = jnp.repeat(jnp.arange(T, dtype=jnp.int32), TOP_K)
    w_flat = w.reshape(A)
    counts = jnp.zeros((N_EXPERTS,), jnp.int32).at[e_flat].add(1)
    padded = ((counts + MOE_BLOCK - 1) // MOE_BLOCK) * MOE_BLOCK
    pad_end = jnp.cumsum(padded)
    pad_start = pad_end - padded
    cnt_start = jnp.cumsum(counts) - counts
    order = jnp.argsort(e_flat)
    e_sorted = e_flat[order]
    rank = jnp.arange(A, dtype=jnp.int32) - cnt_start[e_sorted]
    dest = pad_start[e_sorted] + rank
    n_blocks = -(-A // MOE_BLOCK) + N_EXPERTS
    n_rows = n_blocks * MOE_BLOCK
    row_tok = jnp.full((n_rows,), T, jnp.int32).at[dest].set(tok_flat[order])
    row_w = jnp.zeros((n_rows,), w.dtype).at[dest].set(w_flat[order])
    block_e = jnp.minimum(jnp.searchsorted(pad_end, jnp.arange(n_blocks, dtype=jnp.int32) * MOE_BLOCK,
                                           side='right'), N_EXPERTS - 1)
    h_pad = jnp.concatenate([h, jnp.zeros((1, D), h.dtype)], axis=0)
    xb = h_pad[row_tok].reshape(n_blocks, MOE_BLOCK, D)

    def expert_block(args):
        xe, e = args
        return (jax.nn.silu(xe @ w_g[e]) * (xe @ w_u[e])) @ w_d[e]

    yb = lax.map(expert_block, (xb, block_e))
    y = yb.reshape(n_rows, D) * row_w[:, None].astype(yb.dtype)
    return jax.ops.segment_sum(y, row_tok, num_segments=T + 1)[:T]


def moe_ffn(h, w_router, b_router, w_eg, w_eu, w_ed, w_sg, w_su, w_sd):
    idx, w = route(h, w_router, b_router)
    routed = routed_experts(h, idx, w.astype(h.dtype), w_eg, w_eu, w_ed)
    shared = (jax.nn.silu(h @ w_sg) * (h @ w_su)) @ w_sd
    return routed + shared


def setup_inputs(seed: int = 0) -> dict:
    key = jax.random.key(seed)
    ks = jax.random.split(key, 32)
    f32 = jnp.float32
    L, D = DEPTH, D_MODEL

    def nrm(k, shape, s):
        return jax.random.normal(k, shape, f32) * s

    def gain(k, shape):
        return 1.0 + 0.02 * jax.random.normal(k, shape, f32)

    return {
        'x': nrm(ks[0], (BATCH, SEQ, D), 1.0),
        'c': nrm(ks[1], (BATCH, D), 1.0),
        'ctx': nrm(ks[2], (BATCH, CTX_LEN, D), 1.0),
        'c_ctx': nrm(ks[3], (D,), 1.0),
        'w_mod': nrm(ks[4], (L, D, 6 * D), 0.5 * D ** -0.5),
        'b_mod': nrm(ks[5], (L, 6 * D), 0.01),
        'g_norm1': gain(ks[6], (L, D)),
        'g_norm2': gain(ks[7], (L, D)),
        'w_in': nrm(ks[8], (L, D, D_IN), D ** -0.5),
        'w_fourier': nrm(ks[9], (L, D_FOURIER, D_FOURIER), D_FOURIER ** -0.5),
        'g_q': gain(ks[10], (L, HEAD_DIM)),
        'g_k': gain(ks[11], (L, HEAD_DIM)),
        'sink': nrm(ks[12], (L, N_HEADS), 0.5),
        'g_branch_f': gain(ks[13], (L, D_FOURIER)),
        'g_branch_a': gain(ks[14], (L, D_ATTN)),
        'w_out': nrm(ks[15], (L, D_MIX, D), D_MIX ** -0.5),
        'w_router': nrm(ks[16], (D, N_EXPERTS), D ** -0.5),
        'b_router': nrm(ks[17], (N_EXPERTS,), 0.01),
        'w_exp_gate': nrm(ks[18], (L, N_EXPERTS, D, D_EXPERT), D ** -0.5),
        'w_exp_up': nrm(ks[19], (L, N_EXPERTS, D, D_EXPERT), D ** -0.5),
        'w_exp_down': nrm(ks[20], (L, N_EXPERTS, D_EXPERT, D), D_EXPERT ** -0.5),
        'w_sh_gate': nrm(ks[21], (L, D, D_SHARED), D ** -0.5),
        'w_sh_up': nrm(ks[22], (L, D, D_SHARED), D ** -0.5),
        'w_sh_down': nrm(ks[23], (L, D_SHARED, D), D_SHARED ** -0.5),
    }


def reference(x, c, ctx, c_ctx, w_mod, b_mod, g_norm1, g_norm2, w_in, w_fourier, g_q, g_k, sink,
              g_branch_f, g_branch_a, w_out, w_router, b_router, w_exp_gate, w_exp_up, w_exp_down,
              w_sh_gate, w_sh_up, w_sh_down):
    B, S, D = x.shape
    C = ctx.shape[1]
    cos, sin = axial_rope(S)
    for l in range(DEPTH):
        last = l == DEPTH - 1
        mod_x = (jax.nn.silu(c) @ w_mod[l] + b_mod[l])[:, None, :]
        mod_c = (jax.nn.silu(c_ctx) @ w_mod[l] + b_mod[l])[None, None, :]
        sh1, sc1, ga1, sh2, sc2, ga2 = jnp.split(mod_x, 6, axis=-1)
        csh1, csc1, cga1, csh2, csc2, cga2 = jnp.split(mod_c, 6, axis=-1)

        hx = rms_norm(x, g_norm1[l]) * (1 + sc1) + sh1
        hc = rms_norm(ctx, g_norm1[l]) * (1 + csc1) + csh1
        ux, qx, kx, vx = project(hx, w_in[l])
        uc, qc, kc, vc = project(hc, w_in[l])
        qx = apply_rope(rms_norm(qx, g_q[l]), cos, sin)
        kx = apply_rope(rms_norm(kx, g_k[l]), cos, sin)
        qc = rms_norm(qc, g_q[l])
        kc = rms_norm(kc, g_k[l])
        ax = latent_attention(qx, kx, vx, kc, vc, sink[l])
        fx = fourier_mix(ux, w_fourier[l])
        x = x + ga1 * merge_heads(fx, ax, g_branch_f[l], g_branch_a[l], w_out[l])
        if not last:
            ac = context_attention(qc, kc, vc, sink[l])
            fc = fourier_mix(uc, w_fourier[l])
            ctx = ctx + cga1 * merge_heads(fc, ac, g_branch_f[l], g_branch_a[l], w_out[l])

        hx2 = (rms_norm(x, g_norm2[l]) * (1 + sc2) + sh2).reshape(B * S, D)
        if not last:
            hc2 = (rms_norm(ctx, g_norm2[l]) * (1 + csc2) + csh2).reshape(B * C, D)
            tokens = jnp.concatenate([hc2, hx2], axis=0)
        else:
            tokens = hx2
        y = moe_ffn(tokens, w_router, b_router, w_exp_gate[l], w_exp_up[l], w_exp_down[l],
                    w_sh_gate[l], w_sh_up[l], w_sh_down[l])
        if not last:
            ctx = ctx + cga2 * y[:B * C].reshape(B, C, D)
            x = x + ga2 * y[B * C:].reshape(B, S, D)
        else:
            x = x + ga2 * y.reshape(B, S, D)
    return x
```

```python
import functools

import numpy as np
import jax
import jax.numpy as jnp
from jax import lax
from jax.experimental import pallas as pl
from jax.experimental.pallas import tpu as pltpu

F32 = jnp.float32
BF16 = jnp.bfloat16
I32 = jnp.int32

D_MODEL = 2048
SEQ = 8192
CTX = 256
T = SEQ + CTX
DEPTH = 4
GRID_W = 64
HEAD_DIM = 128
D_F = 512
N_FG = 4
N_HEADS = 12
N_KV = 4
Q_PER_KV = 3
D_ATTN = N_HEADS * HEAD_DIM
D_KV = N_KV * HEAD_DIM
D_IN = D_F + D_ATTN + 2 * D_KV
ROPE_BASE = 10000.0
N_EXPERTS = 32
N_GROUPS = 8
EPG = 4
D_EXPERT = 512
D_SHARED = 512
EPS = 1e-6
NEG = -1e30

TM = 256
NT = T // TM
NT_LAT = SEQ // TM
QB = 128
NQB = T // QB
NQB_LAT = SEQ // QB
BM = 256
NB = (2 * T + BM - 1) // BM + N_EXPERTS
ROWS = NB * BM
N1 = 128
N2 = 64
SUB = 8
RL = D_MODEL // SUB
VMEM_LIMIT = 56 * 1024 * 1024


def _cp(sem):
    return pltpu.CompilerParams(dimension_semantics=sem, vmem_limit_bytes=VMEM_LIMIT)


def _silu(a):
    return a * jax.nn.sigmoid(a)


@functools.lru_cache(maxsize=None)
def _tables():
    t = np.arange(SEQ)
    row = (t // GRID_W).astype(np.float64)
    col = (t % GRID_W).astype(np.float64)
    inv = ROPE_BASE ** (-np.arange(0, HEAD_DIM // 2, 2, dtype=np.float64) / (HEAD_DIM // 2))
    ar = row[:, None] * inv
    ac = col[:, None] * inv
    ang = np.concatenate([ar, ar, ac, ac], axis=-1)
    cos = np.concatenate([np.cos(ang), np.ones((CTX, HEAD_DIM))], axis=0)
    sin = np.concatenate([np.sin(ang), np.zeros((CTX, HEAD_DIM))], axis=0)
    first = (np.arange(HEAD_DIM) % 64) < 32
    rope_a = np.where(first[None, :], -sin, 0.0)
    rope_b = np.where(first[None, :], 0.0, sin)

    def cs(n):
        k = np.arange(n)
        a = 2.0 * np.pi * np.outer(k, k) / n
        return np.cos(a), np.sin(a)

    c64, s64 = cs(N2)
    c128, s128 = cs(N1)
    c256, s256 = cs(CTX)
    m1 = np.concatenate([c64, -s64], axis=0)
    m2 = np.concatenate([c128, s128], axis=0)
    mc = np.concatenate([c256, s256], axis=0)
    csch = np.concatenate([c128, s128], axis=0)
    a = 2.0 * np.pi * np.outer(np.arange(N1), np.arange(N2)) / SEQ
    twr = np.broadcast_to(np.cos(a)[:, :, None], (N1, N2, 128))
    twi = np.broadcast_to(-np.sin(a)[:, :, None], (N1, N2, 128))
    tri = (np.arange(TM)[:, None] < np.arange(TM)[None, :]).astype(np.float32)
    return dict(
        cos=jnp.asarray(cos, F32), rope_a=jnp.asarray(rope_a, F32), rope_b=jnp.asarray(rope_b, F32),
        m1=jnp.asarray(m1, BF16), m2=jnp.asarray(m2, BF16), mc=jnp.asarray(mc, BF16),
        csch=jnp.asarray(csch, BF16), twr=jnp.asarray(twr, F32), twi=jnp.asarray(twi, F32),
        tri=jnp.asarray(tri, BF16))


MOD_TN = 1024


def _mod_kernel(cc_ref, w_ref, b_ref, o_ref):
    a = _silu(cc_ref[...])
    o_ref[...] = jnp.dot(a.astype(BF16), w_ref[...].astype(BF16),
                         preferred_element_type=F32) + b_ref[...]


def _modulation(cc, w_mod, b_mod):
    nl = w_mod.shape[0]
    n6 = w_mod.shape[2]
    return pl.pallas_call(
        _mod_kernel,
        grid=(nl, n6 // MOD_TN),
        in_specs=[pl.BlockSpec((8, D_MODEL), lambda l, j: (0, 0)),
                  pl.BlockSpec((None, D_MODEL, MOD_TN), lambda l, j: (l, 0, j)),
                  pl.BlockSpec((None, 1, MOD_TN), lambda l, j: (l, 0, j))],
        out_specs=pl.BlockSpec((None, 8, MOD_TN), lambda l, j: (l, 0, j)),
        out_shape=jax.ShapeDtypeStruct((nl, 8, n6), F32),
        compiler_params=_cp(("arbitrary", "arbitrary")),
        name="modulation",
    )(cc, w_mod, b_mod.reshape(nl, 1, n6))


def _mod_spec(chunk):
    return pl.BlockSpec((None, None, 1, D_MODEL), lambda i: (i // NT_LAT, chunk, 0, 0))


def _row_spec(n):
    return pl.BlockSpec((1, n), lambda i: (0, 0))


PCW = 256


def _proj_kernel(x_ref, g_ref, sc_ref, sh_ref, w_ref, gq_ref, gk_ref, cos_ref, ra_ref, rb_ref,
                 u_ref, q_ref, k_ref, v_ref, hb_ref):
    x = x_ref[...]
    ms = jnp.mean(x * x, axis=-1, keepdims=True)
    h = x * lax.rsqrt(ms + EPS) * g_ref[...]
    h = h * (1.0 + sc_ref[...]) + sh_ref[...]
    hb_ref[...] = h.astype(BF16)
    cos = cos_ref[...]
    ra = ra_ref[...]
    rb = rb_ref[...]

    def head(t, g):
        m = jnp.mean(t * t, axis=-1, keepdims=True)
        t = t * lax.rsqrt(m + EPS) * g
        return t * cos + pltpu.roll(t, 96, 1) * ra + pltpu.roll(t, 32, 1) * rb

    scale = HEAD_DIM ** -0.5
    for c in range(D_IN // PCW):
        col = c * PCW
        p = jnp.dot(hb_ref[...], w_ref[:, col:col + PCW], preferred_element_type=F32)
        if col < D_F:
            u_ref[:, col:col + PCW] = p.astype(BF16)
        elif col < D_F + D_ATTN:
            o = col - D_F
            for j in range(PCW // HEAD_DIM):
                t = head(p[:, j * HEAD_DIM:(j + 1) * HEAD_DIM], gq_ref[...]) * scale
                q_ref[:, o + j * HEAD_DIM:o + (j + 1) * HEAD_DIM] = t.astype(BF16)
        elif col < D_F + D_ATTN + D_KV:
            o = col - D_F - D_ATTN
            for j in range(PCW // HEAD_DIM):
                t = head(p[:, j * HEAD_DIM:(j + 1) * HEAD_DIM], gk_ref[...])
                k_ref[:, o + j * HEAD_DIM:o + (j + 1) * HEAD_DIM] = t.astype(BF16)
        else:
            o = col - D_F - D_ATTN - D_KV
            v_ref[:, o:o + PCW] = p.astype(BF16)


def _project(xs, g1, modt, w_in_bf, gq, gk, tb):
    tile = lambda n: pl.BlockSpec((TM, n), lambda i: (i, 0))
    return pl.pallas_call(
        _proj_kernel,
        grid=(NT,),
        in_specs=[tile(D_MODEL), _row_spec(D_MODEL), _mod_spec(1), _mod_spec(0),
                  pl.BlockSpec((D_MODEL, D_IN), lambda i: (0, 0)),
                  _row_spec(HEAD_DIM), _row_spec(HEAD_DIM),
                  tile(HEAD_DIM), tile(HEAD_DIM), tile(HEAD_DIM)],
        out_specs=[tile(D_F), tile(D_ATTN), tile(D_KV), tile(D_KV)],
        out_shape=[jax.ShapeDtypeStruct((T, D_F), BF16), jax.ShapeDtypeStruct((T, D_ATTN), BF16),
                   jax.ShapeDtypeStruct((T, D_KV), BF16), jax.ShapeDtypeStruct((T, D_KV), BF16)],
        scratch_shapes=[pltpu.VMEM((TM, D_MODEL), BF16)],
        compiler_params=_cp(("arbitrary",)),
        name="norm_proj",
    )(xs, g1, modt, modt, w_in_bf, gq, gk, tb["cos"], tb["rope_a"], tb["rope_b"])


def _attn_kernel(sink_ref, q_ref, kp_ref, kc_ref, kn_ref, vp_ref, vc_ref, vn_ref, kx_ref, vx_ref,
                 ga_ref, o_ref, a_scr):
    b = pl.program_id(0)
    lat = b < NQB_LAT
    f_prev = jnp.logical_and(lat, b > 0).astype(I32)
    f_cur = lat.astype(I32)
    f_next = (b < NQB_LAT - 1).astype(I32)
    n3 = Q_PER_KV * QB
    r = lax.broadcasted_iota(I32, (n3, 3 * QB), 0) & (QB - 1)
    cidx = lax.broadcasted_iota(I32, (n3, 3 * QB), 1)
    blk = cidx >> 7
    j = cidx & (QB - 1)
    ok = jnp.where(blk == 0, (j >= r).astype(I32) * f_prev,
                   jnp.where(blk == 1, f_cur, (j <= r).astype(I32) * f_next))
    valid = ok > 0
    dn = (((1,), (1,)), ((), ()))
    for hk in range(N_KV):
        ks = slice(hk * HEAD_DIM, (hk + 1) * HEAD_DIM)
        qs = jnp.concatenate(
            [q_ref[:, (hk * Q_PER_KV + g) * HEAD_DIM:(hk * Q_PER_KV + g + 1) * HEAD_DIM]
             for g in range(Q_PER_KV)], axis=0)
        kw = jnp.concatenate([kp_ref[:, ks], kc_ref[:, ks], kn_ref[:, ks]], axis=0)
        vw = jnp.concatenate([vp_ref[:, ks], vc_ref[:, ks], vn_ref[:, ks]], axis=0)
        s = lax.dot_general(qs, kw, dn, preferred_element_type=F32)
        s = jnp.where(valid, s, NEG)
        sx = lax.dot_general(qs, kx_ref[:, ks], dn, preferred_element_type=F32)
        sk = jnp.concatenate(
            [jnp.full((QB, 1), sink_ref[hk * Q_PER_KV + g], F32) for g in range(Q_PER_KV)], axis=0)
        m = jnp.maximum(jnp.maximum(jnp.max(s, axis=-1, keepdims=True),
                                    jnp.max(sx, axis=-1, keepdims=True)), sk)
        e = jnp.exp(s - m)
        ex = jnp.exp(sx - m)
        den = (jnp.sum(e, axis=-1, keepdims=True) + jnp.sum(ex, axis=-1, keepdims=True)
               + jnp.exp(sk - m))
        o = (jnp.dot(e.astype(BF16), vw, preferred_element_type=F32)
             + jnp.dot(ex.astype(BF16), vx_ref[:, ks], preferred_element_type=F32)) / den
        for g in range(Q_PER_KV):
            h = hk * Q_PER_KV + g
            a_scr[:, h * HEAD_DIM:(h + 1) * HEAD_DIM] = o[g * QB:(g + 1) * QB]
    a = a_scr[...]
    ms = jnp.mean(a * a, axis=-1, keepdims=True)
    o_ref[...] = (a * lax.rsqrt(ms + EPS) * ga_ref[...]).astype(BF16)


def _attention(q, k, v, sink, g_a):
    kv = lambda f: pl.BlockSpec((QB, D_KV), f)
    prev = lambda b: (jnp.maximum(b - 1, 0), 0)
    cur = lambda b: (b, 0)
    nxt = lambda b: (jnp.minimum(b + 1, NQB - 1), 0)
    ctx = pl.BlockSpec((CTX, D_KV), lambda b: (SEQ // CTX, 0))
    return pl.pallas_call(
        _attn_kernel,
        grid=(NQB,),
        in_specs=[pl.BlockSpec(memory_space=pltpu.SMEM),
                  pl.BlockSpec((QB, D_ATTN), cur),
                  kv(prev), kv(cur), kv(nxt), kv(prev), kv(cur), kv(nxt), ctx, ctx,
                  _row_spec(D_ATTN)],
        out_specs=pl.BlockSpec((QB, D_ATTN), cur),
        out_shape=jax.ShapeDtypeStruct((T, D_ATTN), BF16),
        scratch_shapes=[pltpu.VMEM((QB, D_ATTN), F32)],
        compiler_params=_cp(("arbitrary",)),
        name="attention",
    )(sink, q, k, k, k, v, v, v, k, v, g_a)


F1_J = 8
F2_K = 4


def _f1_kernel(u_ref, m1_ref, twr_ref, twi_ref, z_ref):
    g = jnp.dot(m1_ref[...], u_ref[...], preferred_element_type=F32)
    for jl in range(F1_J):
        twr = twr_ref[jl]
        twi = twi_ref[jl]
        for lt in range(D_F // 128):
            c0 = jl * D_F + lt * 128
            gr = g[0:N2, c0:c0 + 128]
            gi = g[N2:2 * N2, c0:c0 + 128]
            z_ref[jl, 0:N2, lt * 128:(lt + 1) * 128] = (gr * twr - gi * twi).astype(BF16)
            z_ref[jl, N2:2 * N2, lt * 128:(lt + 1) * 128] = (gr * twi + gi * twr).astype(BF16)


def _f2_kernel(zr_ref, zi_ref, m2_ref, y_ref):
    pr = jnp.dot(m2_ref[...], zr_ref[...], preferred_element_type=F32)
    pi = jnp.dot(m2_ref[...], zi_ref[...], preferred_element_type=F32)
    yr = pr[0:N1] + pi[N1:2 * N1]
    yi = pi[0:N1] - pr[N1:2 * N1]
    for kl in range(F2_K):
        y_ref[:, kl * 2 * D_F:kl * 2 * D_F + D_F] = yr[:, kl * D_F:(kl + 1) * D_F].astype(BF16)
        y_ref[:, kl * 2 * D_F + D_F:(kl + 1) * 2 * D_F] = yi[:, kl * D_F:(kl + 1) * D_F].astype(BF16)


def _f3_kernel(y_ref, uc_ref, cs_ref, mc_ref, wf_ref, gf_ref, o_ref):
    i = pl.program_id(0)

    def finish(parts, scale):
        f = jnp.concatenate(parts, axis=1) * scale
        t = jnp.dot(f.astype(BF16), wf_ref[...], preferred_element_type=F32)
        ms = jnp.mean(t * t, axis=-1, keepdims=True)
        o_ref[...] = (t * lax.rsqrt(ms + EPS) * gf_ref[...]).astype(BF16)

    @pl.when(i < NT_LAT)
    def _():
        parts = []
        for g in range(N_FG):
            lhs = jnp.concatenate([y_ref[:, g * 128:(g + 1) * 128],
                                   y_ref[:, D_F + g * 128:D_F + (g + 1) * 128]], axis=1)
            parts.append(jnp.dot(lhs, cs_ref[...], preferred_element_type=F32))
        finish(parts, float((SEQ * 128) ** -0.5))

    @pl.when(i == NT_LAT)
    def _():
        pq = jnp.dot(mc_ref[...], uc_ref[...], preferred_element_type=F32)
        parts = []
        for g in range(N_FG):
            lhs = jnp.concatenate([pq[0:CTX, g * 128:(g + 1) * 128],
                                   -pq[CTX:2 * CTX, g * 128:(g + 1) * 128]], axis=1)
            parts.append(jnp.dot(lhs.astype(BF16), cs_ref[...], preferred_element_type=F32))
        finish(parts, float((CTX * 128) ** -0.5))


def _fourier(u, w_f_bf, g_f, tb):
    u2 = u.reshape(T // N1, N1 * D_F)
    z = pl.pallas_call(
        _f1_kernel,
        grid=(N1 // F1_J,),
        in_specs=[pl.BlockSpec((N2, F1_J * D_F), lambda c: (0, c)),
                  pl.BlockSpec((2 * N2, N2), lambda c: (0, 0)),
                  pl.BlockSpec((F1_J, N2, 128), lambda c: (c, 0, 0)),
                  pl.BlockSpec((F1_J, N2, 128), lambda c: (c, 0, 0))],
        out_specs=pl.BlockSpec((F1_J, 2 * N2, D_F), lambda c: (c, 0, 0)),
        out_shape=jax.ShapeDtypeStruct((N1, 2 * N2, D_F), BF16),
        compiler_params=_cp(("arbitrary",)),
        name="fourier_s1",
    )(u2, tb["m1"], tb["twr"], tb["twi"])
    z2 = z.reshape(N1, 2 * N2 * D_F)
    nk = N2 // F2_K
    y = pl.pallas_call(
        _f2_kernel,
        grid=(nk,),
        in_specs=[pl.BlockSpec((N1, F2_K * D_F), lambda k: (0, k)),
                  pl.BlockSpec((N1, F2_K * D_F), lambda k: (0, nk + k)),
                  pl.BlockSpec((2 * N1, N1), lambda k: (0, 0))],
        out_specs=pl.BlockSpec((N1, F2_K * 2 * D_F), lambda k: (0, k)),
        out_shape=jax.ShapeDtypeStruct((N1, N2 * 2 * D_F), BF16),
        compiler_params=_cp(("arbitrary",)),
        name="fourier_s2",
    )(z2, z2, tb["m2"])
    y2 = y.reshape(SEQ, 2 * D_F)
    return pl.pallas_call(
        _f3_kernel,
        grid=(NT,),
        in_specs=[pl.BlockSpec((TM, 2 * D_F), lambda i: (jnp.minimum(i, NT_LAT - 1), 0)),
                  pl.BlockSpec((CTX, D_F), lambda i: (SEQ // CTX, 0)),
                  pl.BlockSpec((2 * 128, 128), lambda i: (0, 0)),
                  pl.BlockSpec((2 * CTX, CTX), lambda i: (0, 0)),
                  pl.BlockSpec((D_F, D_F), lambda i: (0, 0)),
                  _row_spec(D_F)],
        out_specs=pl.BlockSpec((TM, D_F), lambda i: (i, 0)),
        out_shape=jax.ShapeDtypeStruct((T, D_F), BF16),
        compiler_params=_cp(("arbitrary",)),
        name="fourier_s3",
    )(y2, u, tb["csch"], tb["mc"], w_f_bf, g_f)


def _merge_kernel(x_ref, fn_ref, an_ref, wo_ref, ga1_ref, g2_ref, sc2_ref, sh2_ref, ga2_ref,
                  wsg_ref, wsu_ref, wsd_ref, wrh_ref, wrl_ref, br_ref, tri_ref,
                  xmid_ref, h2_ref, info_ref, wcol_ref, cnt_ref, carry_scr):
    i = pl.program_id(0)

    @pl.when(i == 0)
    def _():
        carry_scr[...] = jnp.zeros_like(carry_scr)

    m = (jnp.dot(fn_ref[...], wo_ref[0:D_F, :], preferred_element_type=F32)
         + jnp.dot(an_ref[...], wo_ref[D_F:D_MODEL, :], preferred_element_type=F32))
    x1 = x_ref[...] + ga1_ref[...] * m
    ms = jnp.mean(x1 * x1, axis=-1, keepdims=True)
    h = x1 * lax.rsqrt(ms + EPS) * g2_ref[...]
    h = h * (1.0 + sc2_ref[...]) + sh2_ref[...]
    hb = h.astype(BF16)
    for s in range(SUB):
        h2_ref[:, s, :] = h[:, s * RL:(s + 1) * RL]

    a = jnp.dot(hb, wsg_ref[...], preferred_element_type=F32)
    u = jnp.dot(hb, wsu_ref[...], preferred_element_type=F32)
    act = (_silu(a) * u).astype(BF16)
    ys = jnp.dot(act, wsd_ref[...], preferred_element_type=F32)
    xmid_ref[...] = x1 + ga2_ref[...] * ys

    hl = (h - hb.astype(F32)).astype(BF16)
    dn = (((1,), (1,)), ((), ()))
    lg = (lax.dot_general(wrh_ref[...], hb, dn, preferred_element_type=F32)
          + (lax.dot_general(wrl_ref[...], hb, dn, preferred_element_type=F32)
             + lax.dot_general(wrh_ref[...], hl, dn, preferred_element_type=F32)))
    score = jax.nn.sigmoid(lg)
    sel = score + br_ref[...]
    s = [sel[jj * N_GROUPS:(jj + 1) * N_GROUPS] for jj in range(EPG)]
    sr = [score[jj * N_GROUPS:(jj + 1) * N_GROUPS] for jj in range(EPG)]
    hi01, lo01 = jnp.maximum(s[0], s[1]), jnp.minimum(s[0], s[1])
    hi23, lo23 = jnp.maximum(s[2], s[3]), jnp.minimum(s[2], s[3])
    gscore = jnp.maximum(hi01, hi23) + jnp.maximum(jnp.minimum(hi01, hi23), jnp.maximum(lo01, lo23))
    gi = lax.broadcasted_iota(I32, (N_GROUPS, TM), 0)
    gmax = jnp.max(gscore, axis=0, keepdims=True)
    g_idx = jnp.min(jnp.where(gscore == gmax, gi, N_GROUPS), axis=0, keepdims=True)
    gsel = gi == g_idx
    v = [jnp.sum(jnp.where(gsel, s[jj], 0.0), axis=0, keepdims=True) for jj in range(EPG)]
    vr = [jnp.sum(jnp.where(gsel, sr[jj], 0.0), axis=0, keepdims=True) for jj in range(EPG)]

    def first_argmax(vals):
        best = vals[0]
        idx = jnp.zeros((1, TM), I32)
        for jj in range(1, EPG):
            upd = vals[jj] > best
            best = jnp.where(upd, vals[jj], best)
            idx = jnp.where(upd, jj, idx)
        return idx

    i1 = first_argmax(v)
    i2 = first_argmax([jnp.where(i1 == jj, -jnp.inf, v[jj]) for jj in range(EPG)])

    def pick(vals, idx):
        out = vals[0]
        for jj in range(1, EPG):
            out = jnp.where(idx == jj, vals[jj], out)
        return out

    w1 = pick(vr, i1)
    w2 = pick(vr, i2)
    wsum = w1 + w2
    w1 = w1 / wsum
    w2 = w2 / wsum

    ri = lax.broadcasted_iota(I32, (N_EXPERTS, TM), 0)
    oh1 = ri == i1 * N_GROUPS + g_idx
    oh2 = ri == i2 * N_GROUPS + g_idx
    oh = jnp.logical_or(oh1, oh2).astype(F32)
    pre = jnp.dot(oh.astype(BF16), tri_ref[...], preferred_element_type=F32) + carry_scr[:, 0:1]
    r1 = jnp.sum(jnp.where(oh1, pre, 0.0), axis=0, keepdims=True)
    r2 = jnp.sum(jnp.where(oh2, pre, 0.0), axis=0, keepdims=True)
    carry = carry_scr[...] + jnp.sum(oh, axis=1, keepdims=True)
    carry_scr[...] = carry
    cnt_ref[...] = carry

    zi = jnp.zeros((4, TM), I32)
    info_ref[...] = jnp.concatenate(
        [g_idx * EPG + i1, g_idx * EPG + i2, r1.astype(I32), r2.astype(I32), zi], axis=0)
    wrow = jnp.concatenate([w1, w2, jnp.zeros((126, TM), F32)], axis=0)
    wcol_ref[...] = wrow.T


def _merge(xs, fn, an, w_out_bf, modt, g2, wsg, wsu, wsd, wrh, wrl, br, tb):
    tile = lambda n: pl.BlockSpec((TM, n), lambda i: (i, 0))
    full = lambda a, b: pl.BlockSpec((a, b), lambda i: (0, 0))
    return pl.pallas_call(
        _merge_kernel,
        grid=(NT,),
        in_specs=[tile(D_MODEL), tile(D_F), tile(D_ATTN), full(D_MODEL, D_MODEL),
                  _mod_spec(2), _row_spec(D_MODEL), _mod_spec(4), _mod_spec(3), _mod_spec(5),
                  full(D_MODEL, D_SHARED), full(D_MODEL, D_SHARED), full(D_SHARED, D_MODEL),
                  full(N_EXPERTS, D_MODEL), full(N_EXPERTS, D_MODEL), full(N_EXPERTS, TM),
                  full(TM, TM)],
        out_specs=[tile(D_MODEL), pl.BlockSpec((TM, SUB, RL), lambda i: (i, 0, 0)),
                   pl.BlockSpec((8, TM), lambda i: (0, i)),
                   pl.BlockSpec((TM, 128), lambda i: (i, 0)),
                   pl.BlockSpec((N_EXPERTS, 128), lambda i: (0, 0))],
        out_shape=[jax.ShapeDtypeStruct((T, D_MODEL), F32), jax.ShapeDtypeStruct((T, SUB, RL), F32),
                   jax.ShapeDtypeStruct((8, T), I32), jax.ShapeDtypeStruct((T, 128), F32),
                   jax.ShapeDtypeStruct((N_EXPERTS, 128), F32)],
        scratch_shapes=[pltpu.VMEM((N_EXPERTS, 128), F32)],
        compiler_params=_cp(("arbitrary",)),
        name="merge_route",
    )(xs, fn, an, w_out_bf, modt, g2, modt, modt, modt, wsg, wsu, wsd, wrh, wrl, br, tb["tri"])


def _dispatch_kernel(dest_ref, h_hbm, xb_in, xb_hbm, sem):
    del xb_in
    i = pl.program_id(0)

    def copy(t, d):
        return pltpu.make_async_copy(h_hbm.at[pl.ds(t, 1)], xb_hbm.at[pl.ds(d, 1)], sem)

    def issue(r, c):
        t = i * TM + r
        copy(t, dest_ref[2 * t]).start()
        copy(t, dest_ref[2 * t + 1]).start()
        return c

    lax.fori_loop(0, TM, issue, 0)

    def drain(r, c):
        copy(0, 0).wait()
        copy(0, 0).wait()
        return c

    lax.fori_loop(0, TM, drain, 0)


def _dispatch(dest, h2, xb0):
    return pl.pallas_call(
        _dispatch_kernel,
        grid_spec=pltpu.PrefetchScalarGridSpec(
            num_scalar_prefetch=1, grid=(NT,),
            in_specs=[pl.BlockSpec(memory_space=pl.ANY), pl.BlockSpec(memory_space=pl.ANY)],
            out_specs=pl.BlockSpec(memory_space=pl.ANY),
            scratch_shapes=[pltpu.SemaphoreType.DMA(())]),
        out_shape=jax.ShapeDtypeStruct((ROWS, SUB, RL), F32),
        input_output_aliases={2: 0},
        compiler_params=pltpu.CompilerParams(dimension_semantics=("arbitrary",),
                                             has_side_effects=True),
        name="dispatch",
    )(dest, h2, xb0)


def _expert_kernel(be_ref, nu_ref, x_ref, wg_ref, wu_ref, wd_ref, y_ref, wgb, wub, wdb, xs_scr):
    b = pl.program_id(0)
    e = be_ref[b]
    prev = be_ref[jnp.maximum(b - 1, 0)]

    @pl.when(jnp.logical_or(b == 0, e != prev))
    def _():
        wgb[...] = wg_ref[...].astype(BF16)
        wub[...] = wu_ref[...].astype(BF16)
        wdb[...] = wd_ref[...].astype(BF16)

    @pl.when(b < nu_ref[0])
    def _():
        for s in range(SUB):
            xs_scr[:, s * RL:(s + 1) * RL] = x_ref[:, s, :].astype(BF16)
        x = xs_scr[...]
        a = jnp.dot(x, wgb[...], preferred_element_type=F32)
        u = jnp.dot(x, wub[...], preferred_element_type=F32)
        act = (_silu(a) * u).astype(BF16)
        y = jnp.dot(act, wdb[...], preferred_element_type=F32)
        for s in range(SUB):
            y_ref[:, s, :] = y[:, s * RL:(s + 1) * RL]

    @pl.when(b >= nu_ref[0])
    def _():
        y_ref[...] = jnp.zeros_like(y_ref)


def _experts(block_e, n_used, xb, w_g, w_u, w_d):
    return pl.pallas_call(
        _expert_kernel,
        grid_spec=pltpu.PrefetchScalarGridSpec(
            num_scalar_prefetch=2, grid=(NB,),
            in_specs=[pl.BlockSpec((BM, SUB, RL), lambda b, be, nu: (b, 0, 0)),
                      pl.BlockSpec((None, D_MODEL, D_EXPERT), lambda b, be, nu: (be[b], 0, 0)),
                      pl.BlockSpec((None, D_MODEL, D_EXPERT), lambda b, be, nu: (be[b], 0, 0)),
                      pl.BlockSpec((None, D_EXPERT, D_MODEL), lambda b, be, nu: (be[b], 0, 0))],
            out_specs=pl.BlockSpec((BM, SUB, RL), lambda b, be, nu: (b, 0, 0)),
            scratch_shapes=[pltpu.VMEM((D_MODEL, D_EXPERT), BF16),
                            pltpu.VMEM((D_MODEL, D_EXPERT), BF16),
                            pltpu.VMEM((D_EXPERT, D_MODEL), BF16),
                            pltpu.VMEM((BM, D_MODEL), BF16)]),
        out_shape=jax.ShapeDtypeStruct((ROWS, SUB, RL), F32),
        compiler_params=_cp(("arbitrary",)),
        name="experts",
    )(block_e, n_used, xb, w_g, w_u, w_d)


def _combine_kernel(dest_ref, y_hbm, xmid_ref, wcol_ref, ga2_ref, o_ref, b1, b2, sem):
    i = pl.program_id(0)

    def copy(d, buf, r):
        return pltpu.make_async_copy(y_hbm.at[pl.ds(d, 1)], buf.at[pl.ds(r, 1)], sem)

    def issue(r, c):
        t = i * TM + r
        copy(dest_ref[2 * t], b1, r).start()
        copy(dest_ref[2 * t + 1], b2, r).start()
        return c

    lax.fori_loop(0, TM, issue, 0)

    def drain(r, c):
        copy(0, b1, 0).wait()
        copy(0, b2, 0).wait()
        return c

    lax.fori_loop(0, TM, drain, 0)
    w1 = wcol_ref[:, 0:1]
    w2 = wcol_ref[:, 1:2]
    for s in range(SUB):
        cs = slice(s * RL, (s + 1) * RL)
        o_ref[:, cs] = xmid_ref[:, cs] + ga2_ref[:, cs] * (w1 * b1[:, s, :] + w2 * b2[:, s, :])


def _combine(dest, yb, xmid, wcol, modt, n_tiles):
    tile = lambda n: pl.BlockSpec((TM, n), lambda i, d: (i, 0))
    return pl.pallas_call(
        _combine_kernel,
        grid_spec=pltpu.PrefetchScalarGridSpec(
            num_scalar_prefetch=1, grid=(n_tiles,),
            in_specs=[pl.BlockSpec(memory_space=pl.ANY), tile(D_MODEL), tile(128),
                      pl.BlockSpec((None, None, 1, D_MODEL), lambda i, d: (i // NT_LAT, 5, 0, 0))],
            out_specs=tile(D_MODEL),
            scratch_shapes=[pltpu.VMEM((TM, SUB, RL), F32), pltpu.VMEM((TM, SUB, RL), F32),
                            pltpu.SemaphoreType.DMA(())]),
        out_shape=jax.ShapeDtypeStruct((n_tiles * TM, D_MODEL), F32),
        compiler_params=_cp(("arbitrary",)),
        name="combine",
    )(dest, yb, xmid, wcol, modt)


def _perm_experts(a):
    return a.reshape(a.shape[:-1] + (N_GROUPS, EPG)).swapaxes(-1, -2).reshape(a.shape)


def kernel(x, c, ctx, c_ctx, w_mod, b_mod, g_norm1, g_norm2, w_in, w_fourier, g_q, g_k, sink,
           g_branch_f, g_branch_a, w_out, w_router, b_router, w_exp_gate, w_exp_up, w_exp_down,
           w_sh_gate, w_sh_up, w_sh_down):
    assert x.shape == (1, SEQ, D_MODEL) and ctx.shape == (1, CTX, D_MODEL)
    tb = _tables()
    xs = jnp.concatenate([x[0], ctx[0]], axis=0)

    cc = jnp.zeros((8, D_MODEL), F32).at[0].set(c[0]).at[1].set(c_ctx)
    mods = _modulation(cc, w_mod, b_mod)

    wr_t = _perm_experts(w_router).T
    wrh = wr_t.astype(BF16)
    wrl = (wr_t - wrh.astype(F32)).astype(BF16)
    br = jnp.broadcast_to(_perm_experts(b_router)[:, None], (N_EXPERTS, TM)).astype(F32)

    for l in range(DEPTH):
        last = l == DEPTH - 1
        modt = mods[l, 0:2].reshape(2, 6, 1, D_MODEL)
        row = lambda a: a[l].reshape(1, -1)
        u, q, k, v = _project(xs, row(g_norm1), modt, w_in[l].astype(BF16), row(g_q), row(g_k), tb)
        an = _attention(q, k, v, sink[l], row(g_branch_a))
        fn = _fourier(u, w_fourier[l].astype(BF16), row(g_branch_f), tb)
        xmid, h2, info, wcol, cnt = _merge(
            xs, fn, an, w_out[l].astype(BF16), modt, row(g_norm2),
            w_sh_gate[l].astype(BF16), w_sh_up[l].astype(BF16), w_sh_down[l].astype(BF16),
            wrh, wrl, br, tb)

        counts = cnt[:, 0].astype(I32).reshape(EPG, N_GROUPS).T.reshape(N_EXPERTS)
        padded = ((counts + BM - 1) // BM) * BM
        pad_end = jnp.cumsum(padded)
        pad_start = pad_end - padded
        dest = (pad_start[info[0:2]] + info[2:4]).T.reshape(2 * T)
        block_e = jnp.minimum(
            jnp.searchsorted(pad_end, jnp.arange(NB, dtype=I32) * BM, side="right"),
            N_EXPERTS - 1).astype(I32)
        n_used = (pad_end[N_EXPERTS - 1:] // BM).astype(I32)

        xb = _dispatch(dest, h2, jnp.zeros((ROWS, SUB, RL), F32))
        yb = _experts(block_e, n_used, xb, w_exp_gate[l], w_exp_up[l], w_exp_down[l])
        xs = _combine(dest, yb, xmid, wcol, modt, NT_LAT if last else NT)
    return xs[None]
```

```python
import functools

import numpy as np
import jax
import jax.numpy as jnp
from jax import lax
from jax.experimental import pallas as pl
from jax.experimental.pallas import tpu as pltpu

F32 = jnp.float32
BF16 = jnp.bfloat16
I32 = jnp.int32

D_MODEL = 2048
SEQ = 8192
CTX = 256
T = SEQ + CTX
DEPTH = 4
GRID_W = 64
HEAD_DIM = 128
D_F = 512
N_FG = 4
N_HEADS = 12
N_KV = 4
Q_PER_KV = 3
D_ATTN = N_HEADS * HEAD_DIM
D_KV = N_KV * HEAD_DIM
D_IN = D_F + D_ATTN + 2 * D_KV
ROPE_BASE = 10000.0
N_EXPERTS = 32
N_GROUPS = 8
EPG = 4
D_EXPERT = 512
D_SHARED = 512
EPS = 1e-6
NEG = -1e30

TM = 256
NT = T // TM
NT_LAT = SEQ // TM
QB = 128
NQB = T // QB
NQB_LAT = SEQ // QB
BM = 256
NB = (2 * T + BM - 1) // BM + N_EXPERTS
ROWS = NB * BM
N1 = 128
N2 = 64
SUB = 8
RL = D_MODEL // SUB
PK = D_MODEL // 2 // SUB
VMEM_LIMIT = 56 * 1024 * 1024


def _cp(sem):
    return pltpu.CompilerParams(dimension_semantics=sem, vmem_limit_bytes=VMEM_LIMIT)


def _silu(a):
    return a * jax.nn.sigmoid(a)


@functools.lru_cache(maxsize=None)
def _tables():
    t = np.arange(SEQ)
    row = (t // GRID_W).astype(np.float64)
    col = (t % GRID_W).astype(np.float64)
    inv = ROPE_BASE ** (-np.arange(0, HEAD_DIM // 2, 2, dtype=np.float64) / (HEAD_DIM // 2))
    ar = row[:, None] * inv
    ac = col[:, None] * inv
    ang = np.concatenate([ar, ar, ac, ac], axis=-1)
    cos = np.concatenate([np.cos(ang), np.ones((CTX, HEAD_DIM))], axis=0)
    sin = np.concatenate([np.sin(ang), np.zeros((CTX, HEAD_DIM))], axis=0)
    first = (np.arange(HEAD_DIM) % 64) < 32
    rope_a = np.where(first[None, :], -sin, 0.0)
    rope_b = np.where(first[None, :], 0.0, sin)

    def cs(n):
        k = np.arange(n)
        a = 2.0 * np.pi * np.outer(k, k) / n
        return np.cos(a), np.sin(a)

    c64, s64 = cs(N2)
    c128, s128 = cs(N1)
    c256, s256 = cs(CTX)
    m1 = np.concatenate([c64, -s64], axis=0)
    m2 = np.concatenate([c128, s128], axis=0)
    mc = np.concatenate([c256, s256], axis=0)
    csch = np.concatenate([c128, s128], axis=0)
    a = 2.0 * np.pi * np.outer(np.arange(N1), np.arange(N2)) / SEQ
    twr = np.broadcast_to(np.cos(a)[:, :, None], (N1, N2, 128))
    twi = np.broadcast_to(-np.sin(a)[:, :, None], (N1, N2, 128))
    tri = (np.arange(TM)[:, None] < np.arange(TM)[None, :]).astype(np.float32)
    perm = np.zeros((GRP * SUB, GRP * SUB), np.float32)
    rr, ss = np.meshgrid(np.arange(GRP), np.arange(SUB), indexing="ij")
    perm[(ss * GRP + rr).ravel(), (rr * SUB + ss).ravel()] = 1.0
    return dict(
        cos=jnp.asarray(cos, F32), rope_a=jnp.asarray(rope_a, F32), rope_b=jnp.asarray(rope_b, F32),
        m1=jnp.asarray(m1, BF16), m2=jnp.asarray(m2, BF16), mc=jnp.asarray(mc, BF16),
        csch=jnp.asarray(csch, BF16), twr=jnp.asarray(twr, F32), twi=jnp.asarray(twi, F32),
        tri=jnp.asarray(tri, BF16), perm=jnp.asarray(perm, BF16))


MOD_TN = 1024


def _mod_kernel(cc_ref, w_ref, b_ref, o_ref):
    a = _silu(cc_ref[...])
    o_ref[...] = jnp.dot(a.astype(BF16), w_ref[...].astype(BF16),
                         preferred_element_type=F32) + b_ref[...]


def _modulation(cc, w_mod, b_mod):
    nl = w_mod.shape[0]
    n6 = w_mod.shape[2]
    return pl.pallas_call(
        _mod_kernel,
        grid=(nl, n6 // MOD_TN),
        in_specs=[pl.BlockSpec((8, D_MODEL), lambda l, j: (0, 0)),
                  pl.BlockSpec((None, D_MODEL, MOD_TN), lambda l, j: (l, 0, j)),
                  pl.BlockSpec((None, 1, MOD_TN), lambda l, j: (l, 0, j))],
        out_specs=pl.BlockSpec((None, 8, MOD_TN), lambda l, j: (l, 0, j)),
        out_shape=jax.ShapeDtypeStruct((nl, 8, n6), F32),
        compiler_params=_cp(("arbitrary", "arbitrary")),
        name="modulation",
    )(cc, w_mod, b_mod.reshape(nl, 1, n6))


def _mod_spec(chunk):
    return pl.BlockSpec((None, None, 1, D_MODEL), lambda i: (i // NT_LAT, chunk, 0, 0))


def _row_spec(n):
    return pl.BlockSpec((1, n), lambda i: (0, 0))


PCW = 256


def _proj_kernel(x_ref, g_ref, sc_ref, sh_ref, w_ref, gq_ref, gk_ref, cos_ref, ra_ref, rb_ref,
                 u_ref, q_ref, k_ref, v_ref, hb_ref):
    x = x_ref[...]
    ms = jnp.mean(x * x, axis=-1, keepdims=True)
    h = x * lax.rsqrt(ms + EPS) * g_ref[...]
    h = h * (1.0 + sc_ref[...]) + sh_ref[...]
    hb_ref[...] = h.astype(BF16)
    cos = cos_ref[...]
    ra = ra_ref[...]
    rb = rb_ref[...]

    def head(t, g):
        m = jnp.mean(t * t, axis=-1, keepdims=True)
        t = t * lax.rsqrt(m + EPS) * g
        return t * cos + pltpu.roll(t, 96, 1) * ra + pltpu.roll(t, 32, 1) * rb

    scale = HEAD_DIM ** -0.5
    for c in range(D_IN // PCW):
        col = c * PCW
        p = jnp.dot(hb_ref[...], w_ref[:, col:col + PCW], preferred_element_type=F32)
        if col < D_F:
            u_ref[:, col:col + PCW] = p.astype(BF16)
        elif col < D_F + D_ATTN:
            o = col - D_F
            for j in range(PCW // HEAD_DIM):
                t = head(p[:, j * HEAD_DIM:(j + 1) * HEAD_DIM], gq_ref[...]) * scale
                q_ref[:, o + j * HEAD_DIM:o + (j + 1) * HEAD_DIM] = t.astype(BF16)
        elif col < D_F + D_ATTN + D_KV:
            o = col - D_F - D_ATTN
            for j in range(PCW // HEAD_DIM):
                t = head(p[:, j * HEAD_DIM:(j + 1) * HEAD_DIM], gk_ref[...])
                k_ref[:, o + j * HEAD_DIM:o + (j + 1) * HEAD_DIM] = t.astype(BF16)
        else:
            o = col - D_F - D_ATTN - D_KV
            v_ref[:, o:o + PCW] = p.astype(BF16)


def _project(xs, g1, modt, w_in_bf, gq, gk, tb):
    tile = lambda n: pl.BlockSpec((TM, n), lambda i: (i, 0))
    return pl.pallas_call(
        _proj_kernel,
        grid=(NT,),
        in_specs=[tile(D_MODEL), _row_spec(D_MODEL), _mod_spec(1), _mod_spec(0),
                  pl.BlockSpec((D_MODEL, D_IN), lambda i: (0, 0)),
                  _row_spec(HEAD_DIM), _row_spec(HEAD_DIM),
                  tile(HEAD_DIM), tile(HEAD_DIM), tile(HEAD_DIM)],
        out_specs=[tile(D_F), tile(D_ATTN), tile(D_KV), tile(D_KV)],
        out_shape=[jax.ShapeDtypeStruct((T, D_F), BF16), jax.ShapeDtypeStruct((T, D_ATTN), BF16),
                   jax.ShapeDtypeStruct((T, D_KV), BF16), jax.ShapeDtypeStruct((T, D_KV), BF16)],
        scratch_shapes=[pltpu.VMEM((TM, D_MODEL), BF16)],
        compiler_params=_cp(("arbitrary",)),
        name="norm_proj",
    )(xs, g1, modt, modt, w_in_bf, gq, gk, tb["cos"], tb["rope_a"], tb["rope_b"])


def _attn_kernel(sink_ref, q_ref, kp_ref, kc_ref, kn_ref, vp_ref, vc_ref, vn_ref, kx_ref, vx_ref,
                 ga_ref, o_ref, a_scr):
    b = pl.program_id(0)
    lat = b < NQB_LAT
    f_prev = jnp.logical_and(lat, b > 0).astype(I32)
    f_cur = lat.astype(I32)
    f_next = (b < NQB_LAT - 1).astype(I32)
    n3 = Q_PER_KV * QB
    r = lax.broadcasted_iota(I32, (n3, 3 * QB), 0) & (QB - 1)
    cidx = lax.broadcasted_iota(I32, (n3, 3 * QB), 1)
    blk = cidx >> 7
    j = cidx & (QB - 1)
    ok = jnp.where(blk == 0, (j >= r).astype(I32) * f_prev,
                   jnp.where(blk == 1, f_cur, (j <= r).astype(I32) * f_next))
    valid = ok > 0
    dn = (((1,), (1,)), ((), ()))
    for hk in range(N_KV):
        ks = slice(hk * HEAD_DIM, (hk + 1) * HEAD_DIM)
        qs = jnp.concatenate(
            [q_ref[:, (hk * Q_PER_KV + g) * HEAD_DIM:(hk * Q_PER_KV + g + 1) * HEAD_DIM]
             for g in range(Q_PER_KV)], axis=0)
        kw = jnp.concatenate([kp_ref[:, ks], kc_ref[:, ks], kn_ref[:, ks]], axis=0)
        vw = jnp.concatenate([vp_ref[:, ks], vc_ref[:, ks], vn_ref[:, ks]], axis=0)
        s = lax.dot_general(qs, kw, dn, preferred_element_type=F32)
        s = jnp.where(valid, s, NEG)
        sx = lax.dot_general(qs, kx_ref[:, ks], dn, preferred_element_type=F32)
        sk = jnp.concatenate(
            [jnp.full((QB, 1), sink_ref[hk * Q_PER_KV + g], F32) for g in range(Q_PER_KV)], axis=0)
        m = jnp.maximum(jnp.maximum(jnp.max(s, axis=-1, keepdims=True),
                                    jnp.max(sx, axis=-1, keepdims=True)), sk)
        e = jnp.exp(s - m)
        ex = jnp.exp(sx - m)
        den = (jnp.sum(e, axis=-1, keepdims=True) + jnp.sum(ex, axis=-1, keepdims=True)
               + jnp.exp(sk - m))
        o = (jnp.dot(e.astype(BF16), vw, preferred_element_type=F32)
             + jnp.dot(ex.astype(BF16), vx_ref[:, ks], preferred_element_type=F32)) / den
        for g in range(Q_PER_KV):
            h = hk * Q_PER_KV + g
            a_scr[:, h * HEAD_DIM:(h + 1) * HEAD_DIM] = o[g * QB:(g + 1) * QB]
    a = a_scr[...]
    ms = jnp.mean(a * a, axis=-1, keepdims=True)
    o_ref[...] = (a * lax.rsqrt(ms + EPS) * ga_ref[...]).astype(BF16)


def _attention(q, k, v, sink, g_a):
    kv = lambda f: pl.BlockSpec((QB, D_KV), f)
    prev = lambda b: (jnp.maximum(b - 1, 0), 0)
    cur = lambda b: (b, 0)
    nxt = lambda b: (jnp.minimum(b + 1, NQB - 1), 0)
    ctx = pl.BlockSpec((CTX, D_KV), lambda b: (SEQ // CTX, 0))
    return pl.pallas_call(
        _attn_kernel,
        grid=(NQB,),
        in_specs=[pl.BlockSpec(memory_space=pltpu.SMEM),
                  pl.BlockSpec((QB, D_ATTN), cur),
                  kv(prev), kv(cur), kv(nxt), kv(prev), kv(cur), kv(nxt), ctx, ctx,
                  _row_spec(D_ATTN)],
        out_specs=pl.BlockSpec((QB, D_ATTN), cur),
        out_shape=jax.ShapeDtypeStruct((T, D_ATTN), BF16),
        scratch_shapes=[pltpu.VMEM((QB, D_ATTN), F32)],
        compiler_params=_cp(("arbitrary",)),
        name="attention",
    )(sink, q, k, k, k, v, v, v, k, v, g_a)


F1_J = 8
F2_K = 4


def _f1_kernel(u_ref, m1_ref, twr_ref, twi_ref, z_ref):
    g = jnp.dot(m1_ref[...], u_ref[...], preferred_element_type=F32)
    for jl in range(F1_J):
        twr = twr_ref[jl]
        twi = twi_ref[jl]
        for lt in range(D_F // 128):
            c0 = jl * D_F + lt * 128
            gr = g[0:N2, c0:c0 + 128]
            gi = g[N2:2 * N2, c0:c0 + 128]
            z_ref[jl, 0:N2, lt * 128:(lt + 1) * 128] = (gr * twr - gi * twi).astype(BF16)
            z_ref[jl, N2:2 * N2, lt * 128:(lt + 1) * 128] = (gr * twi + gi * twr).astype(BF16)


def _f2_kernel(zr_ref, zi_ref, m2_ref, y_ref):
    pr = jnp.dot(m2_ref[...], zr_ref[...], preferred_element_type=F32)
    pi = jnp.dot(m2_ref[...], zi_ref[...], preferred_element_type=F32)
    yr = pr[0:N1] + pi[N1:2 * N1]
    yi = pi[0:N1] - pr[N1:2 * N1]
    for kl in range(F2_K):
        y_ref[:, kl * 2 * D_F:kl * 2 * D_F + D_F] = yr[:, kl * D_F:(kl + 1) * D_F].astype(BF16)
        y_ref[:, kl * 2 * D_F + D_F:(kl + 1) * 2 * D_F] = yi[:, kl * D_F:(kl + 1) * D_F].astype(BF16)


def _f3_kernel(y_ref, uc_ref, cs_ref, mc_ref, wf_ref, gf_ref, o_ref):
    i = pl.program_id(0)

    def finish(parts, scale):
        f = jnp.concatenate(parts, axis=1) * scale
        t = jnp.dot(f.astype(BF16), wf_ref[...], preferred_element_type=F32)
        ms = jnp.mean(t * t, axis=-1, keepdims=True)
        o_ref[...] = (t * lax.rsqrt(ms + EPS) * gf_ref[...]).astype(BF16)

    @pl.when(i < NT_LAT)
    def _():
        parts = []
        for g in range(N_FG):
            lhs = jnp.concatenate([y_ref[:, g * 128:(g + 1) * 128],
                                   y_ref[:, D_F + g * 128:D_F + (g + 1) * 128]], axis=1)
            parts.append(jnp.dot(lhs, cs_ref[...], preferred_element_type=F32))
        finish(parts, float((SEQ * 128) ** -0.5))

    @pl.when(i == NT_LAT)
    def _():
        pq = jnp.dot(mc_ref[...], uc_ref[...], preferred_element_type=F32)
        parts = []
        for g in range(N_FG):
            lhs = jnp.concatenate([pq[0:CTX, g * 128:(g + 1) * 128],
                                   -pq[CTX:2 * CTX, g * 128:(g + 1) * 128]], axis=1)
            parts.append(jnp.dot(lhs.astype(BF16), cs_ref[...], preferred_element_type=F32))
        finish(parts, float((CTX * 128) ** -0.5))


def _fourier(u, w_f_bf, g_f, tb):
    u2 = u.reshape(T // N1, N1 * D_F)
    z = pl.pallas_call(
        _f1_kernel,
        grid=(N1 // F1_J,),
        in_specs=[pl.BlockSpec((N2, F1_J * D_F), lambda c: (0, c)),
                  pl.BlockSpec((2 * N2, N2), lambda c: (0, 0)),
                  pl.BlockSpec((F1_J, N2, 128), lambda c: (c, 0, 0)),
                  pl.BlockSpec((F1_J, N2, 128), lambda c: (c, 0, 0))],
        out_specs=pl.BlockSpec((F1_J, 2 * N2, D_F), lambda c: (c, 0, 0)),
        out_shape=jax.ShapeDtypeStruct((N1, 2 * N2, D_F), BF16),
        compiler_params=_cp(("arbitrary",)),
        name="fourier_s1",
    )(u2, tb["m1"], tb["twr"], tb["twi"])
    z2 = z.reshape(N1, 2 * N2 * D_F)
    nk = N2 // F2_K
    y = pl.pallas_call(
        _f2_kernel,
        grid=(nk,),
        in_specs=[pl.BlockSpec((N1, F2_K * D_F), lambda k: (0, k)),
                  pl.BlockSpec((N1, F2_K * D_F), lambda k: (0, nk + k)),
                  pl.BlockSpec((2 * N1, N1), lambda k: (0, 0))],
        out_specs=pl.BlockSpec((N1, F2_K * 2 * D_F), lambda k: (0, k)),
        out_shape=jax.ShapeDtypeStruct((N1, N2 * 2 * D_F), BF16),
        compiler_params=_cp(("arbitrary",)),
        name="fourier_s2",
    )(z2, z2, tb["m2"])
    y2 = y.reshape(SEQ, 2 * D_F)
    return pl.pallas_call(
        _f3_kernel,
        grid=(NT,),
        in_specs=[pl.BlockSpec((TM, 2 * D_F), lambda i: (jnp.minimum(i, NT_LAT - 1), 0)),
                  pl.BlockSpec((CTX, D_F), lambda i: (SEQ // CTX, 0)),
                  pl.BlockSpec((2 * 128, 128), lambda i: (0, 0)),
                  pl.BlockSpec((2 * CTX, CTX), lambda i: (0, 0)),
                  pl.BlockSpec((D_F, D_F), lambda i: (0, 0)),
                  _row_spec(D_F)],
        out_specs=pl.BlockSpec((TM, D_F), lambda i: (i, 0)),
        out_shape=jax.ShapeDtypeStruct((T, D_F), BF16),
        compiler_params=_cp(("arbitrary",)),
        name="fourier_s3",
    )(y2, u, tb["csch"], tb["mc"], w_f_bf, g_f)


def _merge_kernel(x_ref, fn_ref, an_ref, wo_ref, ga1_ref, g2_ref, sc2_ref, sh2_ref, ga2_ref,
                  wsg_ref, wsu_ref, wsd_ref, wrh_ref, wrl_ref, br_ref, tri_ref,
                  xmid_ref, h2_ref, info_ref, wcol_ref, cnt_ref, carry_scr):
    i = pl.program_id(0)

    @pl.when(i == 0)
    def _():
        carry_scr[...] = jnp.zeros_like(carry_scr)

    m = (jnp.dot(fn_ref[...], wo_ref[0:D_F, :], preferred_element_type=F32)
         + jnp.dot(an_ref[...], wo_ref[D_F:D_MODEL, :], preferred_element_type=F32))
    x1 = x_ref[...] + ga1_ref[...] * m
    ms = jnp.mean(x1 * x1, axis=-1, keepdims=True)
    h = x1 * lax.rsqrt(ms + EPS) * g2_ref[...]
    h = h * (1.0 + sc2_ref[...]) + sh2_ref[...]
    hb = h.astype(BF16)
    hp = _pack_bf16_pairs(h)
    for s in range(SUB):
        h2_ref[:, s, :] = hp[:, s * PK:(s + 1) * PK]

    a = jnp.dot(hb, wsg_ref[...], preferred_element_type=F32)
    u = jnp.dot(hb, wsu_ref[...], preferred_element_type=F32)
    act = (_silu(a) * u).astype(BF16)
    ys = jnp.dot(act, wsd_ref[...], preferred_element_type=F32)
    xmid_ref[...] = x1 + ga2_ref[...] * ys

    hl = (h - hb.astype(F32)).astype(BF16)
    dn = (((1,), (1,)), ((), ()))
    lg = (lax.dot_general(wrh_ref[...], hb, dn, preferred_element_type=F32)
          + (lax.dot_general(wrl_ref[...], hb, dn, preferred_element_type=F32)
             + lax.dot_general(wrh_ref[...], hl, dn, preferred_element_type=F32)))
    score = jax.nn.sigmoid(lg)
    sel = score + br_ref[...]
    s = [sel[jj * N_GROUPS:(jj + 1) * N_GROUPS] for jj in range(EPG)]
    sr = [score[jj * N_GROUPS:(jj + 1) * N_GROUPS] for jj in range(EPG)]
    hi01, lo01 = jnp.maximum(s[0], s[1]), jnp.minimum(s[0], s[1])
    hi23, lo23 = jnp.maximum(s[2], s[3]), jnp.minimum(s[2], s[3])
    gscore = jnp.maximum(hi01, hi23) + jnp.maximum(jnp.minimum(hi01, hi23), jnp.maximum(lo01, lo23))
    gi = lax.broadcasted_iota(I32, (N_GROUPS, TM), 0)
    gmax = jnp.max(gscore, axis=0, keepdims=True)
    g_idx = jnp.min(jnp.where(gscore == gmax, gi, N_GROUPS), axis=0, keepdims=True)
    gsel = gi == g_idx
    v = [jnp.sum(jnp.where(gsel, s[jj], 0.0), axis=0, keepdims=True) for jj in range(EPG)]
    vr = [jnp.sum(jnp.where(gsel, sr[jj], 0.0), axis=0, keepdims=True) for jj in range(EPG)]

    def first_argmax(vals):
        best = vals[0]
        idx = jnp.zeros((1, TM), I32)
        for jj in range(1, EPG):
            upd = vals[jj] > best
            best = jnp.where(upd, vals[jj], best)
            idx = jnp.where(upd, jj, idx)
        return idx

    i1 = first_argmax(v)
    i2 = first_argmax([jnp.where(i1 == jj, -jnp.inf, v[jj]) for jj in range(EPG)])

    def pick(vals, idx):
        out = vals[0]
        for jj in range(1, EPG):
            out = jnp.where(idx == jj, vals[jj], out)
        return out

    w1 = pick(vr, i1)
    w2 = pick(vr, i2)
    wsum = w1 + w2
    w1 = w1 / wsum
    w2 = w2 / wsum

    ri = lax.broadcasted_iota(I32, (N_EXPERTS, TM), 0)
    oh1 = ri == i1 * N_GROUPS + g_idx
    oh2 = ri == i2 * N_GROUPS + g_idx
    oh = jnp.logical_or(oh1, oh2).astype(F32)
    pre = jnp.dot(oh.astype(BF16), tri_ref[...], preferred_element_type=F32) + carry_scr[:, 0:1]
    r1 = jnp.sum(jnp.where(oh1, pre, 0.0), axis=0, keepdims=True)
    r2 = jnp.sum(jnp.where(oh2, pre, 0.0), axis=0, keepdims=True)
    carry = carry_scr[...] + jnp.sum(oh, axis=1, keepdims=True)
    carry_scr[...] = carry
    cnt_ref[...] = carry

    zi = jnp.zeros((4, TM), I32)
    info_ref[...] = jnp.concatenate(
        [g_idx * EPG + i1, g_idx * EPG + i2, r1.astype(I32), r2.astype(I32), zi], axis=0)
    wrow = jnp.concatenate([w1, w2, jnp.zeros((126, TM), F32)], axis=0)
    wcol_ref[...] = wrow.T


def _merge(xs, fn, an, w_out_bf, modt, g2, wsg, wsu, wsd, wrh, wrl, br, tb):
    tile = lambda n: pl.BlockSpec((TM, n), lambda i: (i, 0))
    full = lambda a, b: pl.BlockSpec((a, b), lambda i: (0, 0))
    return pl.pallas_call(
        _merge_kernel,
        grid=(NT,),
        in_specs=[tile(D_MODEL), tile(D_F), tile(D_ATTN), full(D_MODEL, D_MODEL),
                  _mod_spec(2), _row_spec(D_MODEL), _mod_spec(4), _mod_spec(3), _mod_spec(5),
                  full(D_MODEL, D_SHARED), full(D_MODEL, D_SHARED), full(D_SHARED, D_MODEL),
                  full(N_EXPERTS, D_MODEL), full(N_EXPERTS, D_MODEL), full(N_EXPERTS, TM),
                  full(TM, TM)],
        out_specs=[tile(D_MODEL), pl.BlockSpec((TM, SUB, PK), lambda i: (i, 0, 0)),
                   pl.BlockSpec((8, TM), lambda i: (0, i)),
                   pl.BlockSpec((TM, 128), lambda i: (i, 0)),
                   pl.BlockSpec((N_EXPERTS, 128), lambda i: (0, 0))],
        out_shape=[jax.ShapeDtypeStruct((T, D_MODEL), F32), jax.ShapeDtypeStruct((T, SUB, PK), I32),
                   jax.ShapeDtypeStruct((8, T), I32), jax.ShapeDtypeStruct((T, 128), F32),
                   jax.ShapeDtypeStruct((N_EXPERTS, 128), F32)],
        scratch_shapes=[pltpu.VMEM((N_EXPERTS, 128), F32)],
        compiler_params=_cp(("arbitrary",)),
        name="merge_route",
    )(xs, fn, an, w_out_bf, modt, g2, modt, modt, modt, wsg, wsu, wsd, wrh, wrl, br, tb["tri"])


def _dispatch_kernel(dest_ref, pend_ref, padded_ref, h_ref, xb_hbm, zero_scr, sem, zsem):
    i = pl.program_id(0)

    def zero_copy(row0):
        return pltpu.make_async_copy(zero_scr, xb_hbm.at[pl.ds(row0, BM)], zsem)

    @pl.when(i == 0)
    def _():
        zero_scr[...] = jnp.zeros_like(zero_scr)
        n_used = pend_ref[N_EXPERTS - 1] // BM
        for e in range(N_EXPERTS):
            @pl.when(padded_ref[e] > 0)
            def _():
                zero_copy(pend_ref[e] - BM).start()
        for blk in range(NB - N_EXPERTS, NB):
            @pl.when(blk >= n_used)
            def _():
                zero_copy(blk * BM).start()
        for e in range(N_EXPERTS):
            @pl.when(padded_ref[e] > 0)
            def _():
                zero_copy(0).wait()
        for blk in range(NB - N_EXPERTS, NB):
            @pl.when(blk >= n_used)
            def _():
                zero_copy(0).wait()

    def copy(r, d):
        return pltpu.make_async_copy(h_ref.at[pl.ds(r, 1)], xb_hbm.at[pl.ds(d, 1)], sem)

    def issue(r, c):
        t = i * TM + r
        copy(r, dest_ref[2 * t]).start()
        copy(r, dest_ref[2 * t + 1]).start()
        return c

    lax.fori_loop(0, TM, issue, 0)

    def drain(r, c):
        copy(0, 0).wait()
        copy(0, 0).wait()
        return c

    lax.fori_loop(0, TM, drain, 0)


def _dispatch(dest, pad_end, padded, h2):
    return pl.pallas_call(
        _dispatch_kernel,
        grid_spec=pltpu.PrefetchScalarGridSpec(
            num_scalar_prefetch=3, grid=(NT,),
            in_specs=[pl.BlockSpec((TM, SUB, PK), lambda i, d, pe, pd: (i, 0, 0))],
            out_specs=pl.BlockSpec(memory_space=pl.ANY),
            scratch_shapes=[pltpu.VMEM((BM, SUB, PK), I32), pltpu.SemaphoreType.DMA(()),
                            pltpu.SemaphoreType.DMA(())]),
        out_shape=jax.ShapeDtypeStruct((ROWS, SUB, PK), I32),
        compiler_params=pltpu.CompilerParams(dimension_semantics=("arbitrary",),
                                             has_side_effects=True),
        name="dispatch",
    )(dest, pad_end, padded, h2)


def _pack_bf16_pairs(h):
    bits = lax.bitcast_convert_type(h.astype(BF16).astype(F32), I32)
    half = h.shape[1] // 2
    return (bits[:, :half] & -65536) | lax.shift_right_logical(bits[:, half:], 16)


GRP = 32


def _expert_kernel(be_ref, nu_ref, x_ref, perm_ref, wg_ref, wu_ref, wd_ref, y_ref,
                   wgb, wub, wdb, xs_scr):
    b = pl.program_id(0)
    e = be_ref[b]
    prev = be_ref[jnp.maximum(b - 1, 0)]

    @pl.when(jnp.logical_or(b == 0, e != prev))
    def _():
        wgb[...] = wg_ref[...].astype(BF16)
        wub[...] = wu_ref[...].astype(BF16)
        wdb[...] = wd_ref[...].astype(BF16)

    @pl.when(b < nu_ref[0])
    def _():
        half = D_MODEL // 2
        for grp in range(BM // GRP):
            w = x_ref[grp * GRP:(grp + 1) * GRP].reshape(GRP * SUB, PK)
            hi = lax.bitcast_convert_type(w & -65536, F32).astype(BF16)
            lo = lax.bitcast_convert_type(w << 16, F32).astype(BF16)
            g = jnp.concatenate([hi, lo], axis=1)
            r = jnp.dot(perm_ref[...], g, preferred_element_type=F32).astype(BF16)
            rows = slice(grp * GRP, (grp + 1) * GRP)
            for s in range(SUB):
                xs_scr[rows, s * PK:(s + 1) * PK] = r[s * GRP:(s + 1) * GRP, 0:PK]
                xs_scr[rows, half + s * PK:half + (s + 1) * PK] = r[s * GRP:(s + 1) * GRP, PK:2 * PK]
        x = xs_scr[...]
        a = jnp.dot(x, wgb[...], preferred_element_type=F32)
        u = jnp.dot(x, wub[...], preferred_element_type=F32)
        act = (_silu(a) * u).astype(BF16)
        y = jnp.dot(act, wdb[...], preferred_element_type=F32)
        for s in range(SUB):
            y_ref[:, s, :] = y[:, s * RL:(s + 1) * RL]

    @pl.when(b >= nu_ref[0])
    def _():
        y_ref[...] = jnp.zeros_like(y_ref)


def _experts(l, block_e, n_used, xb, perm, w_g, w_u, w_d):
    return pl.pallas_call(
        _expert_kernel,
        grid_spec=pltpu.PrefetchScalarGridSpec(
            num_scalar_prefetch=2, grid=(NB,),
            in_specs=[pl.BlockSpec((BM, SUB, PK), lambda b, be, nu: (jnp.minimum(b, nu[0] - 1), 0, 0)),
                      pl.BlockSpec((GRP * SUB, GRP * SUB), lambda b, be, nu: (0, 0)),
                      pl.BlockSpec((None, None, D_MODEL, D_EXPERT), lambda b, be, nu: (l, be[b], 0, 0)),
                      pl.BlockSpec((None, None, D_MODEL, D_EXPERT), lambda b, be, nu: (l, be[b], 0, 0)),
                      pl.BlockSpec((None, None, D_EXPERT, D_MODEL), lambda b, be, nu: (l, be[b], 0, 0))],
            out_specs=pl.BlockSpec((BM, SUB, RL), lambda b, be, nu: (b, 0, 0)),
            scratch_shapes=[pltpu.VMEM((D_MODEL, D_EXPERT), BF16),
                            pltpu.VMEM((D_MODEL, D_EXPERT), BF16),
                            pltpu.VMEM((D_EXPERT, D_MODEL), BF16),
                            pltpu.VMEM((BM, D_MODEL), BF16)]),
        out_shape=jax.ShapeDtypeStruct((ROWS, SUB, RL), F32),
        compiler_params=_cp(("arbitrary",)),
        name="experts",
    )(block_e, n_used, xb, perm, w_g, w_u, w_d)


def _combine_kernel(dest_ref, y_hbm, xmid_ref, wcol_ref, ga2_ref, o_ref, b1, b2, sem):
    i = pl.program_id(0)

    def copy(d, buf, r):
        return pltpu.make_async_copy(y_hbm.at[pl.ds(d, 1)], buf.at[pl.ds(r, 1)], sem)

    def issue(r, c):
        t = i * TM + r
        copy(dest_ref[2 * t], b1, r).start()
        copy(dest_ref[2 * t + 1], b2, r).start()
        return c

    lax.fori_loop(0, TM, issue, 0)

    def drain(r, c):
        copy(0, b1, 0).wait()
        copy(0, b2, 0).wait()
        return c

    lax.fori_loop(0, TM, drain, 0)
    w1 = wcol_ref[:, 0:1]
    w2 = wcol_ref[:, 1:2]
    for s in range(SUB):
        cs = slice(s * RL, (s + 1) * RL)
        o_ref[:, cs] = xmid_ref[:, cs] + ga2_ref[:, cs] * (w1 * b1[:, s, :] + w2 * b2[:, s, :])


def _combine(dest, yb, xmid, wcol, modt, n_tiles):
    tile = lambda n: pl.BlockSpec((TM, n), lambda i, d: (i, 0))
    return pl.pallas_call(
        _combine_kernel,
        grid_spec=pltpu.PrefetchScalarGridSpec(
            num_scalar_prefetch=1, grid=(n_tiles,),
            in_specs=[pl.BlockSpec(memory_space=pl.ANY), tile(D_MODEL), tile(128),
                      pl.BlockSpec((None, None, 1, D_MODEL), lambda i, d: (i // NT_LAT, 5, 0, 0))],
            out_specs=tile(D_MODEL),
            scratch_shapes=[pltpu.VMEM((TM, SUB, RL), F32), pltpu.VMEM((TM, SUB, RL), F32),
                            pltpu.SemaphoreType.DMA(())]),
        out_shape=jax.ShapeDtypeStruct((n_tiles * TM, D_MODEL), F32),
        compiler_params=_cp(("arbitrary",)),
        name="combine",
    )(dest, yb, xmid, wcol, modt)


def _perm_experts(a):
    return a.reshape(a.shape[:-1] + (N_GROUPS, EPG)).swapaxes(-1, -2).reshape(a.shape)


def kernel(x, c, ctx, c_ctx, w_mod, b_mod, g_norm1, g_norm2, w_in, w_fourier, g_q, g_k, sink,
           g_branch_f, g_branch_a, w_out, w_router, b_router, w_exp_gate, w_exp_up, w_exp_down,
           w_sh_gate, w_sh_up, w_sh_down):
    assert x.shape == (1, SEQ, D_MODEL) and ctx.shape == (1, CTX, D_MODEL)
    tb = _tables()
    xs = jnp.concatenate([x[0], ctx[0]], axis=0)

    cc = jnp.zeros((8, D_MODEL), F32).at[0].set(c[0]).at[1].set(c_ctx)
    mods = _modulation(cc, w_mod, b_mod)

    wr_t = _perm_experts(w_router).T
    wrh = wr_t.astype(BF16)
    wrl = (wr_t - wrh.astype(F32)).astype(BF16)
    br = jnp.broadcast_to(_perm_experts(b_router)[:, None], (N_EXPERTS, TM)).astype(F32)

    for l in range(DEPTH):
        last = l == DEPTH - 1
        modt = mods[l, 0:2].reshape(2, 6, 1, D_MODEL)
        row = lambda a: a[l].reshape(1, -1)
        u, q, k, v = _project(xs, row(g_norm1), modt, w_in[l].astype(BF16), row(g_q), row(g_k), tb)
        an = _attention(q, k, v, sink[l], row(g_branch_a))
        fn = _fourier(u, w_fourier[l].astype(BF16), row(g_branch_f), tb)
        xmid, h2, info, wcol, cnt = _merge(
            xs, fn, an, w_out[l].astype(BF16), modt, row(g_norm2),
            w_sh_gate[l].astype(BF16), w_sh_up[l].astype(BF16), w_sh_down[l].astype(BF16),
            wrh, wrl, br, tb)

        xs = _routed(l, info, cnt, h2, xmid, wcol, modt, w_exp_gate, w_exp_up, w_exp_down, tb,
                     NT_LAT if last else NT)
    return xs[None]


def _routed(l, info, cnt, h2, xmid, wcol, modt, w_g, w_u, w_d, tb, n_tiles):
    counts = cnt[:, 0].astype(I32).reshape(EPG, N_GROUPS).T.reshape(N_EXPERTS)
    padded = ((counts + BM - 1) // BM) * BM
    pad_end = jnp.cumsum(padded).astype(I32)
    pad_start = pad_end - padded
    start = jnp.sum(jnp.where(info[0:2, :, None] == jnp.arange(N_EXPERTS, dtype=I32),
                              pad_start, 0), axis=-1)
    dest = (start + info[2:4]).T.reshape(2 * T)
    block_e = jnp.minimum(
        jnp.searchsorted(pad_end, jnp.arange(NB, dtype=I32) * BM, side="right"),
        N_EXPERTS - 1).astype(I32)
    n_used = pad_end[N_EXPERTS - 1:] // BM
    xb = _dispatch(dest, pad_end, padded, h2)
    yb = _experts(l, block_e, n_used, xb, tb["perm"], w_g, w_u, w_d)
    return _combine(dest, yb, xmid, wcol, modt, n_tiles)
```

```python
import functools

import numpy as np
import jax
import jax.numpy as jnp
from jax import lax
from jax.experimental import pallas as pl
from jax.experimental.pallas import tpu as pltpu

F32 = jnp.float32
BF16 = jnp.bfloat16
I32 = jnp.int32

D_MODEL = 2048
SEQ = 8192
CTX = 256
T = SEQ + CTX
DEPTH = 4
GRID_W = 64
HEAD_DIM = 128
D_F = 512
N_FG = 4
N_HEADS = 12
N_KV = 4
Q_PER_KV = 3
D_ATTN = N_HEADS * HEAD_DIM
D_KV = N_KV * HEAD_DIM
D_IN = D_F + D_ATTN + 2 * D_KV
ROPE_BASE = 10000.0
N_EXPERTS = 32
N_GROUPS = 8
EPG = 4
D_EXPERT = 512
D_SHARED = 512
EPS = 1e-6
NEG = -1e30

TM = 256
NT = T // TM
NT_LAT = SEQ // TM
QB = 128
NQB = T // QB
NQB_LAT = SEQ // QB
BM = 256
NB = (2 * T + BM - 1) // BM + N_EXPERTS
ROWS = NB * BM
N1 = 128
N2 = 64
SUB = 8
RL = D_MODEL // SUB
PK = D_MODEL // 2 // SUB
VMEM_LIMIT = 56 * 1024 * 1024


def _cp(sem):
    return pltpu.CompilerParams(dimension_semantics=sem, vmem_limit_bytes=VMEM_LIMIT)


def _silu(a):
    return a * jax.nn.sigmoid(a)


@functools.lru_cache(maxsize=None)
def _tables():
    t = np.arange(SEQ)
    row = (t // GRID_W).astype(np.float64)
    col = (t % GRID_W).astype(np.float64)
    inv = ROPE_BASE ** (-np.arange(0, HEAD_DIM // 2, 2, dtype=np.float64) / (HEAD_DIM // 2))
    ar = row[:, None] * inv
    ac = col[:, None] * inv
    ang = np.concatenate([ar, ar, ac, ac], axis=-1)
    cos = np.concatenate([np.cos(ang), np.ones((CTX, HEAD_DIM))], axis=0)
    sin = np.concatenate([np.sin(ang), np.zeros((CTX, HEAD_DIM))], axis=0)
    first = (np.arange(HEAD_DIM) % 64) < 32
    rope_a = np.where(first[None, :], -sin, 0.0)
    rope_b = np.where(first[None, :], 0.0, sin)

    def cs(n):
        k = np.arange(n)
        a = 2.0 * np.pi * np.outer(k, k) / n
        return np.cos(a), np.sin(a)

    c64, s64 = cs(N2)
    c128, s128 = cs(N1)
    c256, s256 = cs(CTX)
    m1 = np.concatenate([c64, -s64], axis=0)
    m2 = np.concatenate([c128, s128], axis=0)
    mc = np.concatenate([c256, s256], axis=0)
    csch = np.concatenate([c128, s128], axis=0)
    a = 2.0 * np.pi * np.outer(np.arange(N1), np.arange(N2)) / SEQ
    twr = np.broadcast_to(np.cos(a)[:, :, None], (N1, N2, 128))
    twi = np.broadcast_to(-np.sin(a)[:, :, None], (N1, N2, 128))
    tri = (np.arange(TM)[:, None] < np.arange(TM)[None, :]).astype(np.float32)
    perm = np.zeros((GRP * SUB, GRP * SUB), np.float32)
    rr, ss = np.meshgrid(np.arange(GRP), np.arange(SUB), indexing="ij")
    perm[(ss * GRP + rr).ravel(), (rr * SUB + ss).ravel()] = 1.0
    return dict(
        cos=jnp.asarray(cos, F32), rope_a=jnp.asarray(rope_a, F32), rope_b=jnp.asarray(rope_b, F32),
        m1=jnp.asarray(m1, BF16), m2=jnp.asarray(m2, BF16), mc=jnp.asarray(mc, BF16),
        csch=jnp.asarray(csch, BF16), twr=jnp.asarray(twr, F32), twi=jnp.asarray(twi, F32),
        tri=jnp.asarray(tri, BF16), perm=jnp.asarray(perm, BF16))


MOD_TN = 1024


def _mod_kernel(cc_ref, w_ref, b_ref, o_ref):
    a = _silu(cc_ref[...])
    o_ref[...] = jnp.dot(a.astype(BF16), w_ref[...].astype(BF16),
                         preferred_element_type=F32) + b_ref[...]


def _modulation(cc, w_mod, b_mod):
    nl = w_mod.shape[0]
    n6 = w_mod.shape[2]
    return pl.pallas_call(
        _mod_kernel,
        grid=(nl, n6 // MOD_TN),
        in_specs=[pl.BlockSpec((8, D_MODEL), lambda l, j: (0, 0)),
                  pl.BlockSpec((None, D_MODEL, MOD_TN), lambda l, j: (l, 0, j)),
                  pl.BlockSpec((None, 1, MOD_TN), lambda l, j: (l, 0, j))],
        out_specs=pl.BlockSpec((None, 8, MOD_TN), lambda l, j: (l, 0, j)),
        out_shape=jax.ShapeDtypeStruct((nl, 8, n6), F32),
        compiler_params=_cp(("arbitrary", "arbitrary")),
        name="modulation",
    )(cc, w_mod, b_mod.reshape(nl, 1, n6))


def _mod_spec(chunk):
    return pl.BlockSpec((None, None, 1, D_MODEL), lambda i: (i // NT_LAT, chunk, 0, 0))


def _row_spec(n):
    return pl.BlockSpec((1, n), lambda i: (0, 0))


PCW = 256


def _proj_kernel(x_ref, g_ref, sc_ref, sh_ref, w_ref, gq_ref, gk_ref, cos_ref, ra_ref, rb_ref,
                 u_ref, q_ref, k_ref, v_ref, hb_ref):
    x = x_ref[...]
    ms = jnp.mean(x * x, axis=-1, keepdims=True)
    h = x * lax.rsqrt(ms + EPS) * g_ref[...]
    h = h * (1.0 + sc_ref[...]) + sh_ref[...]
    hb_ref[...] = h.astype(BF16)
    cos = cos_ref[...]
    ra = ra_ref[...]
    rb = rb_ref[...]

    def head(t, g):
        m = jnp.mean(t * t, axis=-1, keepdims=True)
        t = t * lax.rsqrt(m + EPS) * g
        return t * cos + pltpu.roll(t, 96, 1) * ra + pltpu.roll(t, 32, 1) * rb

    scale = HEAD_DIM ** -0.5
    for c in range(D_IN // PCW):
        col = c * PCW
        p = jnp.dot(hb_ref[...], w_ref[:, col:col + PCW], preferred_element_type=F32)
        if col < D_F:
            u_ref[:, col:col + PCW] = p.astype(BF16)
        elif col < D_F + D_ATTN:
            o = col - D_F
            for j in range(PCW // HEAD_DIM):
                t = head(p[:, j * HEAD_DIM:(j + 1) * HEAD_DIM], gq_ref[...]) * scale
                q_ref[:, o + j * HEAD_DIM:o + (j + 1) * HEAD_DIM] = t.astype(BF16)
        elif col < D_F + D_ATTN + D_KV:
            o = col - D_F - D_ATTN
            for j in range(PCW // HEAD_DIM):
                t = head(p[:, j * HEAD_DIM:(j + 1) * HEAD_DIM], gk_ref[...])
                k_ref[:, o + j * HEAD_DIM:o + (j + 1) * HEAD_DIM] = t.astype(BF16)
        else:
            o = col - D_F - D_ATTN - D_KV
            v_ref[:, o:o + PCW] = p.astype(BF16)


def _project(xs, g1, modt, w_in_bf, gq, gk, tb):
    tile = lambda n: pl.BlockSpec((TM, n), lambda i: (i, 0))
    return pl.pallas_call(
        _proj_kernel,
        grid=(NT,),
        in_specs=[tile(D_MODEL), _row_spec(D_MODEL), _mod_spec(1), _mod_spec(0),
                  pl.BlockSpec((D_MODEL, D_IN), lambda i: (0, 0)),
                  _row_spec(HEAD_DIM), _row_spec(HEAD_DIM),
                  tile(HEAD_DIM), tile(HEAD_DIM), tile(HEAD_DIM)],
        out_specs=[tile(D_F), tile(D_ATTN), tile(D_KV), tile(D_KV)],
        out_shape=[jax.ShapeDtypeStruct((T, D_F), BF16), jax.ShapeDtypeStruct((T, D_ATTN), BF16),
                   jax.ShapeDtypeStruct((T, D_KV), BF16), jax.ShapeDtypeStruct((T, D_KV), BF16)],
        scratch_shapes=[pltpu.VMEM((TM, D_MODEL), BF16)],
        compiler_params=_cp(("arbitrary",)),
        name="norm_proj",
    )(xs, g1, modt, modt, w_in_bf, gq, gk, tb["cos"], tb["rope_a"], tb["rope_b"])


def _attn_kernel(sink_ref, q_ref, kp_ref, kc_ref, kn_ref, vp_ref, vc_ref, vn_ref, kx_ref, vx_ref,
                 ga_ref, o_ref, a_scr):
    b = pl.program_id(0)
    lat = b < NQB_LAT
    f_prev = jnp.logical_and(lat, b > 0).astype(I32)
    f_cur = lat.astype(I32)
    f_next = (b < NQB_LAT - 1).astype(I32)
    n3 = Q_PER_KV * QB
    r = lax.broadcasted_iota(I32, (n3, 3 * QB), 0) & (QB - 1)
    cidx = lax.broadcasted_iota(I32, (n3, 3 * QB), 1)
    blk = cidx >> 7
    j = cidx & (QB - 1)
    ok = jnp.where(blk == 0, (j >= r).astype(I32) * f_prev,
                   jnp.where(blk == 1, f_cur, (j <= r).astype(I32) * f_next))
    valid = ok > 0
    dn = (((1,), (1,)), ((), ()))
    for hk in range(N_KV):
        ks = slice(hk * HEAD_DIM, (hk + 1) * HEAD_DIM)
        qs = jnp.concatenate(
            [q_ref[:, (hk * Q_PER_KV + g) * HEAD_DIM:(hk * Q_PER_KV + g + 1) * HEAD_DIM]
             for g in range(Q_PER_KV)], axis=0)
        kw = jnp.concatenate([kp_ref[:, ks], kc_ref[:, ks], kn_ref[:, ks]], axis=0)
        vw = jnp.concatenate([vp_ref[:, ks], vc_ref[:, ks], vn_ref[:, ks]], axis=0)
        s = lax.dot_general(qs, kw, dn, preferred_element_type=F32)
        s = jnp.where(valid, s, NEG)
        sx = lax.dot_general(qs, kx_ref[:, ks], dn, preferred_element_type=F32)
        sk = jnp.concatenate(
            [jnp.full((QB, 1), sink_ref[hk * Q_PER_KV + g], F32) for g in range(Q_PER_KV)], axis=0)
        m = jnp.maximum(jnp.maximum(jnp.max(s, axis=-1, keepdims=True),
                                    jnp.max(sx, axis=-1, keepdims=True)), sk)
        e = jnp.exp(s - m)
        ex = jnp.exp(sx - m)
        den = (jnp.sum(e, axis=-1, keepdims=True) + jnp.sum(ex, axis=-1, keepdims=True)
               + jnp.exp(sk - m))
        o = (jnp.dot(e.astype(BF16), vw, preferred_element_type=F32)
             + jnp.dot(ex.astype(BF16), vx_ref[:, ks], preferred_element_type=F32)) / den
        for g in range(Q_PER_KV):
            h = hk * Q_PER_KV + g
            a_scr[:, h * HEAD_DIM:(h + 1) * HEAD_DIM] = o[g * QB:(g + 1) * QB]
    a = a_scr[...]
    ms = jnp.mean(a * a, axis=-1, keepdims=True)
    o_ref[...] = (a * lax.rsqrt(ms + EPS) * ga_ref[...]).astype(BF16)


def _attention(q, k, v, sink, g_a):
    kv = lambda f: pl.BlockSpec((QB, D_KV), f)
    prev = lambda b: (jnp.maximum(b - 1, 0), 0)
    cur = lambda b: (b, 0)
    nxt = lambda b: (jnp.minimum(b + 1, NQB - 1), 0)
    ctx = pl.BlockSpec((CTX, D_KV), lambda b: (SEQ // CTX, 0))
    return pl.pallas_call(
        _attn_kernel,
        grid=(NQB,),
        in_specs=[pl.BlockSpec(memory_space=pltpu.SMEM),
                  pl.BlockSpec((QB, D_ATTN), cur),
                  kv(prev), kv(cur), kv(nxt), kv(prev), kv(cur), kv(nxt), ctx, ctx,
                  _row_spec(D_ATTN)],
        out_specs=pl.BlockSpec((QB, D_ATTN), cur),
        out_shape=jax.ShapeDtypeStruct((T, D_ATTN), BF16),
        scratch_shapes=[pltpu.VMEM((QB, D_ATTN), F32)],
        compiler_params=_cp(("arbitrary",)),
        name="attention",
    )(sink, q, k, k, k, v, v, v, k, v, g_a)


F1_J = 8
F2_K = 4


def _f1_kernel(u_ref, m1_ref, twr_ref, twi_ref, z_ref):
    g = jnp.dot(m1_ref[...], u_ref[...], preferred_element_type=F32)
    for jl in range(F1_J):
        twr = twr_ref[jl]
        twi = twi_ref[jl]
        for lt in range(D_F // 128):
            c0 = jl * D_F + lt * 128
            gr = g[0:N2, c0:c0 + 128]
            gi = g[N2:2 * N2, c0:c0 + 128]
            z_ref[jl, 0:N2, lt * 128:(lt + 1) * 128] = (gr * twr - gi * twi).astype(BF16)
            z_ref[jl, N2:2 * N2, lt * 128:(lt + 1) * 128] = (gr * twi + gi * twr).astype(BF16)


def _f2_kernel(zr_ref, zi_ref, m2_ref, y_ref):
    pr = jnp.dot(m2_ref[...], zr_ref[...], preferred_element_type=F32)
    pi = jnp.dot(m2_ref[...], zi_ref[...], preferred_element_type=F32)
    yr = pr[0:N1] + pi[N1:2 * N1]
    yi = pi[0:N1] - pr[N1:2 * N1]
    for kl in range(F2_K):
        y_ref[:, kl * 2 * D_F:kl * 2 * D_F + D_F] = yr[:, kl * D_F:(kl + 1) * D_F].astype(BF16)
        y_ref[:, kl * 2 * D_F + D_F:(kl + 1) * 2 * D_F] = yi[:, kl * D_F:(kl + 1) * D_F].astype(BF16)


def _f3_kernel(y_ref, uc_ref, cs_ref, mc_ref, wf_ref, gf_ref, o_ref):
    i = pl.program_id(0)

    def finish(parts, scale):
        f = jnp.concatenate(parts, axis=1) * scale
        t = jnp.dot(f.astype(BF16), wf_ref[...], preferred_element_type=F32)
        ms = jnp.mean(t * t, axis=-1, keepdims=True)
        o_ref[...] = (t * lax.rsqrt(ms + EPS) * gf_ref[...]).astype(BF16)

    @pl.when(i < NT_LAT)
    def _():
        parts = []
        for g in range(N_FG):
            lhs = jnp.concatenate([y_ref[:, g * 128:(g + 1) * 128],
                                   y_ref[:, D_F + g * 128:D_F + (g + 1) * 128]], axis=1)
            parts.append(jnp.dot(lhs, cs_ref[...], preferred_element_type=F32))
        finish(parts, float((SEQ * 128) ** -0.5))

    @pl.when(i == NT_LAT)
    def _():
        pq = jnp.dot(mc_ref[...], uc_ref[...], preferred_element_type=F32)
        parts = []
        for g in range(N_FG):
            lhs = jnp.concatenate([pq[0:CTX, g * 128:(g + 1) * 128],
                                   -pq[CTX:2 * CTX, g * 128:(g + 1) * 128]], axis=1)
            parts.append(jnp.dot(lhs.astype(BF16), cs_ref[...], preferred_element_type=F32))
        finish(parts, float((CTX * 128) ** -0.5))


def _fourier(u, w_f_bf, g_f, tb):
    u2 = u.reshape(T // N1, N1 * D_F)
    z = pl.pallas_call(
        _f1_kernel,
        grid=(N1 // F1_J,),
        in_specs=[pl.BlockSpec((N2, F1_J * D_F), lambda c: (0, c)),
                  pl.BlockSpec((2 * N2, N2), lambda c: (0, 0)),
                  pl.BlockSpec((F1_J, N2, 128), lambda c: (c, 0, 0)),
                  pl.BlockSpec((F1_J, N2, 128), lambda c: (c, 0, 0))],
        out_specs=pl.BlockSpec((F1_J, 2 * N2, D_F), lambda c: (c, 0, 0)),
        out_shape=jax.ShapeDtypeStruct((N1, 2 * N2, D_F), BF16),
        compiler_params=_cp(("arbitrary",)),
        name="fourier_s1",
    )(u2, tb["m1"], tb["twr"], tb["twi"])
    z2 = z.reshape(N1, 2 * N2 * D_F)
    nk = N2 // F2_K
    y = pl.pallas_call(
        _f2_kernel,
        grid=(nk,),
        in_specs=[pl.BlockSpec((N1, F2_K * D_F), lambda k: (0, k)),
                  pl.BlockSpec((N1, F2_K * D_F), lambda k: (0, nk + k)),
                  pl.BlockSpec((2 * N1, N1), lambda k: (0, 0))],
        out_specs=pl.BlockSpec((N1, F2_K * 2 * D_F), lambda k: (0, k)),
        out_shape=jax.ShapeDtypeStruct((N1, N2 * 2 * D_F), BF16),
        compiler_params=_cp(("arbitrary",)),
        name="fourier_s2",
    )(z2, z2, tb["m2"])
    y2 = y.reshape(SEQ, 2 * D_F)
    return pl.pallas_call(
        _f3_kernel,
        grid=(NT,),
        in_specs=[pl.BlockSpec((TM, 2 * D_F), lambda i: (jnp.minimum(i, NT_LAT - 1), 0)),
                  pl.BlockSpec((CTX, D_F), lambda i: (SEQ // CTX, 0)),
                  pl.BlockSpec((2 * 128, 128), lambda i: (0, 0)),
                  pl.BlockSpec((2 * CTX, CTX), lambda i: (0, 0)),
                  pl.BlockSpec((D_F, D_F), lambda i: (0, 0)),
                  _row_spec(D_F)],
        out_specs=pl.BlockSpec((TM, D_F), lambda i: (i, 0)),
        out_shape=jax.ShapeDtypeStruct((T, D_F), BF16),
        compiler_params=_cp(("arbitrary",)),
        name="fourier_s3",
    )(y2, u, tb["csch"], tb["mc"], w_f_bf, g_f)


def _merge_kernel(x_ref, fn_ref, an_ref, wo_ref, ga1_ref, g2_ref, sc2_ref, sh2_ref, ga2_ref,
                  wsg_ref, wsu_ref, wsd_ref, wrh_ref, wrl_ref, br_ref, tri_ref,
                  xmid_ref, h2_ref, info_ref, wcol_ref, cnt_ref, carry_scr):
    i = pl.program_id(0)

    @pl.when(i == 0)
    def _():
        carry_scr[...] = jnp.zeros_like(carry_scr)

    m = (jnp.dot(fn_ref[...], wo_ref[0:D_F, :], preferred_element_type=F32)
         + jnp.dot(an_ref[...], wo_ref[D_F:D_MODEL, :], preferred_element_type=F32))
    x1 = x_ref[...] + ga1_ref[...] * m
    ms = jnp.mean(x1 * x1, axis=-1, keepdims=True)
    h = x1 * lax.rsqrt(ms + EPS) * g2_ref[...]
    h = h * (1.0 + sc2_ref[...]) + sh2_ref[...]
    hb = h.astype(BF16)
    hp = _pack_bf16_pairs(h)
    for s in range(SUB):
        h2_ref[:, s, :] = hp[:, s * PK:(s + 1) * PK]

    a = jnp.dot(hb, wsg_ref[...], preferred_element_type=F32)
    u = jnp.dot(hb, wsu_ref[...], preferred_element_type=F32)
    act = (_silu(a) * u).astype(BF16)
    ys = jnp.dot(act, wsd_ref[...], preferred_element_type=F32)
    xmid_ref[...] = x1 + ga2_ref[...] * ys

    hl = (h - hb.astype(F32)).astype(BF16)
    dn = (((1,), (1,)), ((), ()))
    lg = (lax.dot_general(wrh_ref[...], hb, dn, preferred_element_type=F32)
          + (lax.dot_general(wrl_ref[...], hb, dn, preferred_element_type=F32)
             + lax.dot_general(wrh_ref[...], hl, dn, preferred_element_type=F32)))
    score = jax.nn.sigmoid(lg)
    sel = score + br_ref[...]
    s = [sel[jj * N_GROUPS:(jj + 1) * N_GROUPS] for jj in range(EPG)]
    sr = [score[jj * N_GROUPS:(jj + 1) * N_GROUPS] for jj in range(EPG)]
    hi01, lo01 = jnp.maximum(s[0], s[1]), jnp.minimum(s[0], s[1])
    hi23, lo23 = jnp.maximum(s[2], s[3]), jnp.minimum(s[2], s[3])
    gscore = jnp.maximum(hi01, hi23) + jnp.maximum(jnp.minimum(hi01, hi23), jnp.maximum(lo01, lo23))
    gi = lax.broadcasted_iota(I32, (N_GROUPS, TM), 0)
    gmax = jnp.max(gscore, axis=0, keepdims=True)
    g_idx = jnp.min(jnp.where(gscore == gmax, gi, N_GROUPS), axis=0, keepdims=True)
    gsel = gi == g_idx
    v = [jnp.sum(jnp.where(gsel, s[jj], 0.0), axis=0, keepdims=True) for jj in range(EPG)]
    vr = [jnp.sum(jnp.where(gsel, sr[jj], 0.0), axis=0, keepdims=True) for jj in range(EPG)]

    def first_argmax(vals):
        best = vals[0]
        idx = jnp.zeros((1, TM), I32)
        for jj in range(1, EPG):
            upd = vals[jj] > best
            best = jnp.where(upd, vals[jj], best)
            idx = jnp.where(upd, jj, idx)
        return idx

    i1 = first_argmax(v)
    i2 = first_argmax([jnp.where(i1 == jj, -jnp.inf, v[jj]) for jj in range(EPG)])

    def pick(vals, idx):
        out = vals[0]
        for jj in range(1, EPG):
            out = jnp.where(idx == jj, vals[jj], out)
        return out

    w1 = pick(vr, i1)
    w2 = pick(vr, i2)
    wsum = w1 + w2
    w1 = w1 / wsum
    w2 = w2 / wsum

    ri = lax.broadcasted_iota(I32, (N_EXPERTS, TM), 0)
    oh1 = ri == i1 * N_GROUPS + g_idx
    oh2 = ri == i2 * N_GROUPS + g_idx
    oh = jnp.logical_or(oh1, oh2).astype(F32)
    pre = jnp.dot(oh.astype(BF16), tri_ref[...], preferred_element_type=F32) + carry_scr[:, 0:1]
    r1 = jnp.sum(jnp.where(oh1, pre, 0.0), axis=0, keepdims=True)
    r2 = jnp.sum(jnp.where(oh2, pre, 0.0), axis=0, keepdims=True)
    carry = carry_scr[...] + jnp.sum(oh, axis=1, keepdims=True)
    carry_scr[...] = carry
    cnt_ref[...] = carry

    zi = jnp.zeros((4, TM), I32)
    info_ref[...] = jnp.concatenate(
        [g_idx * EPG + i1, g_idx * EPG + i2, r1.astype(I32), r2.astype(I32), zi], axis=0)
    wrow = jnp.concatenate([w1, w2, jnp.zeros((126, TM), F32)], axis=0)
    wcol_ref[...] = wrow.T


def _merge(xs, fn, an, w_out_bf, modt, g2, wsg, wsu, wsd, wrh, wrl, br, tb):
    tile = lambda n: pl.BlockSpec((TM, n), lambda i: (i, 0))
    full = lambda a, b: pl.BlockSpec((a, b), lambda i: (0, 0))
    return pl.pallas_call(
        _merge_kernel,
        grid=(NT,),
        in_specs=[tile(D_MODEL), tile(D_F), tile(D_ATTN), full(D_MODEL, D_MODEL),
                  _mod_spec(2), _row_spec(D_MODEL), _mod_spec(4), _mod_spec(3), _mod_spec(5),
                  full(D_MODEL, D_SHARED), full(D_MODEL, D_SHARED), full(D_SHARED, D_MODEL),
                  full(N_EXPERTS, D_MODEL), full(N_EXPERTS, D_MODEL), full(N_EXPERTS, TM),
                  full(TM, TM)],
        out_specs=[tile(D_MODEL), pl.BlockSpec((TM, SUB, PK), lambda i: (i, 0, 0)),
                   pl.BlockSpec((8, TM), lambda i: (0, i)),
                   pl.BlockSpec((TM, 128), lambda i: (i, 0)),
                   pl.BlockSpec((N_EXPERTS, 128), lambda i: (0, 0))],
        out_shape=[jax.ShapeDtypeStruct((T, D_MODEL), F32), jax.ShapeDtypeStruct((T, SUB, PK), I32),
                   jax.ShapeDtypeStruct((8, T), I32), jax.ShapeDtypeStruct((T, 128), F32),
                   jax.ShapeDtypeStruct((N_EXPERTS, 128), F32)],
        scratch_shapes=[pltpu.VMEM((N_EXPERTS, 128), F32)],
        compiler_params=_cp(("arbitrary",)),
        name="merge_route",
    )(xs, fn, an, w_out_bf, modt, g2, modt, modt, modt, wsg, wsu, wsd, wrh, wrl, br, tb["tri"])


def _dispatch_kernel(dest_ref, pend_ref, padded_ref, h_ref, xb_hbm, zero_scr, sem, zsem):
    i = pl.program_id(0)

    def zero_copy(row0):
        return pltpu.make_async_copy(zero_scr, xb_hbm.at[pl.ds(row0, BM)], zsem)

    @pl.when(i == 0)
    def _():
        zero_scr[...] = jnp.zeros_like(zero_scr)
        n_used = pend_ref[N_EXPERTS - 1] // BM
        for e in range(N_EXPERTS):
            @pl.when(padded_ref[e] > 0)
            def _():
                zero_copy(pend_ref[e] - BM).start()
        for blk in range(NB - N_EXPERTS, NB):
            @pl.when(blk >= n_used)
            def _():
                zero_copy(blk * BM).start()
        for e in range(N_EXPERTS):
            @pl.when(padded_ref[e] > 0)
            def _():
                zero_copy(0).wait()
        for blk in range(NB - N_EXPERTS, NB):
            @pl.when(blk >= n_used)
            def _():
                zero_copy(0).wait()

    def copy(r, d):
        return pltpu.make_async_copy(h_ref.at[pl.ds(r, 1)], xb_hbm.at[pl.ds(d, 1)], sem)

    def issue(r, c):
        t = i * TM + r
        copy(r, dest_ref[2 * t]).start()
        copy(r, dest_ref[2 * t + 1]).start()
        return c

    lax.fori_loop(0, TM, issue, 0)

    def drain(r, c):
        copy(0, 0).wait()
        copy(0, 0).wait()
        return c

    lax.fori_loop(0, TM, drain, 0)


def _dispatch(dest, pad_end, padded, h2):
    return pl.pallas_call(
        _dispatch_kernel,
        grid_spec=pltpu.PrefetchScalarGridSpec(
            num_scalar_prefetch=3, grid=(NT,),
            in_specs=[pl.BlockSpec((TM, SUB, PK), lambda i, d, pe, pd: (i, 0, 0))],
            out_specs=pl.BlockSpec(memory_space=pl.ANY),
            scratch_shapes=[pltpu.VMEM((BM, SUB, PK), I32), pltpu.SemaphoreType.DMA(()),
                            pltpu.SemaphoreType.DMA(())]),
        out_shape=jax.ShapeDtypeStruct((ROWS, SUB, PK), I32),
        compiler_params=pltpu.CompilerParams(dimension_semantics=("arbitrary",),
                                             has_side_effects=True),
        name="dispatch",
    )(dest, pad_end, padded, h2)


def _pack_bf16_pairs(h):
    bits = lax.bitcast_convert_type(h.astype(BF16).astype(F32), I32)
    half = h.shape[1] // 2
    return (bits[:, :half] & -65536) | lax.shift_right_logical(bits[:, half:], 16)


GRP = 32


def _expert_kernel(eo_ref, es_ref, n_ref, x_ref, perm_ref, wg_hbm, wu_hbm, wd_hbm, y_ref,
                   wgf, wuf, wdf, wgb, wub, wdb, xs_scr, wsem, *, layer):
    b = pl.program_id(0)
    k = eo_ref[b]
    first = jnp.logical_or(b == 0, k != eo_ref[jnp.maximum(b - 1, 0)])

    def copies(kk, slot):
        e = es_ref[kk]
        return (pltpu.make_async_copy(wg_hbm.at[layer, e], wgf.at[slot], wsem.at[slot, 0]),
                pltpu.make_async_copy(wu_hbm.at[layer, e], wuf.at[slot], wsem.at[slot, 1]),
                pltpu.make_async_copy(wd_hbm.at[layer, e], wdf.at[slot], wsem.at[slot, 2]))

    @pl.when(b == 0)
    def _():
        for c in copies(0, 0):
            c.start()

    @pl.when(jnp.logical_and(first, b < n_ref[0]))
    def _():
        slot = k % 2
        for c in copies(k, slot):
            c.wait()

        @pl.when(k + 1 < n_ref[1])
        def _():
            for c in copies(k + 1, 1 - slot):
                c.start()

        wgb[...] = wgf[slot].astype(BF16)
        wub[...] = wuf[slot].astype(BF16)
        wdb[...] = wdf[slot].astype(BF16)

    @pl.when(b < n_ref[0])
    def _():
        half = D_MODEL // 2
        for grp in range(BM // GRP):
            w = x_ref[grp * GRP:(grp + 1) * GRP].reshape(GRP * SUB, PK)
            hi = lax.bitcast_convert_type(w & -65536, F32).astype(BF16)
            lo = lax.bitcast_convert_type(w << 16, F32).astype(BF16)
            g = jnp.concatenate([hi, lo], axis=1)
            r = jnp.dot(perm_ref[...], g, preferred_element_type=F32).astype(BF16)
            rows = slice(grp * GRP, (grp + 1) * GRP)
            for s in range(SUB):
                xs_scr[rows, s * PK:(s + 1) * PK] = r[s * GRP:(s + 1) * GRP, 0:PK]
                xs_scr[rows, half + s * PK:half + (s + 1) * PK] = r[s * GRP:(s + 1) * GRP, PK:2 * PK]
        x = xs_scr[...]
        a = jnp.dot(x, wgb[...], preferred_element_type=F32)
        u = jnp.dot(x, wub[...], preferred_element_type=F32)
        act = (_silu(a) * u).astype(BF16)
        y = jnp.dot(act, wdb[...], preferred_element_type=F32)
        for s in range(SUB):
            y_ref[:, s, :] = y[:, s * RL:(s + 1) * RL]

    @pl.when(b >= n_ref[0])
    def _():
        y_ref[...] = jnp.zeros_like(y_ref)


def _experts(l, eo, es, nn, xb, perm, w_g, w_u, w_d):
    hbm = pl.BlockSpec(memory_space=pl.ANY)
    return pl.pallas_call(
        functools.partial(_expert_kernel, layer=l),
        grid_spec=pltpu.PrefetchScalarGridSpec(
            num_scalar_prefetch=3, grid=(NB,),
            in_specs=[pl.BlockSpec((BM, SUB, PK), lambda b, eo, es, n: (jnp.minimum(b, n[0] - 1), 0, 0)),
                      pl.BlockSpec((GRP * SUB, GRP * SUB), lambda b, eo, es, n: (0, 0)),
                      hbm, hbm, hbm],
            out_specs=pl.BlockSpec((BM, SUB, RL), lambda b, eo, es, n: (b, 0, 0)),
            scratch_shapes=[pltpu.VMEM((2, D_MODEL, D_EXPERT), F32),
                            pltpu.VMEM((2, D_MODEL, D_EXPERT), F32),
                            pltpu.VMEM((2, D_EXPERT, D_MODEL), F32),
                            pltpu.VMEM((D_MODEL, D_EXPERT), BF16),
                            pltpu.VMEM((D_MODEL, D_EXPERT), BF16),
                            pltpu.VMEM((D_EXPERT, D_MODEL), BF16),
                            pltpu.VMEM((BM, D_MODEL), BF16),
                            pltpu.SemaphoreType.DMA((2, 3))]),
        out_shape=jax.ShapeDtypeStruct((ROWS, SUB, RL), F32),
        compiler_params=_cp(("arbitrary",)),
        name="experts",
    )(eo, es, nn, xb, perm, w_g, w_u, w_d)


def _combine_kernel(dest_ref, y_hbm, xmid_ref, wcol_ref, ga2_ref, o_ref, b1, b2, sem):
    i = pl.program_id(0)

    def copy(d, buf, r):
        return pltpu.make_async_copy(y_hbm.at[pl.ds(d, 1)], buf.at[pl.ds(r, 1)], sem)

    def issue(r, c):
        t = i * TM + r
        copy(dest_ref[2 * t], b1, r).start()
        copy(dest_ref[2 * t + 1], b2, r).start()
        return c

    lax.fori_loop(0, TM, issue, 0)

    def drain(r, c):
        copy(0, b1, 0).wait()
        copy(0, b2, 0).wait()
        return c

    lax.fori_loop(0, TM, drain, 0)
    w1 = wcol_ref[:, 0:1]
    w2 = wcol_ref[:, 1:2]
    for s in range(SUB):
        cs = slice(s * RL, (s + 1) * RL)
        o_ref[:, cs] = xmid_ref[:, cs] + ga2_ref[:, cs] * (w1 * b1[:, s, :] + w2 * b2[:, s, :])


def _combine(dest, yb, xmid, wcol, modt, n_tiles):
    tile = lambda n: pl.BlockSpec((TM, n), lambda i, d: (i, 0))
    return pl.pallas_call(
        _combine_kernel,
        grid_spec=pltpu.PrefetchScalarGridSpec(
            num_scalar_prefetch=1, grid=(n_tiles,),
            in_specs=[pl.BlockSpec(memory_space=pl.ANY), tile(D_MODEL), tile(128),
                      pl.BlockSpec((None, None, 1, D_MODEL), lambda i, d: (i // NT_LAT, 5, 0, 0))],
            out_specs=tile(D_MODEL),
            scratch_shapes=[pltpu.VMEM((TM, SUB, RL), F32), pltpu.VMEM((TM, SUB, RL), F32),
                            pltpu.SemaphoreType.DMA(())]),
        out_shape=jax.ShapeDtypeStruct((n_tiles * TM, D_MODEL), F32),
        compiler_params=_cp(("arbitrary",)),
        name="combine",
    )(dest, yb, xmid, wcol, modt)


def _perm_experts(a):
    return a.reshape(a.shape[:-1] + (N_GROUPS, EPG)).swapaxes(-1, -2).reshape(a.shape)


def kernel(x, c, ctx, c_ctx, w_mod, b_mod, g_norm1, g_norm2, w_in, w_fourier, g_q, g_k, sink,
           g_branch_f, g_branch_a, w_out, w_router, b_router, w_exp_gate, w_exp_up, w_exp_down,
           w_sh_gate, w_sh_up, w_sh_down):
    assert x.shape == (1, SEQ, D_MODEL) and ctx.shape == (1, CTX, D_MODEL)
    tb = _tables()
    xs = jnp.concatenate([x[0], ctx[0]], axis=0)

    cc = jnp.zeros((8, D_MODEL), F32).at[0].set(c[0]).at[1].set(c_ctx)
    mods = _modulation(cc, w_mod, b_mod)

    wr_t = _perm_experts(w_router).T
    wrh = wr_t.astype(BF16)
    wrl = (wr_t - wrh.astype(F32)).astype(BF16)
    br = jnp.broadcast_to(_perm_experts(b_router)[:, None], (N_EXPERTS, TM)).astype(F32)

    for l in range(DEPTH):
        last = l == DEPTH - 1
        modt = mods[l, 0:2].reshape(2, 6, 1, D_MODEL)
        row = lambda a: a[l].reshape(1, -1)
        u, q, k, v = _project(xs, row(g_norm1), modt, w_in[l].astype(BF16), row(g_q), row(g_k), tb)
        an = _attention(q, k, v, sink[l], row(g_branch_a))
        fn = _fourier(u, w_fourier[l].astype(BF16), row(g_branch_f), tb)
        xmid, h2, info, wcol, cnt = _merge(
            xs, fn, an, w_out[l].astype(BF16), modt, row(g_norm2),
            w_sh_gate[l].astype(BF16), w_sh_up[l].astype(BF16), w_sh_down[l].astype(BF16),
            wrh, wrl, br, tb)

        xs = _routed(l, info, cnt, h2, xmid, wcol, modt, w_exp_gate, w_exp_up, w_exp_down, tb,
                     NT_LAT if last else NT)
    return xs[None]


def _routed(l, info, cnt, h2, xmid, wcol, modt, w_g, w_u, w_d, tb, n_tiles):
    counts = cnt[:, 0].astype(I32).reshape(EPG, N_GROUPS).T.reshape(N_EXPERTS)
    padded = ((counts + BM - 1) // BM) * BM
    pad_end = jnp.cumsum(padded).astype(I32)
    pad_start = pad_end - padded
    start = jnp.sum(jnp.where(info[0:2, :, None] == jnp.arange(N_EXPERTS, dtype=I32),
                              pad_start, 0), axis=-1)
    dest = (start + info[2:4]).T.reshape(2 * T)
    block_e = jnp.minimum(
        jnp.searchsorted(pad_end, jnp.arange(NB, dtype=I32) * BM, side="right"),
        N_EXPERTS - 1).astype(I32)
    ids = jnp.arange(N_EXPERTS, dtype=I32)
    used = padded > 0
    ordinal = jnp.cumsum(used.astype(I32)) - 1
    eo = jnp.sum(jnp.where(block_e[:, None] == ids, ordinal, 0), axis=-1)
    es = jnp.sum(jnp.where(jnp.logical_and(used, ordinal == ids[:, None]), ids, 0), axis=-1)
    nn = jnp.stack([pad_end[N_EXPERTS - 1] // BM, ordinal[N_EXPERTS - 1] + 1]).astype(I32)
    xb = _dispatch(dest, pad_end, padded, h2)
    yb = _experts(l, eo.astype(I32), es.astype(I32), nn, xb, tb["perm"], w_g, w_u, w_d)
    return _combine(dest, yb, xmid, wcol, modt, n_tiles)
```

```python
import functools

import numpy as np
import jax
import jax.numpy as jnp
from jax import lax
from jax.experimental import pallas as pl
from jax.experimental.pallas import tpu as pltpu

F32 = jnp.float32
BF16 = jnp.bfloat16
I32 = jnp.int32

D_MODEL = 2048
SEQ = 8192
CTX = 256
T = SEQ + CTX
DEPTH = 4
GRID_W = 64
HEAD_DIM = 128
D_F = 512
N_FG = 4
N_HEADS = 12
N_KV = 4
Q_PER_KV = 3
D_ATTN = N_HEADS * HEAD_DIM
D_KV = N_KV * HEAD_DIM
D_IN = D_F + D_ATTN + 2 * D_KV
ROPE_BASE = 10000.0
N_EXPERTS = 32
N_GROUPS = 8
EPG = 4
D_EXPERT = 512
D_SHARED = 512
EPS = 1e-6
NEG = -1e30

TM = 256
NT = T // TM
NT_LAT = SEQ // TM
QB = 128
NQB = T // QB
NQB_LAT = SEQ // QB
BM = 256
NB = (2 * T + BM - 1) // BM + N_EXPERTS
ROWS = NB * BM
N1 = 128
N2 = 64
SUB = 8
RL = D_MODEL // SUB
PK = D_MODEL // 2 // SUB
VMEM_LIMIT = 56 * 1024 * 1024


def _cp(sem):
    return pltpu.CompilerParams(dimension_semantics=sem, vmem_limit_bytes=VMEM_LIMIT)


def _silu(a):
    return a * jax.nn.sigmoid(a)


@functools.lru_cache(maxsize=None)
def _tables():
    t = np.arange(SEQ)
    row = (t // GRID_W).astype(np.float64)
    col = (t % GRID_W).astype(np.float64)
    inv = ROPE_BASE ** (-np.arange(0, HEAD_DIM // 2, 2, dtype=np.float64) / (HEAD_DIM // 2))
    ar = row[:, None] * inv
    ac = col[:, None] * inv
    ang = np.concatenate([ar, ar, ac, ac], axis=-1)
    cos = np.concatenate([np.cos(ang), np.ones((CTX, HEAD_DIM))], axis=0)
    sin = np.concatenate([np.sin(ang), np.zeros((CTX, HEAD_DIM))], axis=0)
    first = (np.arange(HEAD_DIM) % 64) < 32
    rope_a = np.where(first[None, :], -sin, 0.0)
    rope_b = np.where(first[None, :], 0.0, sin)

    def cs(n):
        k = np.arange(n)
        a = 2.0 * np.pi * np.outer(k, k) / n
        return np.cos(a), np.sin(a)

    c64, s64 = cs(N2)
    c128, s128 = cs(N1)
    c256, s256 = cs(CTX)
    m1 = np.concatenate([c64, -s64], axis=0)
    m2 = np.concatenate([c128, s128], axis=0)
    mc = np.concatenate([c256, s256], axis=0)
    csch = np.concatenate([c128, s128], axis=0)
    a = 2.0 * np.pi * np.outer(np.arange(N1), np.arange(N2)) / SEQ
    twr = np.broadcast_to(np.cos(a)[:, :, None], (N1, N2, 128))
    twi = np.broadcast_to(-np.sin(a)[:, :, None], (N1, N2, 128))
    tri = (np.arange(TM)[:, None] < np.arange(TM)[None, :]).astype(np.float32)
    perm = np.zeros((GRP * SUB, GRP * SUB), np.float32)
    rr, ss = np.meshgrid(np.arange(GRP), np.arange(SUB), indexing="ij")
    perm[(ss * GRP + rr).ravel(), (rr * SUB + ss).ravel()] = 1.0
    return dict(
        cos=jnp.asarray(cos, F32), rope_a=jnp.asarray(rope_a, F32), rope_b=jnp.asarray(rope_b, F32),
        m1=jnp.asarray(m1, BF16), m2=jnp.asarray(m2, BF16), mc=jnp.asarray(mc, BF16),
        csch=jnp.asarray(csch, BF16), twr=jnp.asarray(twr, F32), twi=jnp.asarray(twi, F32),
        tri=jnp.asarray(tri, BF16), perm=jnp.asarray(perm, BF16))


MOD_TN = 1024


def _mod_kernel(cc_ref, w_ref, b_ref, o_ref):
    a = _silu(cc_ref[...])
    o_ref[...] = jnp.dot(a.astype(BF16), w_ref[...].astype(BF16),
                         preferred_element_type=F32) + b_ref[...]


def _modulation(cc, w_mod, b_mod):
    nl = w_mod.shape[0]
    n6 = w_mod.shape[2]
    return pl.pallas_call(
        _mod_kernel,
        grid=(nl, n6 // MOD_TN),
        in_specs=[pl.BlockSpec((8, D_MODEL), lambda l, j: (0, 0)),
                  pl.BlockSpec((None, D_MODEL, MOD_TN), lambda l, j: (l, 0, j)),
                  pl.BlockSpec((None, 1, MOD_TN), lambda l, j: (l, 0, j))],
        out_specs=pl.BlockSpec((None, 8, MOD_TN), lambda l, j: (l, 0, j)),
        out_shape=jax.ShapeDtypeStruct((nl, 8, n6), F32),
        compiler_params=_cp(("arbitrary", "arbitrary")),
        name="modulation",
    )(cc, w_mod, b_mod.reshape(nl, 1, n6))


def _mod_spec(chunk):
    return pl.BlockSpec((None, None, 1, D_MODEL), lambda i: (i // NT_LAT, chunk, 0, 0))


def _row_spec(n):
    return pl.BlockSpec((1, n), lambda i: (0, 0))


PCW = 256


def _proj_kernel(x_ref, g_ref, sc_ref, sh_ref, w_ref, gq_ref, gk_ref, cos_ref, ra_ref, rb_ref,
                 u_ref, q_ref, k_ref, v_ref, hb_ref):
    x = x_ref[...]
    ms = jnp.mean(x * x, axis=-1, keepdims=True)
    h = x * lax.rsqrt(ms + EPS) * g_ref[...]
    h = h * (1.0 + sc_ref[...]) + sh_ref[...]
    hb_ref[...] = h.astype(BF16)
    cos = cos_ref[...]
    ra = ra_ref[...]
    rb = rb_ref[...]

    def head(t, g):
        m = jnp.mean(t * t, axis=-1, keepdims=True)
        t = t * lax.rsqrt(m + EPS) * g
        return t * cos + pltpu.roll(t, 96, 1) * ra + pltpu.roll(t, 32, 1) * rb

    scale = HEAD_DIM ** -0.5
    for c in range(D_IN // PCW):
        col = c * PCW
        p = jnp.dot(hb_ref[...], w_ref[:, col:col + PCW], preferred_element_type=F32)
        if col < D_F:
            u_ref[:, col:col + PCW] = p.astype(BF16)
        elif col < D_F + D_ATTN:
            o = col - D_F
            for j in range(PCW // HEAD_DIM):
                t = head(p[:, j * HEAD_DIM:(j + 1) * HEAD_DIM], gq_ref[...]) * scale
                q_ref[:, o + j * HEAD_DIM:o + (j + 1) * HEAD_DIM] = t.astype(BF16)
        elif col < D_F + D_ATTN + D_KV:
            o = col - D_F - D_ATTN
            for j in range(PCW // HEAD_DIM):
                t = head(p[:, j * HEAD_DIM:(j + 1) * HEAD_DIM], gk_ref[...])
                k_ref[:, o + j * HEAD_DIM:o + (j + 1) * HEAD_DIM] = t.astype(BF16)
        else:
            o = col - D_F - D_ATTN - D_KV
            v_ref[:, o:o + PCW] = p.astype(BF16)


def _project(xs, g1, modt, w_in_bf, gq, gk, tb):
    tile = lambda n: pl.BlockSpec((TM, n), lambda i: (i, 0))
    return pl.pallas_call(
        _proj_kernel,
        grid=(NT,),
        in_specs=[tile(D_MODEL), _row_spec(D_MODEL), _mod_spec(1), _mod_spec(0),
                  pl.BlockSpec((D_MODEL, D_IN), lambda i: (0, 0)),
                  _row_spec(HEAD_DIM), _row_spec(HEAD_DIM),
                  tile(HEAD_DIM), tile(HEAD_DIM), tile(HEAD_DIM)],
        out_specs=[tile(D_F), tile(D_ATTN), tile(D_KV), tile(D_KV)],
        out_shape=[jax.ShapeDtypeStruct((T, D_F), BF16), jax.ShapeDtypeStruct((T, D_ATTN), BF16),
                   jax.ShapeDtypeStruct((T, D_KV), BF16), jax.ShapeDtypeStruct((T, D_KV), BF16)],
        scratch_shapes=[pltpu.VMEM((TM, D_MODEL), BF16)],
        compiler_params=_cp(("arbitrary",)),
        name="norm_proj",
    )(xs, g1, modt, modt, w_in_bf, gq, gk, tb["cos"], tb["rope_a"], tb["rope_b"])


def _attn_kernel(sink_ref, q_ref, kp_ref, kc_ref, kn_ref, vp_ref, vc_ref, vn_ref, kx_ref, vx_ref,
                 ga_ref, o_ref, a_scr):
    b = pl.program_id(0)
    lat = b < NQB_LAT
    f_prev = jnp.logical_and(lat, b > 0).astype(I32)
    f_cur = lat.astype(I32)
    f_next = (b < NQB_LAT - 1).astype(I32)
    n3 = Q_PER_KV * QB
    r = lax.broadcasted_iota(I32, (n3, 3 * QB), 0) & (QB - 1)
    cidx = lax.broadcasted_iota(I32, (n3, 3 * QB), 1)
    blk = cidx >> 7
    j = cidx & (QB - 1)
    ok = jnp.where(blk == 0, (j >= r).astype(I32) * f_prev,
                   jnp.where(blk == 1, f_cur, (j <= r).astype(I32) * f_next))
    valid = ok > 0
    dn = (((1,), (1,)), ((), ()))
    for hk in range(N_KV):
        ks = slice(hk * HEAD_DIM, (hk + 1) * HEAD_DIM)
        qs = jnp.concatenate(
            [q_ref[:, (hk * Q_PER_KV + g) * HEAD_DIM:(hk * Q_PER_KV + g + 1) * HEAD_DIM]
             for g in range(Q_PER_KV)], axis=0)
        kw = jnp.concatenate([kp_ref[:, ks], kc_ref[:, ks], kn_ref[:, ks]], axis=0)
        vw = jnp.concatenate([vp_ref[:, ks], vc_ref[:, ks], vn_ref[:, ks]], axis=0)
        s = lax.dot_general(qs, kw, dn, preferred_element_type=F32)
        s = jnp.where(valid, s, NEG)
        sx = lax.dot_general(qs, kx_ref[:, ks], dn, preferred_element_type=F32)
        sk = jnp.concatenate(
            [jnp.full((QB, 1), sink_ref[hk * Q_PER_KV + g], F32) for g in range(Q_PER_KV)], axis=0)
        m = jnp.maximum(jnp.maximum(jnp.max(s, axis=-1, keepdims=True),
                                    jnp.max(sx, axis=-1, keepdims=True)), sk)
        e = jnp.exp(s - m)
        ex = jnp.exp(sx - m)
        den = (jnp.sum(e, axis=-1, keepdims=True) + jnp.sum(ex, axis=-1, keepdims=True)
               + jnp.exp(sk - m))
        o = (jnp.dot(e.astype(BF16), vw, preferred_element_type=F32)
             + jnp.dot(ex.astype(BF16), vx_ref[:, ks], preferred_element_type=F32)) / den
        for g in range(Q_PER_KV):
            h = hk * Q_PER_KV + g
            a_scr[:, h * HEAD_DIM:(h + 1) * HEAD_DIM] = o[g * QB:(g + 1) * QB]
    a = a_scr[...]
    ms = jnp.mean(a * a, axis=-1, keepdims=True)
    o_ref[...] = (a * lax.rsqrt(ms + EPS) * ga_ref[...]).astype(BF16)


def _attention(q, k, v, sink, g_a):
    kv = lambda f: pl.BlockSpec((QB, D_KV), f)
    prev = lambda b: (jnp.maximum(b - 1, 0), 0)
    cur = lambda b: (b, 0)
    nxt = lambda b: (jnp.minimum(b + 1, NQB - 1), 0)
    ctx = pl.BlockSpec((CTX, D_KV), lambda b: (SEQ // CTX, 0))
    return pl.pallas_call(
        _attn_kernel,
        grid=(NQB,),
        in_specs=[pl.BlockSpec(memory_space=pltpu.SMEM),
                  pl.BlockSpec((QB, D_ATTN), cur),
                  kv(prev), kv(cur), kv(nxt), kv(prev), kv(cur), kv(nxt), ctx, ctx,
                  _row_spec(D_ATTN)],
        out_specs=pl.BlockSpec((QB, D_ATTN), cur),
        out_shape=jax.ShapeDtypeStruct((T, D_ATTN), BF16),
        scratch_shapes=[pltpu.VMEM((QB, D_ATTN), F32)],
        compiler_params=_cp(("arbitrary",)),
        name="attention",
    )(sink, q, k, k, k, v, v, v, k, v, g_a)


F1_J = 8
F2_K = 4


def _f1_kernel(u_ref, m1_ref, twr_ref, twi_ref, z_ref):
    g = jnp.dot(m1_ref[...], u_ref[...], preferred_element_type=F32)
    for jl in range(F1_J):
        twr = twr_ref[jl]
        twi = twi_ref[jl]
        for lt in range(D_F // 128):
            c0 = jl * D_F + lt * 128
            gr = g[0:N2, c0:c0 + 128]
            gi = g[N2:2 * N2, c0:c0 + 128]
            z_ref[jl, 0:N2, lt * 128:(lt + 1) * 128] = (gr * twr - gi * twi).astype(BF16)
            z_ref[jl, N2:2 * N2, lt * 128:(lt + 1) * 128] = (gr * twi + gi * twr).astype(BF16)


def _f2_kernel(zr_ref, zi_ref, m2_ref, y_ref):
    pr = jnp.dot(m2_ref[...], zr_ref[...], preferred_element_type=F32)
    pi = jnp.dot(m2_ref[...], zi_ref[...], preferred_element_type=F32)
    yr = pr[0:N1] + pi[N1:2 * N1]
    yi = pi[0:N1] - pr[N1:2 * N1]
    for kl in range(F2_K):
        y_ref[:, kl * 2 * D_F:kl * 2 * D_F + D_F] = yr[:, kl * D_F:(kl + 1) * D_F].astype(BF16)
        y_ref[:, kl * 2 * D_F + D_F:(kl + 1) * 2 * D_F] = yi[:, kl * D_F:(kl + 1) * D_F].astype(BF16)


def _f3_kernel(y_ref, uc_ref, cs_ref, mc_ref, wf_ref, gf_ref, o_ref):
    i = pl.program_id(0)

    def finish(parts, scale):
        f = jnp.concatenate(parts, axis=1) * scale
        t = jnp.dot(f.astype(BF16), wf_ref[...], preferred_element_type=F32)
        ms = jnp.mean(t * t, axis=-1, keepdims=True)
        o_ref[...] = (t * lax.rsqrt(ms + EPS) * gf_ref[...]).astype(BF16)

    @pl.when(i < NT_LAT)
    def _():
        parts = []
        for g in range(N_FG):
            lhs = jnp.concatenate([y_ref[:, g * 128:(g + 1) * 128],
                                   y_ref[:, D_F + g * 128:D_F + (g + 1) * 128]], axis=1)
            parts.append(jnp.dot(lhs, cs_ref[...], preferred_element_type=F32))
        finish(parts, float((SEQ * 128) ** -0.5))

    @pl.when(i == NT_LAT)
    def _():
        pq = jnp.dot(mc_ref[...], uc_ref[...], preferred_element_type=F32)
        parts = []
        for g in range(N_FG):
            lhs = jnp.concatenate([pq[0:CTX, g * 128:(g + 1) * 128],
                                   -pq[CTX:2 * CTX, g * 128:(g + 1) * 128]], axis=1)
            parts.append(jnp.dot(lhs.astype(BF16), cs_ref[...], preferred_element_type=F32))
        finish(parts, float((CTX * 128) ** -0.5))


def _fourier(u, w_f_bf, g_f, tb):
    u2 = u.reshape(T // N1, N1 * D_F)
    z = pl.pallas_call(
        _f1_kernel,
        grid=(N1 // F1_J,),
        in_specs=[pl.BlockSpec((N2, F1_J * D_F), lambda c: (0, c)),
                  pl.BlockSpec((2 * N2, N2), lambda c: (0, 0)),
                  pl.BlockSpec((F1_J, N2, 128), lambda c: (c, 0, 0)),
                  pl.BlockSpec((F1_J, N2, 128), lambda c: (c, 0, 0))],
        out_specs=pl.BlockSpec((F1_J, 2 * N2, D_F), lambda c: (c, 0, 0)),
        out_shape=jax.ShapeDtypeStruct((N1, 2 * N2, D_F), BF16),
        compiler_params=_cp(("arbitrary",)),
        name="fourier_s1",
    )(u2, tb["m1"], tb["twr"], tb["twi"])
    z2 = z.reshape(N1, 2 * N2 * D_F)
    nk = N2 // F2_K
    y = pl.pallas_call(
        _f2_kernel,
        grid=(nk,),
        in_specs=[pl.BlockSpec((N1, F2_K * D_F), lambda k: (0, k)),
                  pl.BlockSpec((N1, F2_K * D_F), lambda k: (0, nk + k)),
                  pl.BlockSpec((2 * N1, N1), lambda k: (0, 0))],
        out_specs=pl.BlockSpec((N1, F2_K * 2 * D_F), lambda k: (0, k)),
        out_shape=jax.ShapeDtypeStruct((N1, N2 * 2 * D_F), BF16),
        compiler_params=_cp(("arbitrary",)),
        name="fourier_s2",
    )(z2, z2, tb["m2"])
    y2 = y.reshape(SEQ, 2 * D_F)
    return pl.pallas_call(
        _f3_kernel,
        grid=(NT,),
        in_specs=[pl.BlockSpec((TM, 2 * D_F), lambda i: (jnp.minimum(i, NT_LAT - 1), 0)),
                  pl.BlockSpec((CTX, D_F), lambda i: (SEQ // CTX, 0)),
                  pl.BlockSpec((2 * 128, 128), lambda i: (0, 0)),
                  pl.BlockSpec((2 * CTX, CTX), lambda i: (0, 0)),
                  pl.BlockSpec((D_F, D_F), lambda i: (0, 0)),
                  _row_spec(D_F)],
        out_specs=pl.BlockSpec((TM, D_F), lambda i: (i, 0)),
        out_shape=jax.ShapeDtypeStruct((T, D_F), BF16),
        compiler_params=_cp(("arbitrary",)),
        name="fourier_s3",
    )(y2, u, tb["csch"], tb["mc"], w_f_bf, g_f)


def _merge_kernel(x_ref, fn_ref, an_ref, wo_ref, ga1_ref, g2_ref, sc2_ref, sh2_ref, ga2_ref,
                  wsg_ref, wsu_ref, wsd_ref, wrh_ref, wrl_ref, br_ref, tri_ref,
                  xmid_ref, h2_ref, info_ref, wcol_ref, cnt_ref, carry_scr):
    i = pl.program_id(0)

    @pl.when(i == 0)
    def _():
        carry_scr[...] = jnp.zeros_like(carry_scr)

    m = (jnp.dot(fn_ref[...], wo_ref[0:D_F, :], preferred_element_type=F32)
         + jnp.dot(an_ref[...], wo_ref[D_F:D_MODEL, :], preferred_element_type=F32))
    x1 = x_ref[...] + ga1_ref[...] * m
    ms = jnp.mean(x1 * x1, axis=-1, keepdims=True)
    h = x1 * lax.rsqrt(ms + EPS) * g2_ref[...]
    h = h * (1.0 + sc2_ref[...]) + sh2_ref[...]
    hb = h.astype(BF16)
    hp = _pack_bf16_pairs(h)
    for s in range(SUB):
        h2_ref[:, s, :] = hp[:, s * PK:(s + 1) * PK]

    a = jnp.dot(hb, wsg_ref[...], preferred_element_type=F32)
    u = jnp.dot(hb, wsu_ref[...], preferred_element_type=F32)
    act = (_silu(a) * u).astype(BF16)
    ys = jnp.dot(act, wsd_ref[...], preferred_element_type=F32)
    xmid_ref[...] = x1 + ga2_ref[...] * ys

    hl = (h - hb.astype(F32)).astype(BF16)
    dn = (((1,), (1,)), ((), ()))
    lg = (lax.dot_general(wrh_ref[...], hb, dn, preferred_element_type=F32)
          + (lax.dot_general(wrl_ref[...], hb, dn, preferred_element_type=F32)
             + lax.dot_general(wrh_ref[...], hl, dn, preferred_element_type=F32)))
    score = jax.nn.sigmoid(lg)
    sel = score + br_ref[...]
    s = [sel[jj * N_GROUPS:(jj + 1) * N_GROUPS] for jj in range(EPG)]
    sr = [score[jj * N_GROUPS:(jj + 1) * N_GROUPS] for jj in range(EPG)]
    hi01, lo01 = jnp.maximum(s[0], s[1]), jnp.minimum(s[0], s[1])
    hi23, lo23 = jnp.maximum(s[2], s[3]), jnp.minimum(s[2], s[3])
    gscore = jnp.maximum(hi01, hi23) + jnp.maximum(jnp.minimum(hi01, hi23), jnp.maximum(lo01, lo23))
    gi = lax.broadcasted_iota(I32, (N_GROUPS, TM), 0)
    gmax = jnp.max(gscore, axis=0, keepdims=True)
    g_idx = jnp.min(jnp.where(gscore == gmax, gi, N_GROUPS), axis=0, keepdims=True)
    gsel = gi == g_idx
    v = [jnp.sum(jnp.where(gsel, s[jj], 0.0), axis=0, keepdims=True) for jj in range(EPG)]
    vr = [jnp.sum(jnp.where(gsel, sr[jj], 0.0), axis=0, keepdims=True) for jj in range(EPG)]

    def first_argmax(vals):
        best = vals[0]
        idx = jnp.zeros((1, TM), I32)
        for jj in range(1, EPG):
            upd = vals[jj] > best
            best = jnp.where(upd, vals[jj], best)
            idx = jnp.where(upd, jj, idx)
        return idx

    i1 = first_argmax(v)
    i2 = first_argmax([jnp.where(i1 == jj, -jnp.inf, v[jj]) for jj in range(EPG)])

    def pick(vals, idx):
        out = vals[0]
        for jj in range(1, EPG):
            out = jnp.where(idx == jj, vals[jj], out)
        return out

    w1 = pick(vr, i1)
    w2 = pick(vr, i2)
    wsum = w1 + w2
    w1 = w1 / wsum
    w2 = w2 / wsum

    ri = lax.broadcasted_iota(I32, (N_EXPERTS, TM), 0)
    oh1 = ri == i1 * N_GROUPS + g_idx
    oh2 = ri == i2 * N_GROUPS + g_idx
    oh = jnp.logical_or(oh1, oh2).astype(F32)
    pre = jnp.dot(oh.astype(BF16), tri_ref[...], preferred_element_type=F32) + carry_scr[:, 0:1]
    r1 = jnp.sum(jnp.where(oh1, pre, 0.0), axis=0, keepdims=True)
    r2 = jnp.sum(jnp.where(oh2, pre, 0.0), axis=0, keepdims=True)
    carry = carry_scr[...] + jnp.sum(oh, axis=1, keepdims=True)
    carry_scr[...] = carry
    cnt_ref[...] = carry

    zi = jnp.zeros((4, TM), I32)
    info_ref[...] = jnp.concatenate(
        [g_idx * EPG + i1, g_idx * EPG + i2, r1.astype(I32), r2.astype(I32), zi], axis=0)
    wrow = jnp.concatenate([w1, w2, jnp.zeros((126, TM), F32)], axis=0)
    wcol_ref[...] = wrow.T


def _merge(xs, fn, an, w_out_bf, modt, g2, wsg, wsu, wsd, wrh, wrl, br, tb):
    tile = lambda n: pl.BlockSpec((TM, n), lambda i: (i, 0))
    full = lambda a, b: pl.BlockSpec((a, b), lambda i: (0, 0))
    return pl.pallas_call(
        _merge_kernel,
        grid=(NT,),
        in_specs=[tile(D_MODEL), tile(D_F), tile(D_ATTN), full(D_MODEL, D_MODEL),
                  _mod_spec(2), _row_spec(D_MODEL), _mod_spec(4), _mod_spec(3), _mod_spec(5),
                  full(D_MODEL, D_SHARED), full(D_MODEL, D_SHARED), full(D_SHARED, D_MODEL),
                  full(N_EXPERTS, D_MODEL), full(N_EXPERTS, D_MODEL), full(N_EXPERTS, TM),
                  full(TM, TM)],
        out_specs=[tile(D_MODEL), pl.BlockSpec((TM, SUB, PK), lambda i: (i, 0, 0)),
                   pl.BlockSpec((8, TM), lambda i: (0, i)),
                   pl.BlockSpec((TM, 128), lambda i: (i, 0)),
                   pl.BlockSpec((N_EXPERTS, 128), lambda i: (0, 0))],
        out_shape=[jax.ShapeDtypeStruct((T, D_MODEL), F32), jax.ShapeDtypeStruct((T, SUB, PK), I32),
                   jax.ShapeDtypeStruct((8, T), I32), jax.ShapeDtypeStruct((T, 128), F32),
                   jax.ShapeDtypeStruct((N_EXPERTS, 128), F32)],
        scratch_shapes=[pltpu.VMEM((N_EXPERTS, 128), F32)],
        compiler_params=_cp(("arbitrary",)),
        name="merge_route",
    )(xs, fn, an, w_out_bf, modt, g2, modt, modt, modt, wsg, wsu, wsd, wrh, wrl, br, tb["tri"])


def _dispatch_kernel(dest_ref, pend_ref, padded_ref, h_ref, xb_hbm, zero_scr, sem, zsem):
    i = pl.program_id(0)

    def zero_copy(row0):
        return pltpu.make_async_copy(zero_scr, xb_hbm.at[pl.ds(row0, BM)], zsem)

    @pl.when(i == 0)
    def _():
        zero_scr[...] = jnp.zeros_like(zero_scr)
        n_used = pend_ref[N_EXPERTS - 1] // BM
        for e in range(N_EXPERTS):
            @pl.when(padded_ref[e] > 0)
            def _():
                zero_copy(pend_ref[e] - BM).start()
        for blk in range(NB - N_EXPERTS, NB):
            @pl.when(blk >= n_used)
            def _():
                zero_copy(blk * BM).start()
        for e in range(N_EXPERTS):
            @pl.when(padded_ref[e] > 0)
            def _():
                zero_copy(0).wait()
        for blk in range(NB - N_EXPERTS, NB):
            @pl.when(blk >= n_used)
            def _():
                zero_copy(0).wait()

    def copy(r, d):
        return pltpu.make_async_copy(h_ref.at[pl.ds(r, 1)], xb_hbm.at[pl.ds(d, 1)], sem)

    def issue(r, c):
        t = i * TM + r
        copy(r, dest_ref[2 * t]).start()
        copy(r, dest_ref[2 * t + 1]).start(priority=1)
        return c

    lax.fori_loop(0, TM, issue, 0)

    def drain(r, c):
        copy(0, 0).wait()
        copy(0, 0).wait()
        return c

    lax.fori_loop(0, TM, drain, 0)


def _dispatch(dest, pad_end, padded, h2):
    return pl.pallas_call(
        _dispatch_kernel,
        grid_spec=pltpu.PrefetchScalarGridSpec(
            num_scalar_prefetch=3, grid=(NT,),
            in_specs=[pl.BlockSpec((TM, SUB, PK), lambda i, d, pe, pd: (i, 0, 0))],
            out_specs=pl.BlockSpec(memory_space=pl.ANY),
            scratch_shapes=[pltpu.VMEM((BM, SUB, PK), I32), pltpu.SemaphoreType.DMA(()),
                            pltpu.SemaphoreType.DMA(())]),
        out_shape=jax.ShapeDtypeStruct((ROWS, SUB, PK), I32),
        compiler_params=pltpu.CompilerParams(dimension_semantics=("arbitrary",),
                                             has_side_effects=True),
        name="dispatch",
    )(dest, pad_end, padded, h2)


def _pack_bf16_pairs(h):
    bits = lax.bitcast_convert_type(h.astype(BF16).astype(F32), I32)
    half = h.shape[1] // 2
    return (bits[:, :half] & -65536) | lax.shift_right_logical(bits[:, half:], 16)


GRP = 32


def _expert_kernel(eo_ref, es_ref, n_ref, x_ref, perm_ref, wg_hbm, wu_hbm, wd_hbm, y_ref,
                   wgf, wuf, wdf, wgb, wub, wdb, xs_scr, wsem, *, layer):
    b = pl.program_id(0)
    k = eo_ref[b]
    first = jnp.logical_or(b == 0, k != eo_ref[jnp.maximum(b - 1, 0)])

    def copies(kk, slot):
        e = es_ref[kk]
        return (pltpu.make_async_copy(wg_hbm.at[layer, e], wgf.at[slot], wsem.at[slot, 0]),
                pltpu.make_async_copy(wu_hbm.at[layer, e], wuf.at[slot], wsem.at[slot, 1]),
                pltpu.make_async_copy(wd_hbm.at[layer, e], wdf.at[slot], wsem.at[slot, 2]))

    @pl.when(b == 0)
    def _():
        for c in copies(0, 0):
            c.start()

    @pl.when(jnp.logical_and(first, b < n_ref[0]))
    def _():
        slot = k % 2
        for c in copies(k, slot):
            c.wait()

        @pl.when(k + 1 < n_ref[1])
        def _():
            for c in copies(k + 1, 1 - slot):
                c.start()

        wgb[...] = wgf[slot].astype(BF16)
        wub[...] = wuf[slot].astype(BF16)
        wdb[...] = wdf[slot].astype(BF16)

    @pl.when(b < n_ref[0])
    def _():
        half = D_MODEL // 2
        for grp in range(BM // GRP):
            w = x_ref[grp * GRP:(grp + 1) * GRP].reshape(GRP * SUB, PK)
            hi = lax.bitcast_convert_type(w & -65536, F32).astype(BF16)
            lo = lax.bitcast_convert_type(w << 16, F32).astype(BF16)
            g = jnp.concatenate([hi, lo], axis=1)
            r = jnp.dot(perm_ref[...], g, preferred_element_type=F32).astype(BF16)
            rows = slice(grp * GRP, (grp + 1) * GRP)
            for s in range(SUB):
                xs_scr[rows, s * PK:(s + 1) * PK] = r[s * GRP:(s + 1) * GRP, 0:PK]
                xs_scr[rows, half + s * PK:half + (s + 1) * PK] = r[s * GRP:(s + 1) * GRP, PK:2 * PK]
        x = xs_scr[...]
        a = jnp.dot(x, wgb[...], preferred_element_type=F32)
        u = jnp.dot(x, wub[...], preferred_element_type=F32)
        act = (_silu(a) * u).astype(BF16)
        yp = _pack_bf16_pairs(jnp.dot(act, wdb[...], preferred_element_type=F32))
        for s in range(SUB):
            y_ref[:, s, :] = yp[:, s * PK:(s + 1) * PK]

    @pl.when(b >= n_ref[0])
    def _():
        y_ref[...] = jnp.zeros_like(y_ref)


def _experts(l, eo, es, nn, xb, perm, w_g, w_u, w_d):
    hbm = pl.BlockSpec(memory_space=pl.ANY)
    return pl.pallas_call(
        functools.partial(_expert_kernel, layer=l),
        grid_spec=pltpu.PrefetchScalarGridSpec(
            num_scalar_prefetch=3, grid=(NB,),
            in_specs=[pl.BlockSpec((BM, SUB, PK), lambda b, eo, es, n: (jnp.minimum(b, n[0] - 1), 0, 0)),
                      pl.BlockSpec((GRP * SUB, GRP * SUB), lambda b, eo, es, n: (0, 0)),
                      hbm, hbm, hbm],
            out_specs=pl.BlockSpec((BM, SUB, PK), lambda b, eo, es, n: (b, 0, 0)),
            scratch_shapes=[pltpu.VMEM((2, D_MODEL, D_EXPERT), F32),
                            pltpu.VMEM((2, D_MODEL, D_EXPERT), F32),
                            pltpu.VMEM((2, D_EXPERT, D_MODEL), F32),
                            pltpu.VMEM((D_MODEL, D_EXPERT), BF16),
                            pltpu.VMEM((D_MODEL, D_EXPERT), BF16),
                            pltpu.VMEM((D_EXPERT, D_MODEL), BF16),
                            pltpu.VMEM((BM, D_MODEL), BF16),
                            pltpu.SemaphoreType.DMA((2, 3))]),
        out_shape=jax.ShapeDtypeStruct((ROWS, SUB, PK), I32),
        compiler_params=_cp(("arbitrary",)),
        name="experts",
    )(eo, es, nn, xb, perm, w_g, w_u, w_d)


def _combine_kernel(dest_ref, y_hbm, xmid_ref, wcol_ref, ga2_ref, perm_ref, o_ref, b1, b2, sem, *, n):
    i = pl.program_id(0)

    def copy(d, buf, slot, r):
        return pltpu.make_async_copy(y_hbm.at[pl.ds(d, 1)], buf.at[slot, pl.ds(r, 1)], sem.at[slot])

    def issue(tile, slot):
        def body(r, c):
            t = tile * TM + r
            copy(dest_ref[2 * t], b1, slot, r).start()
            copy(dest_ref[2 * t + 1], b2, slot, r).start(priority=1)
            return c
        lax.fori_loop(0, TM, body, 0)

    @pl.when(i == 0)
    def _():
        issue(0, 0)

    @pl.when(i + 1 < n)
    def _():
        issue(i + 1, (i + 1) % 2)

    slot = i % 2

    def drain(r, c):
        copy(0, b1, slot, 0).wait()
        copy(0, b2, slot, 0).wait()
        return c

    lax.fori_loop(0, TM, drain, 0)

    def unpack_group(buf, grp):
        w = buf[slot, grp * GRP:(grp + 1) * GRP].reshape(GRP * SUB, PK)
        hi = lax.bitcast_convert_type(w & -65536, F32).astype(BF16)
        lo = lax.bitcast_convert_type(w << 16, F32).astype(BF16)
        return jnp.dot(perm_ref[...], jnp.concatenate([hi, lo], axis=1), preferred_element_type=F32)

    half = D_MODEL // 2
    for grp in range(TM // GRP):
        rows = slice(grp * GRP, (grp + 1) * GRP)
        w1 = wcol_ref[rows, 0:1]
        w2 = wcol_ref[rows, 1:2]
        r1 = unpack_group(b1, grp)
        r2 = unpack_group(b2, grp)
        for s in range(SUB):
            sr = slice(s * GRP, (s + 1) * GRP)
            for c0, lanes in ((s * PK, slice(0, PK)), (half + s * PK, slice(PK, 2 * PK))):
                cs = slice(c0, c0 + PK)
                o_ref[rows, cs] = xmid_ref[rows, cs] + ga2_ref[:, cs] * (w1 * r1[sr, lanes] + w2 * r2[sr, lanes])


def _combine(dest, yb, xmid, wcol, modt, perm, n_tiles):
    tile = lambda n: pl.BlockSpec((TM, n), lambda i, d: (i, 0))
    return pl.pallas_call(
        functools.partial(_combine_kernel, n=n_tiles),
        grid_spec=pltpu.PrefetchScalarGridSpec(
            num_scalar_prefetch=1, grid=(n_tiles,),
            in_specs=[pl.BlockSpec(memory_space=pl.ANY), tile(D_MODEL), tile(128),
                      pl.BlockSpec((None, None, 1, D_MODEL), lambda i, d: (i // NT_LAT, 5, 0, 0)),
                      pl.BlockSpec((GRP * SUB, GRP * SUB), lambda i, d: (0, 0))],
            out_specs=tile(D_MODEL),
            scratch_shapes=[pltpu.VMEM((2, TM, SUB, PK), I32), pltpu.VMEM((2, TM, SUB, PK), I32),
                            pltpu.SemaphoreType.DMA((2,))]),
        out_shape=jax.ShapeDtypeStruct((n_tiles * TM, D_MODEL), F32),
        compiler_params=_cp(("arbitrary",)),
        name="combine",
    )(dest, yb, xmid, wcol, modt, perm)


def _perm_experts(a):
    return a.reshape(a.shape[:-1] + (N_GROUPS, EPG)).swapaxes(-1, -2).reshape(a.shape)


def kernel(x, c, ctx, c_ctx, w_mod, b_mod, g_norm1, g_norm2, w_in, w_fourier, g_q, g_k, sink,
           g_branch_f, g_branch_a, w_out, w_router, b_router, w_exp_gate, w_exp_up, w_exp_down,
           w_sh_gate, w_sh_up, w_sh_down):
    assert x.shape == (1, SEQ, D_MODEL) and ctx.shape == (1, CTX, D_MODEL)
    tb = _tables()
    xs = jnp.concatenate([x[0], ctx[0]], axis=0)

    cc = jnp.zeros((8, D_MODEL), F32).at[0].set(c[0]).at[1].set(c_ctx)
    mods = _modulation(cc, w_mod, b_mod)

    wr_t = _perm_experts(w_router).T
    wrh = wr_t.astype(BF16)
    wrl = (wr_t - wrh.astype(F32)).astype(BF16)
    br = jnp.broadcast_to(_perm_experts(b_router)[:, None], (N_EXPERTS, TM)).astype(F32)

    for l in range(DEPTH):
        last = l == DEPTH - 1
        modt = mods[l, 0:2].reshape(2, 6, 1, D_MODEL)
        row = lambda a: a[l].reshape(1, -1)
        u, q, k, v = _project(xs, row(g_norm1), modt, w_in[l].astype(BF16), row(g_q), row(g_k), tb)
        an = _attention(q, k, v, sink[l], row(g_branch_a))
        fn = _fourier(u, w_fourier[l].astype(BF16), row(g_branch_f), tb)
        xmid, h2, info, wcol, cnt = _merge(
            xs, fn, an, w_out[l].astype(BF16), modt, row(g_norm2),
            w_sh_gate[l].astype(BF16), w_sh_up[l].astype(BF16), w_sh_down[l].astype(BF16),
            wrh, wrl, br, tb)

        xs = _routed(l, info, cnt, h2, xmid, wcol, modt, w_exp_gate, w_exp_up, w_exp_down, tb,
                     NT_LAT if last else NT)
    return xs[None]


def _routed(l, info, cnt, h2, xmid, wcol, modt, w_g, w_u, w_d, tb, n_tiles):
    counts = cnt[:, 0].astype(I32).reshape(EPG, N_GROUPS).T.reshape(N_EXPERTS)
    padded = ((counts + BM - 1) // BM) * BM
    pad_end = jnp.cumsum(padded).astype(I32)
    pad_start = pad_end - padded
    start = jnp.sum(jnp.where(info[0:2, :, None] == jnp.arange(N_EXPERTS, dtype=I32),
                              pad_start, 0), axis=-1)
    dest = (start + info[2:4]).T.reshape(2 * T)
    block_e = jnp.minimum(
        jnp.searchsorted(pad_end, jnp.arange(NB, dtype=I32) * BM, side="right"),
        N_EXPERTS - 1).astype(I32)
    ids = jnp.arange(N_EXPERTS, dtype=I32)
    used = padded > 0
    ordinal = jnp.cumsum(used.astype(I32)) - 1
    eo = jnp.sum(jnp.where(block_e[:, None] == ids, ordinal, 0), axis=-1)
    es = jnp.sum(jnp.where(jnp.logical_and(used, ordinal == ids[:, None]), ids, 0), axis=-1)
    nn = jnp.stack([pad_end[N_EXPERTS - 1] // BM, ordinal[N_EXPERTS - 1] + 1]).astype(I32)
    xb = _dispatch(dest, pad_end, padded, h2)
    yb = _experts(l, eo.astype(I32), es.astype(I32), nn, xb, tb["perm"], w_g, w_u, w_d)
    return _combine(dest, yb, xmid, wcol, modt, tb["perm"], n_tiles)
```

```python
import functools

import numpy as np
import jax
import jax.numpy as jnp
from jax import lax
from jax.experimental import pallas as pl
from jax.experimental.pallas import tpu as pltpu

F32 = jnp.float32
BF16 = jnp.bfloat16
I32 = jnp.int32

D_MODEL = 2048
SEQ = 8192
CTX = 256
T = SEQ + CTX
DEPTH = 4
GRID_W = 64
HEAD_DIM = 128
D_F = 512
N_FG = 4
N_HEADS = 12
N_KV = 4
Q_PER_KV = 3
D_ATTN = N_HEADS * HEAD_DIM
D_KV = N_KV * HEAD_DIM
D_IN = D_F + D_ATTN + 2 * D_KV
ROPE_BASE = 10000.0
N_EXPERTS = 32
N_GROUPS = 8
EPG = 4
D_EXPERT = 512
D_SHARED = 512
EPS = 1e-6
NEG = -1e30

TM = 256
NT = T // TM
NT_LAT = SEQ // TM
QB = 128
NQB = T // QB
NQB_LAT = SEQ // QB
BM = 512
NB = (2 * T + BM - 1) // BM + N_EXPERTS
ROWS = NB * BM
N1 = 128
N2 = 64
SUB = 8
RL = D_MODEL // SUB
PK = D_MODEL // 2 // SUB
VMEM_LIMIT = 56 * 1024 * 1024


def _cp(sem):
    return pltpu.CompilerParams(dimension_semantics=sem, vmem_limit_bytes=VMEM_LIMIT)


def _silu(a):
    return a * jax.nn.sigmoid(a)


@functools.lru_cache(maxsize=None)
def _tables():
    t = np.arange(SEQ)
    row = (t // GRID_W).astype(np.float64)
    col = (t % GRID_W).astype(np.float64)
    inv = ROPE_BASE ** (-np.arange(0, HEAD_DIM // 2, 2, dtype=np.float64) / (HEAD_DIM // 2))
    ar = row[:, None] * inv
    ac = col[:, None] * inv
    ang = np.concatenate([ar, ar, ac, ac], axis=-1)
    cos = np.concatenate([np.cos(ang), np.ones((CTX, HEAD_DIM))], axis=0)
    sin = np.concatenate([np.sin(ang), np.zeros((CTX, HEAD_DIM))], axis=0)
    first = (np.arange(HEAD_DIM) % 64) < 32
    rope_a = np.where(first[None, :], -sin, 0.0)
    rope_b = np.where(first[None, :], 0.0, sin)

    def cs(n):
        k = np.arange(n)
        a = 2.0 * np.pi * np.outer(k, k) / n
        return np.cos(a), np.sin(a)

    c64, s64 = cs(N2)
    c128, s128 = cs(N1)
    c256, s256 = cs(CTX)
    m1 = np.concatenate([c64, -s64], axis=0)
    m2 = np.concatenate([c128, s128], axis=0)
    mc = np.concatenate([c256, s256], axis=0)
    csch = np.concatenate([c128, s128], axis=0)
    a = 2.0 * np.pi * np.outer(np.arange(N1), np.arange(N2)) / SEQ
    twr = np.broadcast_to(np.cos(a)[:, :, None], (N1, N2, 128))
    twi = np.broadcast_to(-np.sin(a)[:, :, None], (N1, N2, 128))
    tri = (np.arange(TM)[:, None] < np.arange(TM)[None, :]).astype(np.float32)
    perm = np.zeros((GRP * SUB, GRP * SUB), np.float32)
    rr, ss = np.meshgrid(np.arange(GRP), np.arange(SUB), indexing="ij")
    perm[(ss * GRP + rr).ravel(), (rr * SUB + ss).ravel()] = 1.0
    return dict(
        cos=jnp.asarray(cos, F32), rope_a=jnp.asarray(rope_a, F32), rope_b=jnp.asarray(rope_b, F32),
        m1=jnp.asarray(m1, BF16), m2=jnp.asarray(m2, BF16), mc=jnp.asarray(mc, BF16),
        csch=jnp.asarray(csch, BF16), twr=jnp.asarray(twr, F32), twi=jnp.asarray(twi, F32),
        tri=jnp.asarray(tri, BF16), perm=jnp.asarray(perm, BF16))


MOD_TN = 1024


def _mod_kernel(cc_ref, w_ref, b_ref, o_ref):
    a = _silu(cc_ref[...])
    o_ref[...] = jnp.dot(a.astype(BF16), w_ref[...].astype(BF16),
                         preferred_element_type=F32) + b_ref[...]


def _modulation(cc, w_mod, b_mod):
    nl = w_mod.shape[0]
    n6 = w_mod.shape[2]
    return pl.pallas_call(
        _mod_kernel,
        grid=(nl, n6 // MOD_TN),
        in_specs=[pl.BlockSpec((8, D_MODEL), lambda l, j: (0, 0)),
                  pl.BlockSpec((None, D_MODEL, MOD_TN), lambda l, j: (l, 0, j)),
                  pl.BlockSpec((None, 1, MOD_TN), lambda l, j: (l, 0, j))],
        out_specs=pl.BlockSpec((None, 8, MOD_TN), lambda l, j: (l, 0, j)),
        out_shape=jax.ShapeDtypeStruct((nl, 8, n6), F32),
        compiler_params=_cp(("arbitrary", "arbitrary")),
        name="modulation",
    )(cc, w_mod, b_mod.reshape(nl, 1, n6))


def _mod_spec(chunk):
    return pl.BlockSpec((None, None, 1, D_MODEL), lambda i: (i // NT_LAT, chunk, 0, 0))


def _row_spec(n):
    return pl.BlockSpec((1, n), lambda i: (0, 0))


PCW = 256


def _proj_kernel(x_ref, g_ref, sc_ref, sh_ref, w_ref, gq_ref, gk_ref, cos_ref, ra_ref, rb_ref,
                 u_ref, q_ref, k_ref, v_ref, hb_ref):
    x = x_ref[...]
    ms = jnp.mean(x * x, axis=-1, keepdims=True)
    h = x * lax.rsqrt(ms + EPS) * g_ref[...]
    h = h * (1.0 + sc_ref[...]) + sh_ref[...]
    hb_ref[...] = h.astype(BF16)
    cos = cos_ref[...]
    ra = ra_ref[...]
    rb = rb_ref[...]

    def head(t, g):
        m = jnp.mean(t * t, axis=-1, keepdims=True)
        t = t * lax.rsqrt(m + EPS) * g
        return t * cos + pltpu.roll(t, 96, 1) * ra + pltpu.roll(t, 32, 1) * rb

    scale = HEAD_DIM ** -0.5
    for c in range(D_IN // PCW):
        col = c * PCW
        p = jnp.dot(hb_ref[...], w_ref[:, col:col + PCW], preferred_element_type=F32)
        if col < D_F:
            u_ref[:, col:col + PCW] = p.astype(BF16)
        elif col < D_F + D_ATTN:
            o = col - D_F
            for j in range(PCW // HEAD_DIM):
                t = head(p[:, j * HEAD_DIM:(j + 1) * HEAD_DIM], gq_ref[...]) * scale
                q_ref[:, o + j * HEAD_DIM:o + (j + 1) * HEAD_DIM] = t.astype(BF16)
        elif col < D_F + D_ATTN + D_KV:
            o = col - D_F - D_ATTN
            for j in range(PCW // HEAD_DIM):
                t = head(p[:, j * HEAD_DIM:(j + 1) * HEAD_DIM], gk_ref[...])
                k_ref[:, o + j * HEAD_DIM:o + (j + 1) * HEAD_DIM] = t.astype(BF16)
        else:
            o = col - D_F - D_ATTN - D_KV
            v_ref[:, o:o + PCW] = p.astype(BF16)


def _project(xs, g1, modt, w_in_bf, gq, gk, tb):
    tile = lambda n: pl.BlockSpec((TM, n), lambda i: (i, 0))
    return pl.pallas_call(
        _proj_kernel,
        grid=(NT,),
        in_specs=[tile(D_MODEL), _row_spec(D_MODEL), _mod_spec(1), _mod_spec(0),
                  pl.BlockSpec((D_MODEL, D_IN), lambda i: (0, 0)),
                  _row_spec(HEAD_DIM), _row_spec(HEAD_DIM),
                  tile(HEAD_DIM), tile(HEAD_DIM), tile(HEAD_DIM)],
        out_specs=[tile(D_F), tile(D_ATTN), tile(D_KV), tile(D_KV)],
        out_shape=[jax.ShapeDtypeStruct((T, D_F), BF16), jax.ShapeDtypeStruct((T, D_ATTN), BF16),
                   jax.ShapeDtypeStruct((T, D_KV), BF16), jax.ShapeDtypeStruct((T, D_KV), BF16)],
        scratch_shapes=[pltpu.VMEM((TM, D_MODEL), BF16)],
        compiler_params=_cp(("arbitrary",)),
        name="norm_proj",
    )(xs, g1, modt, modt, w_in_bf, gq, gk, tb["cos"], tb["rope_a"], tb["rope_b"])


def _attn_kernel(sink_ref, q_ref, kp_ref, kc_ref, kn_ref, vp_ref, vc_ref, vn_ref, kx_ref, vx_ref,
                 ga_ref, o_ref, a_scr):
    b = pl.program_id(0)
    lat = b < NQB_LAT
    f_prev = jnp.logical_and(lat, b > 0).astype(I32)
    f_cur = lat.astype(I32)
    f_next = (b < NQB_LAT - 1).astype(I32)
    n3 = Q_PER_KV * QB
    r = lax.broadcasted_iota(I32, (n3, 3 * QB), 0) & (QB - 1)
    cidx = lax.broadcasted_iota(I32, (n3, 3 * QB), 1)
    blk = cidx >> 7
    j = cidx & (QB - 1)
    ok = jnp.where(blk == 0, (j >= r).astype(I32) * f_prev,
                   jnp.where(blk == 1, f_cur, (j <= r).astype(I32) * f_next))
    valid = ok > 0
    dn = (((1,), (1,)), ((), ()))
    for hk in range(N_KV):
        ks = slice(hk * HEAD_DIM, (hk + 1) * HEAD_DIM)
        qs = jnp.concatenate(
            [q_ref[:, (hk * Q_PER_KV + g) * HEAD_DIM:(hk * Q_PER_KV + g + 1) * HEAD_DIM]
             for g in range(Q_PER_KV)], axis=0)
        kw = jnp.concatenate([kp_ref[:, ks], kc_ref[:, ks], kn_ref[:, ks]], axis=0)
        vw = jnp.concatenate([vp_ref[:, ks], vc_ref[:, ks], vn_ref[:, ks]], axis=0)
        s = lax.dot_general(qs, kw, dn, preferred_element_type=F32)
        s = jnp.where(valid, s, NEG)
        sx = lax.dot_general(qs, kx_ref[:, ks], dn, preferred_element_type=F32)
        sk = jnp.concatenate(
            [jnp.full((QB, 1), sink_ref[hk * Q_PER_KV + g], F32) for g in range(Q_PER_KV)], axis=0)
        m = jnp.maximum(jnp.maximum(jnp.max(s, axis=-1, keepdims=True),
                                    jnp.max(sx, axis=-1, keepdims=True)), sk)
        e = jnp.exp(s - m)
        ex = jnp.exp(sx - m)
        den = (jnp.sum(e, axis=-1, keepdims=True) + jnp.sum(ex, axis=-1, keepdims=True)
               + jnp.exp(sk - m))
        o = (jnp.dot(e.astype(BF16), vw, preferred_element_type=F32)
             + jnp.dot(ex.astype(BF16), vx_ref[:, ks], preferred_element_type=F32)) / den
        for g in range(Q_PER_KV):
            h = hk * Q_PER_KV + g
            a_scr[:, h * HEAD_DIM:(h + 1) * HEAD_DIM] = o[g * QB:(g + 1) * QB]
    a = a_scr[...]
    ms = jnp.mean(a * a, axis=-1, keepdims=True)
    o_ref[...] = (a * lax.rsqrt(ms + EPS) * ga_ref[...]).astype(BF16)


def _attention(q, k, v, sink, g_a):
    kv = lambda f: pl.BlockSpec((QB, D_KV), f)
    prev = lambda b: (jnp.maximum(b - 1, 0), 0)
    cur = lambda b: (b, 0)
    nxt = lambda b: (jnp.minimum(b + 1, NQB - 1), 0)
    ctx = pl.BlockSpec((CTX, D_KV), lambda b: (SEQ // CTX, 0))
    return pl.pallas_call(
        _attn_kernel,
        grid=(NQB,),
        in_specs=[pl.BlockSpec(memory_space=pltpu.SMEM),
                  pl.BlockSpec((QB, D_ATTN), cur),
                  kv(prev), kv(cur), kv(nxt), kv(prev), kv(cur), kv(nxt), ctx, ctx,
                  _row_spec(D_ATTN)],
        out_specs=pl.BlockSpec((QB, D_ATTN), cur),
        out_shape=jax.ShapeDtypeStruct((T, D_ATTN), BF16),
        scratch_shapes=[pltpu.VMEM((QB, D_ATTN), F32)],
        compiler_params=_cp(("arbitrary",)),
        name="attention",
    )(sink, q, k, k, k, v, v, v, k, v, g_a)


F1_J = 8
F2_K = 4


def _f1_kernel(u_ref, m1_ref, twr_ref, twi_ref, z_ref):
    g = jnp.dot(m1_ref[...], u_ref[...], preferred_element_type=F32)
    for jl in range(F1_J):
        twr = twr_ref[jl]
        twi = twi_ref[jl]
        for lt in range(D_F // 128):
            c0 = jl * D_F + lt * 128
            gr = g[0:N2, c0:c0 + 128]
            gi = g[N2:2 * N2, c0:c0 + 128]
            z_ref[jl, 0:N2, lt * 128:(lt + 1) * 128] = (gr * twr - gi * twi).astype(BF16)
            z_ref[jl, N2:2 * N2, lt * 128:(lt + 1) * 128] = (gr * twi + gi * twr).astype(BF16)


def _f2_kernel(zr_ref, zi_ref, m2_ref, y_ref):
    pr = jnp.dot(m2_ref[...], zr_ref[...], preferred_element_type=F32)
    pi = jnp.dot(m2_ref[...], zi_ref[...], preferred_element_type=F32)
    yr = pr[0:N1] + pi[N1:2 * N1]
    yi = pi[0:N1] - pr[N1:2 * N1]
    for kl in range(F2_K):
        y_ref[:, kl * 2 * D_F:kl * 2 * D_F + D_F] = yr[:, kl * D_F:(kl + 1) * D_F].astype(BF16)
        y_ref[:, kl * 2 * D_F + D_F:(kl + 1) * 2 * D_F] = yi[:, kl * D_F:(kl + 1) * D_F].astype(BF16)


def _f3_kernel(y_ref, uc_ref, cs_ref, mc_ref, wf_ref, gf_ref, o_ref):
    i = pl.program_id(0)

    def finish(parts, scale):
        f = jnp.concatenate(parts, axis=1) * scale
        t = jnp.dot(f.astype(BF16), wf_ref[...], preferred_element_type=F32)
        ms = jnp.mean(t * t, axis=-1, keepdims=True)
        o_ref[...] = (t * lax.rsqrt(ms + EPS) * gf_ref[...]).astype(BF16)

    @pl.when(i < NT_LAT)
    def _():
        parts = []
        for g in range(N_FG):
            lhs = jnp.concatenate([y_ref[:, g * 128:(g + 1) * 128],
                                   y_ref[:, D_F + g * 128:D_F + (g + 1) * 128]], axis=1)
            parts.append(jnp.dot(lhs, cs_ref[...], preferred_element_type=F32))
        finish(parts, float((SEQ * 128) ** -0.5))

    @pl.when(i == NT_LAT)
    def _():
        pq = jnp.dot(mc_ref[...], uc_ref[...], preferred_element_type=F32)
        parts = []
        for g in range(N_FG):
            lhs = jnp.concatenate([pq[0:CTX, g * 128:(g + 1) * 128],
                                   -pq[CTX:2 * CTX, g * 128:(g + 1) * 128]], axis=1)
            parts.append(jnp.dot(lhs.astype(BF16), cs_ref[...], preferred_element_type=F32))
        finish(parts, float((CTX * 128) ** -0.5))


def _fourier(u, w_f_bf, g_f, tb):
    u2 = u.reshape(T // N1, N1 * D_F)
    z = pl.pallas_call(
        _f1_kernel,
        grid=(N1 // F1_J,),
        in_specs=[pl.BlockSpec((N2, F1_J * D_F), lambda c: (0, c)),
                  pl.BlockSpec((2 * N2, N2), lambda c: (0, 0)),
                  pl.BlockSpec((F1_J, N2, 128), lambda c: (c, 0, 0)),
                  pl.BlockSpec((F1_J, N2, 128), lambda c: (c, 0, 0))],
        out_specs=pl.BlockSpec((F1_J, 2 * N2, D_F), lambda c: (c, 0, 0)),
        out_shape=jax.ShapeDtypeStruct((N1, 2 * N2, D_F), BF16),
        compiler_params=_cp(("arbitrary",)),
        name="fourier_s1",
    )(u2, tb["m1"], tb["twr"], tb["twi"])
    z2 = z.reshape(N1, 2 * N2 * D_F)
    nk = N2 // F2_K
    y = pl.pallas_call(
        _f2_kernel,
        grid=(nk,),
        in_specs=[pl.BlockSpec((N1, F2_K * D_F), lambda k: (0, k)),
                  pl.BlockSpec((N1, F2_K * D_F), lambda k: (0, nk + k)),
                  pl.BlockSpec((2 * N1, N1), lambda k: (0, 0))],
        out_specs=pl.BlockSpec((N1, F2_K * 2 * D_F), lambda k: (0, k)),
        out_shape=jax.ShapeDtypeStruct((N1, N2 * 2 * D_F), BF16),
        compiler_params=_cp(("arbitrary",)),
        name="fourier_s2",
    )(z2, z2, tb["m2"])
    y2 = y.reshape(SEQ, 2 * D_F)
    return pl.pallas_call(
        _f3_kernel,
        grid=(NT,),
        in_specs=[pl.BlockSpec((TM, 2 * D_F), lambda i: (jnp.minimum(i, NT_LAT - 1), 0)),
                  pl.BlockSpec((CTX, D_F), lambda i: (SEQ // CTX, 0)),
                  pl.BlockSpec((2 * 128, 128), lambda i: (0, 0)),
                  pl.BlockSpec((2 * CTX, CTX), lambda i: (0, 0)),
                  pl.BlockSpec((D_F, D_F), lambda i: (0, 0)),
                  _row_spec(D_F)],
        out_specs=pl.BlockSpec((TM, D_F), lambda i: (i, 0)),
        out_shape=jax.ShapeDtypeStruct((T, D_F), BF16),
        compiler_params=_cp(("arbitrary",)),
        name="fourier_s3",
    )(y2, u, tb["csch"], tb["mc"], w_f_bf, g_f)


def _merge_kernel(x_ref, fn_ref, an_ref, wo_ref, ga1_ref, g2_ref, sc2_ref, sh2_ref, ga2_ref,
                  wsg_ref, wsu_ref, wsd_ref, wrh_ref, wrl_ref, br_ref, tri_ref,
                  xmid_ref, h2_ref, info_ref, wcol_ref, cnt_ref, carry_scr):
    i = pl.program_id(0)

    @pl.when(i == 0)
    def _():
        carry_scr[...] = jnp.zeros_like(carry_scr)

    m = (jnp.dot(fn_ref[...], wo_ref[0:D_F, :], preferred_element_type=F32)
         + jnp.dot(an_ref[...], wo_ref[D_F:D_MODEL, :], preferred_element_type=F32))
    x1 = x_ref[...] + ga1_ref[...] * m
    ms = jnp.mean(x1 * x1, axis=-1, keepdims=True)
    h = x1 * lax.rsqrt(ms + EPS) * g2_ref[...]
    h = h * (1.0 + sc2_ref[...]) + sh2_ref[...]
    hb = h.astype(BF16)
    hp = _pack_bf16_pairs(h)
    for s in range(SUB):
        h2_ref[:, s, :] = hp[:, s * PK:(s + 1) * PK]

    a = jnp.dot(hb, wsg_ref[...], preferred_element_type=F32)
    u = jnp.dot(hb, wsu_ref[...], preferred_element_type=F32)
    act = (_silu(a) * u).astype(BF16)
    ys = jnp.dot(act, wsd_ref[...], preferred_element_type=F32)
    xmid_ref[...] = x1 + ga2_ref[...] * ys

    hl = (h - hb.astype(F32)).astype(BF16)
    dn = (((1,), (1,)), ((), ()))
    lg = (lax.dot_general(wrh_ref[...], hb, dn, preferred_element_type=F32)
          + (lax.dot_general(wrl_ref[...], hb, dn, preferred_element_type=F32)
             + lax.dot_general(wrh_ref[...], hl, dn, preferred_element_type=F32)))
    score = jax.nn.sigmoid(lg)
    sel = score + br_ref[...]
    s = [sel[jj * N_GROUPS:(jj + 1) * N_GROUPS] for jj in range(EPG)]
    sr = [score[jj * N_GROUPS:(jj + 1) * N_GROUPS] for jj in range(EPG)]
    hi01, lo01 = jnp.maximum(s[0], s[1]), jnp.minimum(s[0], s[1])
    hi23, lo23 = jnp.maximum(s[2], s[3]), jnp.minimum(s[2], s[3])
    gscore = jnp.maximum(hi01, hi23) + jnp.maximum(jnp.minimum(hi01, hi23), jnp.maximum(lo01, lo23))
    gi = lax.broadcasted_iota(I32, (N_GROUPS, TM), 0)
    gmax = jnp.max(gscore, axis=0, keepdims=True)
    g_idx = jnp.min(jnp.where(gscore == gmax, gi, N_GROUPS), axis=0, keepdims=True)
    gsel = gi == g_idx
    v = [jnp.sum(jnp.where(gsel, s[jj], 0.0), axis=0, keepdims=True) for jj in range(EPG)]
    vr = [jnp.sum(jnp.where(gsel, sr[jj], 0.0), axis=0, keepdims=True) for jj in range(EPG)]

    def first_argmax(vals):
        best = vals[0]
        idx = jnp.zeros((1, TM), I32)
        for jj in range(1, EPG):
            upd = vals[jj] > best
            best = jnp.where(upd, vals[jj], best)
            idx = jnp.where(upd, jj, idx)
        return idx

    i1 = first_argmax(v)
    i2 = first_argmax([jnp.where(i1 == jj, -jnp.inf, v[jj]) for jj in range(EPG)])

    def pick(vals, idx):
        out = vals[0]
        for jj in range(1, EPG):
            out = jnp.where(idx == jj, vals[jj], out)
        return out

    w1 = pick(vr, i1)
    w2 = pick(vr, i2)
    wsum = w1 + w2
    w1 = w1 / wsum
    w2 = w2 / wsum

    ri = lax.broadcasted_iota(I32, (N_EXPERTS, TM), 0)
    oh1 = ri == i1 * N_GROUPS + g_idx
    oh2 = ri == i2 * N_GROUPS + g_idx
    oh = jnp.logical_or(oh1, oh2).astype(F32)
    pre = jnp.dot(oh.astype(BF16), tri_ref[...], preferred_element_type=F32) + carry_scr[:, 0:1]
    r1 = jnp.sum(jnp.where(oh1, pre, 0.0), axis=0, keepdims=True)
    r2 = jnp.sum(jnp.where(oh2, pre, 0.0), axis=0, keepdims=True)
    carry = carry_scr[...] + jnp.sum(oh, axis=1, keepdims=True)
    carry_scr[...] = carry
    cnt_ref[...] = carry

    zi = jnp.zeros((4, TM), I32)
    info_ref[...] = jnp.concatenate(
        [g_idx * EPG + i1, g_idx * EPG + i2, r1.astype(I32), r2.astype(I32), zi], axis=0)
    wrow = jnp.concatenate([w1, w2, jnp.zeros((126, TM), F32)], axis=0)
    wcol_ref[...] = wrow.T


def _merge(xs, fn, an, w_out_bf, modt, g2, wsg, wsu, wsd, wrh, wrl, br, tb):
    tile = lambda n: pl.BlockSpec((TM, n), lambda i: (i, 0))
    full = lambda a, b: pl.BlockSpec((a, b), lambda i: (0, 0))
    return pl.pallas_call(
        _merge_kernel,
        grid=(NT,),
        in_specs=[tile(D_MODEL), tile(D_F), tile(D_ATTN), full(D_MODEL, D_MODEL),
                  _mod_spec(2), _row_spec(D_MODEL), _mod_spec(4), _mod_spec(3), _mod_spec(5),
                  full(D_MODEL, D_SHARED), full(D_MODEL, D_SHARED), full(D_SHARED, D_MODEL),
                  full(N_EXPERTS, D_MODEL), full(N_EXPERTS, D_MODEL), full(N_EXPERTS, TM),
                  full(TM, TM)],
        out_specs=[tile(D_MODEL), pl.BlockSpec((TM, SUB, PK), lambda i: (i, 0, 0)),
                   pl.BlockSpec((8, TM), lambda i: (0, i)),
                   pl.BlockSpec((TM, 128), lambda i: (i, 0)),
                   pl.BlockSpec((N_EXPERTS, 128), lambda i: (0, 0))],
        out_shape=[jax.ShapeDtypeStruct((T, D_MODEL), F32), jax.ShapeDtypeStruct((T, SUB, PK), I32),
                   jax.ShapeDtypeStruct((8, T), I32), jax.ShapeDtypeStruct((T, 128), F32),
                   jax.ShapeDtypeStruct((N_EXPERTS, 128), F32)],
        scratch_shapes=[pltpu.VMEM((N_EXPERTS, 128), F32)],
        compiler_params=_cp(("arbitrary",)),
        name="merge_route",
    )(xs, fn, an, w_out_bf, modt, g2, modt, modt, modt, wsg, wsu, wsd, wrh, wrl, br, tb["tri"])


def _dispatch_kernel(dest_ref, pend_ref, padded_ref, h_ref, xb_hbm, zero_scr, sem, zsem):
    i = pl.program_id(0)

    def zero_copy(row0):
        return pltpu.make_async_copy(zero_scr, xb_hbm.at[pl.ds(row0, BM)], zsem)

    @pl.when(i == 0)
    def _():
        zero_scr[...] = jnp.zeros_like(zero_scr)
        n_used = pend_ref[N_EXPERTS - 1] // BM
        for e in range(N_EXPERTS):
            @pl.when(padded_ref[e] > 0)
            def _():
                zero_copy(pend_ref[e] - BM).start()
        for blk in range(NB - N_EXPERTS, NB):
            @pl.when(blk >= n_used)
            def _():
                zero_copy(blk * BM).start()
        for e in range(N_EXPERTS):
            @pl.when(padded_ref[e] > 0)
            def _():
                zero_copy(0).wait()
        for blk in range(NB - N_EXPERTS, NB):
            @pl.when(blk >= n_used)
            def _():
                zero_copy(0).wait()

    def copy(r, d):
        return pltpu.make_async_copy(h_ref.at[pl.ds(r, 1)], xb_hbm.at[pl.ds(d, 1)], sem)

    def issue(r, c):
        t = i * TM + r
        copy(r, dest_ref[2 * t]).start()
        copy(r, dest_ref[2 * t + 1]).start(priority=1)
        return c

    lax.fori_loop(0, TM, issue, 0)

    def drain(r, c):
        copy(0, 0).wait()
        copy(0, 0).wait()
        return c

    lax.fori_loop(0, TM, drain, 0)


def _dispatch(dest, pad_end, padded, h2):
    return pl.pallas_call(
        _dispatch_kernel,
        grid_spec=pltpu.PrefetchScalarGridSpec(
            num_scalar_prefetch=3, grid=(NT,),
            in_specs=[pl.BlockSpec((TM, SUB, PK), lambda i, d, pe, pd: (i, 0, 0))],
            out_specs=pl.BlockSpec(memory_space=pl.ANY),
            scratch_shapes=[pltpu.VMEM((BM, SUB, PK), I32), pltpu.SemaphoreType.DMA(()),
                            pltpu.SemaphoreType.DMA(())]),
        out_shape=jax.ShapeDtypeStruct((ROWS, SUB, PK), I32),
        compiler_params=pltpu.CompilerParams(dimension_semantics=("arbitrary",),
                                             has_side_effects=True),
        name="dispatch",
    )(dest, pad_end, padded, h2)


def _pack_bf16_pairs(h):
    bits = lax.bitcast_convert_type(h.astype(BF16).astype(F32), I32)
    half = h.shape[1] // 2
    return (bits[:, :half] & -65536) | lax.shift_right_logical(bits[:, half:], 16)


GRP = 32


def _expert_kernel(eo_ref, es_ref, n_ref, x_ref, perm_ref, wg_hbm, wu_hbm, wd_hbm, y_ref,
                   wgf, wuf, wdf, wgb, wub, wdb, xs_scr, wsem, *, layer):
    b = pl.program_id(0)
    k = eo_ref[b]
    first = jnp.logical_or(b == 0, k != eo_ref[jnp.maximum(b - 1, 0)])

    def copies(kk, slot):
        e = es_ref[kk]
        return (pltpu.make_async_copy(wg_hbm.at[layer, e], wgf.at[slot], wsem.at[slot, 0]),
                pltpu.make_async_copy(wu_hbm.at[layer, e], wuf.at[slot], wsem.at[slot, 1]),
                pltpu.make_async_copy(wd_hbm.at[layer, e], wdf.at[slot], wsem.at[slot, 2]))

    @pl.when(b == 0)
    def _():
        for c in copies(0, 0):
            c.start()

    @pl.when(jnp.logical_and(first, b < n_ref[0]))
    def _():
        slot = k % 2
        for c in copies(k, slot):
            c.wait()

        @pl.when(k + 1 < n_ref[1])
        def _():
            for c in copies(k + 1, 1 - slot):
                c.start()

        wgb[...] = wgf[slot].astype(BF16)
        wub[...] = wuf[slot].astype(BF16)
        wdb[...] = wdf[slot].astype(BF16)

    @pl.when(b < n_ref[0])
    def _():
        half = D_MODEL // 2
        for grp in range(BM // GRP):
            w = x_ref[grp * GRP:(grp + 1) * GRP].reshape(GRP * SUB, PK)
            hi = lax.bitcast_convert_type(w & -65536, F32).astype(BF16)
            lo = lax.bitcast_convert_type(w << 16, F32).astype(BF16)
            g = jnp.concatenate([hi, lo], axis=1)
            r = jnp.dot(perm_ref[...], g, preferred_element_type=F32).astype(BF16)
            rows = slice(grp * GRP, (grp + 1) * GRP)
            for s in range(SUB):
                xs_scr[rows, s * PK:(s + 1) * PK] = r[s * GRP:(s + 1) * GRP, 0:PK]
                xs_scr[rows, half + s * PK:half + (s + 1) * PK] = r[s * GRP:(s + 1) * GRP, PK:2 * PK]
        x = xs_scr[...]
        a = jnp.dot(x, wgb[...], preferred_element_type=F32)
        u = jnp.dot(x, wub[...], preferred_element_type=F32)
        act = (_silu(a) * u).astype(BF16)
        yp = _pack_bf16_pairs(jnp.dot(act, wdb[...], preferred_element_type=F32))
        for s in range(SUB):
            y_ref[:, s, :] = yp[:, s * PK:(s + 1) * PK]

    @pl.when(b >= n_ref[0])
    def _():
        y_ref[...] = jnp.zeros_like(y_ref)


def _experts(l, eo, es, nn, xb, perm, w_g, w_u, w_d):
    hbm = pl.BlockSpec(memory_space=pl.ANY)
    return pl.pallas_call(
        functools.partial(_expert_kernel, layer=l),
        grid_spec=pltpu.PrefetchScalarGridSpec(
            num_scalar_prefetch=3, grid=(NB,),
            in_specs=[pl.BlockSpec((BM, SUB, PK), lambda b, eo, es, n: (jnp.minimum(b, n[0] - 1), 0, 0)),
                      pl.BlockSpec((GRP * SUB, GRP * SUB), lambda b, eo, es, n: (0, 0)),
                      hbm, hbm, hbm],
            out_specs=pl.BlockSpec((BM, SUB, PK), lambda b, eo, es, n: (b, 0, 0)),
            scratch_shapes=[pltpu.VMEM((2, D_MODEL, D_EXPERT), F32),
                            pltpu.VMEM((2, D_MODEL, D_EXPERT), F32),
                            pltpu.VMEM((2, D_EXPERT, D_MODEL), F32),
                            pltpu.VMEM((D_MODEL, D_EXPERT), BF16),
                            pltpu.VMEM((D_MODEL, D_EXPERT), BF16),
                            pltpu.VMEM((D_EXPERT, D_MODEL), BF16),
                            pltpu.VMEM((BM, D_MODEL), BF16),
                            pltpu.SemaphoreType.DMA((2, 3))]),
        out_shape=jax.ShapeDtypeStruct((ROWS, SUB, PK), I32),
        compiler_params=_cp(("arbitrary",)),
        name="experts",
    )(eo, es, nn, xb, perm, w_g, w_u, w_d)


def _combine_kernel(dest_ref, y_hbm, xmid_ref, wcol_ref, ga2_ref, perm_ref, o_ref, b1, b2, sem, *, n):
    i = pl.program_id(0)

    def copy(d, buf, slot, r):
        return pltpu.make_async_copy(y_hbm.at[pl.ds(d, 1)], buf.at[slot, pl.ds(r, 1)], sem.at[slot])

    def issue(tile, slot):
        def body(r, c):
            t = tile * TM + r
            copy(dest_ref[2 * t], b1, slot, r).start()
            copy(dest_ref[2 * t + 1], b2, slot, r).start(priority=1)
            return c
        lax.fori_loop(0, TM, body, 0)

    @pl.when(i == 0)
    def _():
        issue(0, 0)

    @pl.when(i + 1 < n)
    def _():
        issue(i + 1, (i + 1) % 2)

    slot = i % 2

    def drain(r, c):
        copy(0, b1, slot, 0).wait()
        copy(0, b2, slot, 0).wait()
        return c

    lax.fori_loop(0, TM, drain, 0)

    def unpack_group(buf, grp):
        w = buf[slot, grp * GRP:(grp + 1) * GRP].reshape(GRP * SUB, PK)
        hi = lax.bitcast_convert_type(w & -65536, F32).astype(BF16)
        lo = lax.bitcast_convert_type(w << 16, F32).astype(BF16)
        return jnp.dot(perm_ref[...], jnp.concatenate([hi, lo], axis=1), preferred_element_type=F32)

    half = D_MODEL // 2
    for grp in range(TM // GRP):
        rows = slice(grp * GRP, (grp + 1) * GRP)
        w1 = wcol_ref[rows, 0:1]
        w2 = wcol_ref[rows, 1:2]
        r1 = unpack_group(b1, grp)
        r2 = unpack_group(b2, grp)
        for s in range(SUB):
            sr = slice(s * GRP, (s + 1) * GRP)
            for c0, lanes in ((s * PK, slice(0, PK)), (half + s * PK, slice(PK, 2 * PK))):
                cs = slice(c0, c0 + PK)
                o_ref[rows, cs] = xmid_ref[rows, cs] + ga2_ref[:, cs] * (w1 * r1[sr, lanes] + w2 * r2[sr, lanes])


def _combine(dest, yb, xmid, wcol, modt, perm, n_tiles):
    tile = lambda n: pl.BlockSpec((TM, n), lambda i, d: (i, 0))
    return pl.pallas_call(
        functools.partial(_combine_kernel, n=n_tiles),
        grid_spec=pltpu.PrefetchScalarGridSpec(
            num_scalar_prefetch=1, grid=(n_tiles,),
            in_specs=[pl.BlockSpec(memory_space=pl.ANY), tile(D_MODEL), tile(128),
                      pl.BlockSpec((None, None, 1, D_MODEL), lambda i, d: (i // NT_LAT, 5, 0, 0)),
                      pl.BlockSpec((GRP * SUB, GRP * SUB), lambda i, d: (0, 0))],
            out_specs=tile(D_MODEL),
            scratch_shapes=[pltpu.VMEM((2, TM, SUB, PK), I32), pltpu.VMEM((2, TM, SUB, PK), I32),
                            pltpu.SemaphoreType.DMA((2,))]),
        out_shape=jax.ShapeDtypeStruct((n_tiles * TM, D_MODEL), F32),
        compiler_params=_cp(("arbitrary",)),
        name="combine",
    )(dest, yb, xmid, wcol, modt, perm)


def _perm_experts(a):
    return a.reshape(a.shape[:-1] + (N_GROUPS, EPG)).swapaxes(-1, -2).reshape(a.shape)


def kernel(x, c, ctx, c_ctx, w_mod, b_mod, g_norm1, g_norm2, w_in, w_fourier, g_q, g_k, sink,
           g_branch_f, g_branch_a, w_out, w_router, b_router, w_exp_gate, w_exp_up, w_exp_down,
           w_sh_gate, w_sh_up, w_sh_down):
    assert x.shape == (1, SEQ, D_MODEL) and ctx.shape == (1, CTX, D_MODEL)
    tb = _tables()
    xs = jnp.concatenate([x[0], ctx[0]], axis=0)

    cc = jnp.zeros((8, D_MODEL), F32).at[0].set(c[0]).at[1].set(c_ctx)
    mods = _modulation(cc, w_mod, b_mod)

    wr_t = _perm_experts(w_router).T
    wrh = wr_t.astype(BF16)
    wrl = (wr_t - wrh.astype(F32)).astype(BF16)
    br = jnp.broadcast_to(_perm_experts(b_router)[:, None], (N_EXPERTS, TM)).astype(F32)

    for l in range(DEPTH):
        last = l == DEPTH - 1
        modt = mods[l, 0:2].reshape(2, 6, 1, D_MODEL)
        row = lambda a: a[l].reshape(1, -1)
        u, q, k, v = _project(xs, row(g_norm1), modt, w_in[l].astype(BF16), row(g_q), row(g_k), tb)
        an = _attention(q, k, v, sink[l], row(g_branch_a))
        fn = _fourier(u, w_fourier[l].astype(BF16), row(g_branch_f), tb)
        xmid, h2, info, wcol, cnt = _merge(
            xs, fn, an, w_out[l].astype(BF16), modt, row(g_norm2),
            w_sh_gate[l].astype(BF16), w_sh_up[l].astype(BF16), w_sh_down[l].astype(BF16),
            wrh, wrl, br, tb)

        xs = _routed(l, info, cnt, h2, xmid, wcol, modt, w_exp_gate, w_exp_up, w_exp_down, tb,
                     NT_LAT if last else NT)
    return xs[None]


def _routed(l, info, cnt, h2, xmid, wcol, modt, w_g, w_u, w_d, tb, n_tiles):
    counts = cnt[:, 0].astype(I32).reshape(EPG, N_GROUPS).T.reshape(N_EXPERTS)
    padded = ((counts + BM - 1) // BM) * BM
    pad_end = jnp.cumsum(padded).astype(I32)
    pad_start = pad_end - padded
    start = jnp.sum(jnp.where(info[0:2, :, None] == jnp.arange(N_EXPERTS, dtype=I32),
                              pad_start, 0), axis=-1)
    dest = (start + info[2:4]).T.reshape(2 * T)
    ids = jnp.arange(N_EXPERTS, dtype=I32)
    block_row0 = jnp.arange(NB, dtype=I32)[:, None] * BM
    block_e = jnp.minimum(jnp.sum((pad_end <= block_row0).astype(I32), axis=-1), N_EXPERTS - 1)
    used = padded > 0
    ordinal = jnp.cumsum(used.astype(I32)) - 1
    eo = jnp.sum(jnp.where(block_e[:, None] == ids, ordinal, 0), axis=-1)
    es = jnp.sum(jnp.where(jnp.logical_and(used, ordinal == ids[:, None]), ids, 0), axis=-1)
    nn = jnp.stack([pad_end[N_EXPERTS - 1] // BM, ordinal[N_EXPERTS - 1] + 1]).astype(I32)
    xb = _dispatch(dest, pad_end, padded, h2)
    yb = _experts(l, eo.astype(I32), es.astype(I32), nn, xb, tb["perm"], w_g, w_u, w_d)
    return _combine(dest, yb, xmid, wcol, modt, tb["perm"], n_tiles)
```

```python
import functools

import numpy as np
import jax
import jax.numpy as jnp
from jax import lax
from jax.experimental import pallas as pl
from jax.experimental.pallas import tpu as pltpu

F32 = jnp.float32
BF16 = jnp.bfloat16
I32 = jnp.int32

D_MODEL = 2048
SEQ = 8192
CTX = 256
T = SEQ + CTX
DEPTH = 4
GRID_W = 64
HEAD_DIM = 128
D_F = 512
N_FG = 4
N_HEADS = 12
N_KV = 4
Q_PER_KV = 3
D_ATTN = N_HEADS * HEAD_DIM
D_KV = N_KV * HEAD_DIM
D_IN = D_F + D_ATTN + 2 * D_KV
ROPE_BASE = 10000.0
N_EXPERTS = 32
N_GROUPS = 8
EPG = 4
D_EXPERT = 512
D_SHARED = 512
EPS = 1e-6
NEG = -1e30

TM = 256
NT = T // TM
NT_LAT = SEQ // TM
QB = 128
NQB = T // QB
NQB_LAT = SEQ // QB
BM = 256
NB = (2 * T + BM - 1) // BM + N_EXPERTS
ROWS = NB * BM
N1 = 128
N2 = 64
SUB = 8
RL = D_MODEL // SUB
PK = D_MODEL // 2 // SUB
VMEM_LIMIT = 56 * 1024 * 1024


def _cp(sem):
    return pltpu.CompilerParams(dimension_semantics=sem, vmem_limit_bytes=VMEM_LIMIT)


def _silu(a):
    return a * jax.nn.sigmoid(a)


@functools.lru_cache(maxsize=None)
def _tables():
    t = np.arange(SEQ)
    row = (t // GRID_W).astype(np.float64)
    col = (t % GRID_W).astype(np.float64)
    inv = ROPE_BASE ** (-np.arange(0, HEAD_DIM // 2, 2, dtype=np.float64) / (HEAD_DIM // 2))
    ar = row[:, None] * inv
    ac = col[:, None] * inv
    ang = np.concatenate([ar, ar, ac, ac], axis=-1)
    cos = np.concatenate([np.cos(ang), np.ones((CTX, HEAD_DIM))], axis=0)
    sin = np.concatenate([np.sin(ang), np.zeros((CTX, HEAD_DIM))], axis=0)
    first = (np.arange(HEAD_DIM) % 64) < 32
    rope_a = np.where(first[None, :], -sin, 0.0)
    rope_b = np.where(first[None, :], 0.0, sin)

    def cs(n):
        k = np.arange(n)
        a = 2.0 * np.pi * np.outer(k, k) / n
        return np.cos(a), np.sin(a)

    c64, s64 = cs(N2)
    c128, s128 = cs(N1)
    c256, s256 = cs(CTX)
    m1 = np.concatenate([c64, -s64], axis=0)
    m2 = np.concatenate([c128, s128], axis=0)
    mc = np.concatenate([c256, s256], axis=0)
    csch = np.concatenate([c128, s128], axis=0)
    a = 2.0 * np.pi * np.outer(np.arange(N1), np.arange(N2)) / SEQ
    twr = np.broadcast_to(np.cos(a)[:, :, None], (N1, N2, 128))
    twi = np.broadcast_to(-np.sin(a)[:, :, None], (N1, N2, 128))
    tri = (np.arange(TM)[:, None] < np.arange(TM)[None, :]).astype(np.float32)
    perm = np.zeros((GRP * SUB, GRP * SUB), np.float32)
    rr, ss = np.meshgrid(np.arange(GRP), np.arange(SUB), indexing="ij")
    perm[(ss * GRP + rr).ravel(), (rr * SUB + ss).ravel()] = 1.0
    return dict(
        cos=jnp.asarray(cos, F32), rope_a=jnp.asarray(rope_a, F32), rope_b=jnp.asarray(rope_b, F32),
        m1=jnp.asarray(m1, BF16), m2=jnp.asarray(m2, BF16), mc=jnp.asarray(mc, BF16),
        csch=jnp.asarray(csch, BF16), twr=jnp.asarray(twr, F32), twi=jnp.asarray(twi, F32),
        tri=jnp.asarray(tri, BF16), perm=jnp.asarray(perm, BF16))


MOD_TN = 1024


def _mod_kernel(cc_ref, w_ref, b_ref, o_ref):
    a = _silu(cc_ref[...])
    o_ref[...] = jnp.dot(a.astype(BF16), w_ref[...].astype(BF16),
                         preferred_element_type=F32) + b_ref[...]


def _modulation(cc, w_mod, b_mod):
    nl = w_mod.shape[0]
    n6 = w_mod.shape[2]
    return pl.pallas_call(
        _mod_kernel,
        grid=(nl, n6 // MOD_TN),
        in_specs=[pl.BlockSpec((8, D_MODEL), lambda l, j: (0, 0)),
                  pl.BlockSpec((None, D_MODEL, MOD_TN), lambda l, j: (l, 0, j)),
                  pl.BlockSpec((None, 1, MOD_TN), lambda l, j: (l, 0, j))],
        out_specs=pl.BlockSpec((None, 8, MOD_TN), lambda l, j: (l, 0, j)),
        out_shape=jax.ShapeDtypeStruct((nl, 8, n6), F32),
        compiler_params=_cp(("arbitrary", "arbitrary")),
        name="modulation",
    )(cc, w_mod, b_mod.reshape(nl, 1, n6))


def _mod_spec(chunk):
    return pl.BlockSpec((None, None, 1, D_MODEL), lambda i: (i // NT_LAT, chunk, 0, 0))


def _row_spec(n):
    return pl.BlockSpec((1, n), lambda i: (0, 0))


PCW = 256


def _proj_kernel(x_ref, g_ref, sc_ref, sh_ref, w_ref, gq_ref, gk_ref, cos_ref, ra_ref, rb_ref,
                 u_ref, q_ref, k_ref, v_ref, hb_ref):
    x = x_ref[...]
    ms = jnp.mean(x * x, axis=-1, keepdims=True)
    h = x * lax.rsqrt(ms + EPS) * g_ref[...]
    h = h * (1.0 + sc_ref[...]) + sh_ref[...]
    hb_ref[...] = h.astype(BF16)
    cos = cos_ref[...]
    ra = ra_ref[...]
    rb = rb_ref[...]

    def head(t, g):
        m = jnp.mean(t * t, axis=-1, keepdims=True)
        t = t * lax.rsqrt(m + EPS) * g
        return t * cos + pltpu.roll(t, 96, 1) * ra + pltpu.roll(t, 32, 1) * rb

    scale = HEAD_DIM ** -0.5
    for c in range(D_IN // PCW):
        col = c * PCW
        p = jnp.dot(hb_ref[...], w_ref[:, col:col + PCW], preferred_element_type=F32)
        if col < D_F:
            u_ref[:, col:col + PCW] = p.astype(BF16)
        elif col < D_F + D_ATTN:
            o = col - D_F
            for j in range(PCW // HEAD_DIM):
                t = head(p[:, j * HEAD_DIM:(j + 1) * HEAD_DIM], gq_ref[...]) * scale
                q_ref[:, o + j * HEAD_DIM:o + (j + 1) * HEAD_DIM] = t.astype(BF16)
        elif col < D_F + D_ATTN + D_KV:
            o = col - D_F - D_ATTN
            for j in range(PCW // HEAD_DIM):
                t = head(p[:, j * HEAD_DIM:(j + 1) * HEAD_DIM], gk_ref[...])
                k_ref[:, o + j * HEAD_DIM:o + (j + 1) * HEAD_DIM] = t.astype(BF16)
        else:
            o = col - D_F - D_ATTN - D_KV
            v_ref[:, o:o + PCW] = p.astype(BF16)


def _project(xs, g1, modt, w_in_bf, gq, gk, tb):
    tile = lambda n: pl.BlockSpec((TM, n), lambda i: (i, 0))
    return pl.pallas_call(
        _proj_kernel,
        grid=(NT,),
        in_specs=[tile(D_MODEL), _row_spec(D_MODEL), _mod_spec(1), _mod_spec(0),
                  pl.BlockSpec((D_MODEL, D_IN), lambda i: (0, 0)),
                  _row_spec(HEAD_DIM), _row_spec(HEAD_DIM),
                  tile(HEAD_DIM), tile(HEAD_DIM), tile(HEAD_DIM)],
        out_specs=[tile(D_F), tile(D_ATTN), tile(D_KV), tile(D_KV)],
        out_shape=[jax.ShapeDtypeStruct((T, D_F), BF16), jax.ShapeDtypeStruct((T, D_ATTN), BF16),
                   jax.ShapeDtypeStruct((T, D_KV), BF16), jax.ShapeDtypeStruct((T, D_KV), BF16)],
        scratch_shapes=[pltpu.VMEM((TM, D_MODEL), BF16)],
        compiler_params=_cp(("arbitrary",)),
        name="norm_proj",
    )(xs, g1, modt, modt, w_in_bf, gq, gk, tb["cos"], tb["rope_a"], tb["rope_b"])


def _attn_kernel(sink_ref, q_ref, kp_ref, kc_ref, kn_ref, vp_ref, vc_ref, vn_ref, kx_ref, vx_ref,
                 ga_ref, o_ref, a_scr):
    b = pl.program_id(0)
    lat = b < NQB_LAT
    f_prev = jnp.logical_and(lat, b > 0).astype(I32)
    f_cur = lat.astype(I32)
    f_next = (b < NQB_LAT - 1).astype(I32)
    n3 = Q_PER_KV * QB
    r = lax.broadcasted_iota(I32, (n3, 3 * QB), 0) & (QB - 1)
    cidx = lax.broadcasted_iota(I32, (n3, 3 * QB), 1)
    blk = cidx >> 7
    j = cidx & (QB - 1)
    ok = jnp.where(blk == 0, (j >= r).astype(I32) * f_prev,
                   jnp.where(blk == 1, f_cur, (j <= r).astype(I32) * f_next))
    valid = ok > 0
    dn = (((1,), (1,)), ((), ()))
    for hk in range(N_KV):
        ks = slice(hk * HEAD_DIM, (hk + 1) * HEAD_DIM)
        qs = jnp.concatenate(
            [q_ref[:, (hk * Q_PER_KV + g) * HEAD_DIM:(hk * Q_PER_KV + g + 1) * HEAD_DIM]
             for g in range(Q_PER_KV)], axis=0)
        kw = jnp.concatenate([kp_ref[:, ks], kc_ref[:, ks], kn_ref[:, ks]], axis=0)
        vw = jnp.concatenate([vp_ref[:, ks], vc_ref[:, ks], vn_ref[:, ks]], axis=0)
        s = lax.dot_general(qs, kw, dn, preferred_element_type=F32)
        s = jnp.where(valid, s, NEG)
        sx = lax.dot_general(qs, kx_ref[:, ks], dn, preferred_element_type=F32)
        sk = jnp.concatenate(
            [jnp.full((QB, 1), sink_ref[hk * Q_PER_KV + g], F32) for g in range(Q_PER_KV)], axis=0)
        m = jnp.maximum(jnp.maximum(jnp.max(s, axis=-1, keepdims=True),
                                    jnp.max(sx, axis=-1, keepdims=True)), sk)
        e = jnp.exp(s - m)
        ex = jnp.exp(sx - m)
        den = (jnp.sum(e, axis=-1, keepdims=True) + jnp.sum(ex, axis=-1, keepdims=True)
               + jnp.exp(sk - m))
        o = (jnp.dot(e.astype(BF16), vw, preferred_element_type=F32)
             + jnp.dot(ex.astype(BF16), vx_ref[:, ks], preferred_element_type=F32)) / den
        for g in range(Q_PER_KV):
            h = hk * Q_PER_KV + g
            a_scr[:, h * HEAD_DIM:(h + 1) * HEAD_DIM] = o[g * QB:(g + 1) * QB]
    a = a_scr[...]
    ms = jnp.mean(a * a, axis=-1, keepdims=True)
    o_ref[...] = (a * lax.rsqrt(ms + EPS) * ga_ref[...]).astype(BF16)


def _attention(q, k, v, sink, g_a):
    kv = lambda f: pl.BlockSpec((QB, D_KV), f)
    prev = lambda b: (jnp.maximum(b - 1, 0), 0)
    cur = lambda b: (b, 0)
    nxt = lambda b: (jnp.minimum(b + 1, NQB - 1), 0)
    ctx = pl.BlockSpec((CTX, D_KV), lambda b: (SEQ // CTX, 0))
    return pl.pallas_call(
        _attn_kernel,
        grid=(NQB,),
        in_specs=[pl.BlockSpec(memory_space=pltpu.SMEM),
                  pl.BlockSpec((QB, D_ATTN), cur),
                  kv(prev), kv(cur), kv(nxt), kv(prev), kv(cur), kv(nxt), ctx, ctx,
                  _row_spec(D_ATTN)],
        out_specs=pl.BlockSpec((QB, D_ATTN), cur),
        out_shape=jax.ShapeDtypeStruct((T, D_ATTN), BF16),
        scratch_shapes=[pltpu.VMEM((QB, D_ATTN), F32)],
        compiler_params=_cp(("arbitrary",)),
        name="attention",
    )(sink, q, k, k, k, v, v, v, k, v, g_a)


F1_J = 8
F2_K = 4


def _f1_kernel(u_ref, m1_ref, twr_ref, twi_ref, z_ref):
    g = jnp.dot(m1_ref[...], u_ref[...], preferred_element_type=F32)
    for jl in range(F1_J):
        twr = twr_ref[jl]
        twi = twi_ref[jl]
        for lt in range(D_F // 128):
            c0 = jl * D_F + lt * 128
            gr = g[0:N2, c0:c0 + 128]
            gi = g[N2:2 * N2, c0:c0 + 128]
            z_ref[jl, 0:N2, lt * 128:(lt + 1) * 128] = (gr * twr - gi * twi).astype(BF16)
            z_ref[jl, N2:2 * N2, lt * 128:(lt + 1) * 128] = (gr * twi + gi * twr).astype(BF16)


def _f2_kernel(zr_ref, zi_ref, m2_ref, y_ref):
    pr = jnp.dot(m2_ref[...], zr_ref[...], preferred_element_type=F32)
    pi = jnp.dot(m2_ref[...], zi_ref[...], preferred_element_type=F32)
    yr = pr[0:N1] + pi[N1:2 * N1]
    yi = pi[0:N1] - pr[N1:2 * N1]
    for kl in range(F2_K):
        y_ref[:, kl * 2 * D_F:kl * 2 * D_F + D_F] = yr[:, kl * D_F:(kl + 1) * D_F].astype(BF16)
        y_ref[:, kl * 2 * D_F + D_F:(kl + 1) * 2 * D_F] = yi[:, kl * D_F:(kl + 1) * D_F].astype(BF16)


def _f3_kernel(y_ref, uc_ref, cs_ref, mc_ref, wf_ref, gf_ref, o_ref):
    i = pl.program_id(0)

    def finish(parts, scale):
        f = jnp.concatenate(parts, axis=1) * scale
        t = jnp.dot(f.astype(BF16), wf_ref[...], preferred_element_type=F32)
        ms = jnp.mean(t * t, axis=-1, keepdims=True)
        o_ref[...] = (t * lax.rsqrt(ms + EPS) * gf_ref[...]).astype(BF16)

    @pl.when(i < NT_LAT)
    def _():
        parts = []
        for g in range(N_FG):
            lhs = jnp.concatenate([y_ref[:, g * 128:(g + 1) * 128],
                                   y_ref[:, D_F + g * 128:D_F + (g + 1) * 128]], axis=1)
            parts.append(jnp.dot(lhs, cs_ref[...], preferred_element_type=F32))
        finish(parts, float((SEQ * 128) ** -0.5))

    @pl.when(i == NT_LAT)
    def _():
        pq = jnp.dot(mc_ref[...], uc_ref[...], preferred_element_type=F32)
        parts = []
        for g in range(N_FG):
            lhs = jnp.concatenate([pq[0:CTX, g * 128:(g + 1) * 128],
                                   -pq[CTX:2 * CTX, g * 128:(g + 1) * 128]], axis=1)
            parts.append(jnp.dot(lhs.astype(BF16), cs_ref[...], preferred_element_type=F32))
        finish(parts, float((CTX * 128) ** -0.5))


def _fourier(u, w_f_bf, g_f, tb):
    u2 = u.reshape(T // N1, N1 * D_F)
    z = pl.pallas_call(
        _f1_kernel,
        grid=(N1 // F1_J,),
        in_specs=[pl.BlockSpec((N2, F1_J * D_F), lambda c: (0, c)),
                  pl.BlockSpec((2 * N2, N2), lambda c: (0, 0)),
                  pl.BlockSpec((F1_J, N2, 128), lambda c: (c, 0, 0)),
                  pl.BlockSpec((F1_J, N2, 128), lambda c: (c, 0, 0))],
        out_specs=pl.BlockSpec((F1_J, 2 * N2, D_F), lambda c: (c, 0, 0)),
        out_shape=jax.ShapeDtypeStruct((N1, 2 * N2, D_F), BF16),
        compiler_params=_cp(("arbitrary",)),
        name="fourier_s1",
    )(u2, tb["m1"], tb["twr"], tb["twi"])
    z2 = z.reshape(N1, 2 * N2 * D_F)
    nk = N2 // F2_K
    y = pl.pallas_call(
        _f2_kernel,
        grid=(nk,),
        in_specs=[pl.BlockSpec((N1, F2_K * D_F), lambda k: (0, k)),
                  pl.BlockSpec((N1, F2_K * D_F), lambda k: (0, nk + k)),
                  pl.BlockSpec((2 * N1, N1), lambda k: (0, 0))],
        out_specs=pl.BlockSpec((N1, F2_K * 2 * D_F), lambda k: (0, k)),
        out_shape=jax.ShapeDtypeStruct((N1, N2 * 2 * D_F), BF16),
        compiler_params=_cp(("arbitrary",)),
        name="fourier_s2",
    )(z2, z2, tb["m2"])
    y2 = y.reshape(SEQ, 2 * D_F)
    return pl.pallas_call(
        _f3_kernel,
        grid=(NT,),
        in_specs=[pl.BlockSpec((TM, 2 * D_F), lambda i: (jnp.minimum(i, NT_LAT - 1), 0)),
                  pl.BlockSpec((CTX, D_F), lambda i: (SEQ // CTX, 0)),
                  pl.BlockSpec((2 * 128, 128), lambda i: (0, 0)),
                  pl.BlockSpec((2 * CTX, CTX), lambda i: (0, 0)),
                  pl.BlockSpec((D_F, D_F), lambda i: (0, 0)),
                  _row_spec(D_F)],
        out_specs=pl.BlockSpec((TM, D_F), lambda i: (i, 0)),
        out_shape=jax.ShapeDtypeStruct((T, D_F), BF16),
        compiler_params=_cp(("arbitrary",)),
        name="fourier_s3",
    )(y2, u, tb["csch"], tb["mc"], w_f_bf, g_f)


def _merge_kernel(x_ref, fn_ref, an_ref, wo_ref, ga1_ref, g2_ref, sc2_ref, sh2_ref, ga2_ref,
                  wsg_ref, wsu_ref, wsd_ref, wrh_ref, wrl_ref, br_ref, tri_ref,
                  xmid_ref, h2_ref, info_ref, wcol_ref, cnt_ref, carry_scr):
    i = pl.program_id(0)

    @pl.when(i == 0)
    def _():
        carry_scr[...] = jnp.zeros_like(carry_scr)

    m = (jnp.dot(fn_ref[...], wo_ref[0:D_F, :], preferred_element_type=F32)
         + jnp.dot(an_ref[...], wo_ref[D_F:D_MODEL, :], preferred_element_type=F32))
    x1 = x_ref[...] + ga1_ref[...] * m
    ms = jnp.mean(x1 * x1, axis=-1, keepdims=True)
    h = x1 * lax.rsqrt(ms + EPS) * g2_ref[...]
    h = h * (1.0 + sc2_ref[...]) + sh2_ref[...]
    hb = h.astype(BF16)
    hp = _pack_bf16_pairs(h)
    for s in range(SUB):
        h2_ref[:, s, :] = hp[:, s * PK:(s + 1) * PK]

    a = jnp.dot(hb, wsg_ref[...], preferred_element_type=F32)
    u = jnp.dot(hb, wsu_ref[...], preferred_element_type=F32)
    act = (_silu(a) * u).astype(BF16)
    ys = jnp.dot(act, wsd_ref[...], preferred_element_type=F32)
    xmid_ref[...] = x1 + ga2_ref[...] * ys

    hl = (h - hb.astype(F32)).astype(BF16)
    dn = (((1,), (1,)), ((), ()))
    lg = (lax.dot_general(wrh_ref[...], hb, dn, preferred_element_type=F32)
          + (lax.dot_general(wrl_ref[...], hb, dn, preferred_element_type=F32)
             + lax.dot_general(wrh_ref[...], hl, dn, preferred_element_type=F32)))
    score = jax.nn.sigmoid(lg)
    sel = score + br_ref[...]
    s = [sel[jj * N_GROUPS:(jj + 1) * N_GROUPS] for jj in range(EPG)]
    sr = [score[jj * N_GROUPS:(jj + 1) * N_GROUPS] for jj in range(EPG)]
    hi01, lo01 = jnp.maximum(s[0], s[1]), jnp.minimum(s[0], s[1])
    hi23, lo23 = jnp.maximum(s[2], s[3]), jnp.minimum(s[2], s[3])
    gscore = jnp.maximum(hi01, hi23) + jnp.maximum(jnp.minimum(hi01, hi23), jnp.maximum(lo01, lo23))
    gi = lax.broadcasted_iota(I32, (N_GROUPS, TM), 0)
    gmax = jnp.max(gscore, axis=0, keepdims=True)
    g_idx = jnp.min(jnp.where(gscore == gmax, gi, N_GROUPS), axis=0, keepdims=True)
    gsel = gi == g_idx
    v = [jnp.sum(jnp.where(gsel, s[jj], 0.0), axis=0, keepdims=True) for jj in range(EPG)]
    vr = [jnp.sum(jnp.where(gsel, sr[jj], 0.0), axis=0, keepdims=True) for jj in range(EPG)]

    def first_argmax(vals):
        best = vals[0]
        idx = jnp.zeros((1, TM), I32)
        for jj in range(1, EPG):
            upd = vals[jj] > best
            best = jnp.where(upd, vals[jj], best)
            idx = jnp.where(upd, jj, idx)
        return idx

    i1 = first_argmax(v)
    i2 = first_argmax([jnp.where(i1 == jj, -jnp.inf, v[jj]) for jj in range(EPG)])

    def pick(vals, idx):
        out = vals[0]
        for jj in range(1, EPG):
            out = jnp.where(idx == jj, vals[jj], out)
        return out

    w1 = pick(vr, i1)
    w2 = pick(vr, i2)
    wsum = w1 + w2
    w1 = w1 / wsum
    w2 = w2 / wsum

    ri = lax.broadcasted_iota(I32, (N_EXPERTS, TM), 0)
    oh1 = ri == i1 * N_GROUPS + g_idx
    oh2 = ri == i2 * N_GROUPS + g_idx
    oh = jnp.logical_or(oh1, oh2).astype(F32)
    pre = jnp.dot(oh.astype(BF16), tri_ref[...], preferred_element_type=F32) + carry_scr[:, 0:1]
    r1 = jnp.sum(jnp.where(oh1, pre, 0.0), axis=0, keepdims=True)
    r2 = jnp.sum(jnp.where(oh2, pre, 0.0), axis=0, keepdims=True)
    carry = carry_scr[...] + jnp.sum(oh, axis=1, keepdims=True)
    carry_scr[...] = carry
    cnt_ref[...] = carry

    zi = jnp.zeros((4, TM), I32)
    info_ref[...] = jnp.concatenate(
        [g_idx * EPG + i1, g_idx * EPG + i2, r1.astype(I32), r2.astype(I32), zi], axis=0)
    wrow = jnp.concatenate([w1, w2, jnp.zeros((126, TM), F32)], axis=0)
    wcol_ref[...] = wrow.T


def _merge(xs, fn, an, w_out_bf, modt, g2, wsg, wsu, wsd, wrh, wrl, br, tb):
    tile = lambda n: pl.BlockSpec((TM, n), lambda i: (i, 0))
    full = lambda a, b: pl.BlockSpec((a, b), lambda i: (0, 0))
    return pl.pallas_call(
        _merge_kernel,
        grid=(NT,),
        in_specs=[tile(D_MODEL), tile(D_F), tile(D_ATTN), full(D_MODEL, D_MODEL),
                  _mod_spec(2), _row_spec(D_MODEL), _mod_spec(4), _mod_spec(3), _mod_spec(5),
                  full(D_MODEL, D_SHARED), full(D_MODEL, D_SHARED), full(D_SHARED, D_MODEL),
                  full(N_EXPERTS, D_MODEL), full(N_EXPERTS, D_MODEL), full(N_EXPERTS, TM),
                  full(TM, TM)],
        out_specs=[tile(D_MODEL), pl.BlockSpec((TM, SUB, PK), lambda i: (i, 0, 0)),
                   pl.BlockSpec((8, TM), lambda i: (0, i)),
                   pl.BlockSpec((TM, 128), lambda i: (i, 0)),
                   pl.BlockSpec((N_EXPERTS, 128), lambda i: (0, 0))],
        out_shape=[jax.ShapeDtypeStruct((T, D_MODEL), F32), jax.ShapeDtypeStruct((T, SUB, PK), I32),
                   jax.ShapeDtypeStruct((8, T), I32), jax.ShapeDtypeStruct((T, 128), F32),
                   jax.ShapeDtypeStruct((N_EXPERTS, 128), F32)],
        scratch_shapes=[pltpu.VMEM((N_EXPERTS, 128), F32)],
        compiler_params=_cp(("arbitrary",)),
        name="merge_route",
    )(xs, fn, an, w_out_bf, modt, g2, modt, modt, modt, wsg, wsu, wsd, wrh, wrl, br, tb["tri"])


def _dispatch_kernel(dest_ref, pend_ref, padded_ref, cnt_ref, h_hbm, xb_ref, hv, row_tok, sem):
    b = pl.program_id(0)

    @pl.when(b == 0)
    def _():
        load = pltpu.make_async_copy(h_hbm, hv.at[pl.ds(0, T)], sem)
        load.start()
        hv[T] = jnp.zeros((SUB, PK), I32)
        for blk in range(NB - N_EXPERTS, NB):
            @pl.when(blk * BM >= pend_ref[N_EXPERTS - 1])
            def _():
                def fill(r, c):
                    row_tok[blk * BM + r] = T
                    return c
                lax.fori_loop(0, BM, fill, 0)
        for e in range(N_EXPERTS):
            def pad(r, c):
                row_tok[r] = T
                return c
            lax.fori_loop(pend_ref[e] - padded_ref[e] + cnt_ref[e], pend_ref[e], pad, 0)

        def place(a, c):
            row_tok[dest_ref[a]] = a >> 1
            return c
        lax.fori_loop(0, 2 * T, place, 0, unroll=8)
        load.wait()

    def gather(r, c):
        xb_ref[r] = hv[row_tok[b * BM + r]]
        return c
    lax.fori_loop(0, BM, gather, 0, unroll=8)


def _dispatch(dest, pad_end, padded, counts, h2):
    return pl.pallas_call(
        _dispatch_kernel,
        grid_spec=pltpu.PrefetchScalarGridSpec(
            num_scalar_prefetch=4, grid=(NB,),
            in_specs=[pl.BlockSpec(memory_space=pl.ANY)],
            out_specs=pl.BlockSpec((BM, SUB, PK), lambda b, d, pe, pd, ct: (b, 0, 0)),
            scratch_shapes=[pltpu.VMEM((T + 1, SUB, PK), I32), pltpu.SMEM((ROWS,), I32),
                            pltpu.SemaphoreType.DMA(())]),
        out_shape=jax.ShapeDtypeStruct((ROWS, SUB, PK), I32),
        compiler_params=_cp(("arbitrary",)),
        name="dispatch",
    )(dest, pad_end, padded, counts, h2)


def _pack_bf16_pairs(h):
    bits = lax.bitcast_convert_type(h.astype(BF16).astype(F32), I32)
    half = h.shape[1] // 2
    return (bits[:, :half] & -65536) | lax.shift_right_logical(bits[:, half:], 16)


GRP = 32


def _expert_kernel(eo_ref, es_ref, n_ref, x_ref, perm_ref, wg_hbm, wu_hbm, wd_hbm, y_ref,
                   wgf, wuf, wdf, wgb, wub, wdb, xs_scr, wsem, *, layer):
    b = pl.program_id(0)
    k = eo_ref[b]
    first = jnp.logical_or(b == 0, k != eo_ref[jnp.maximum(b - 1, 0)])

    def copies(kk, slot):
        e = es_ref[kk]
        return (pltpu.make_async_copy(wg_hbm.at[layer, e], wgf.at[slot], wsem.at[slot, 0]),
                pltpu.make_async_copy(wu_hbm.at[layer, e], wuf.at[slot], wsem.at[slot, 1]),
                pltpu.make_async_copy(wd_hbm.at[layer, e], wdf.at[slot], wsem.at[slot, 2]))

    @pl.when(b == 0)
    def _():
        for c in copies(0, 0):
            c.start()

    @pl.when(jnp.logical_and(first, b < n_ref[0]))
    def _():
        slot = k % 2
        for c in copies(k, slot):
            c.wait()

        @pl.when(k + 1 < n_ref[1])
        def _():
            for c in copies(k + 1, 1 - slot):
                c.start()

        wgb[...] = wgf[slot].astype(BF16)
        wub[...] = wuf[slot].astype(BF16)
        wdb[...] = wdf[slot].astype(BF16)

    @pl.when(b < n_ref[0])
    def _():
        half = D_MODEL // 2
        for grp in range(BM // GRP):
            w = x_ref[grp * GRP:(grp + 1) * GRP].reshape(GRP * SUB, PK)
            hi = lax.bitcast_convert_type(w & -65536, F32).astype(BF16)
            lo = lax.bitcast_convert_type(w << 16, F32).astype(BF16)
            g = jnp.concatenate([hi, lo], axis=1)
            r = jnp.dot(perm_ref[...], g, preferred_element_type=F32).astype(BF16)
            rows = slice(grp * GRP, (grp + 1) * GRP)
            for s in range(SUB):
                xs_scr[rows, s * PK:(s + 1) * PK] = r[s * GRP:(s + 1) * GRP, 0:PK]
                xs_scr[rows, half + s * PK:half + (s + 1) * PK] = r[s * GRP:(s + 1) * GRP, PK:2 * PK]
        x = xs_scr[...]
        a = jnp.dot(x, wgb[...], preferred_element_type=F32)
        u = jnp.dot(x, wub[...], preferred_element_type=F32)
        act = (_silu(a) * u).astype(BF16)
        yp = _pack_bf16_pairs(jnp.dot(act, wdb[...], preferred_element_type=F32))
        for s in range(SUB):
            y_ref[:, s, :] = yp[:, s * PK:(s + 1) * PK]

    @pl.when(b >= n_ref[0])
    def _():
        y_ref[...] = jnp.zeros_like(y_ref)


def _experts(l, eo, es, nn, xb, perm, w_g, w_u, w_d):
    hbm = pl.BlockSpec(memory_space=pl.ANY)
    return pl.pallas_call(
        functools.partial(_expert_kernel, layer=l),
        grid_spec=pltpu.PrefetchScalarGridSpec(
            num_scalar_prefetch=3, grid=(NB,),
            in_specs=[pl.BlockSpec((BM, SUB, PK), lambda b, eo, es, n: (jnp.minimum(b, n[0] - 1), 0, 0)),
                      pl.BlockSpec((GRP * SUB, GRP * SUB), lambda b, eo, es, n: (0, 0)),
                      hbm, hbm, hbm],
            out_specs=pl.BlockSpec((BM, SUB, PK), lambda b, eo, es, n: (b, 0, 0)),
            scratch_shapes=[pltpu.VMEM((2, D_MODEL, D_EXPERT), F32),
                            pltpu.VMEM((2, D_MODEL, D_EXPERT), F32),
                            pltpu.VMEM((2, D_EXPERT, D_MODEL), F32),
                            pltpu.VMEM((D_MODEL, D_EXPERT), BF16),
                            pltpu.VMEM((D_MODEL, D_EXPERT), BF16),
                            pltpu.VMEM((D_EXPERT, D_MODEL), BF16),
                            pltpu.VMEM((BM, D_MODEL), BF16),
                            pltpu.SemaphoreType.DMA((2, 3))]),
        out_shape=jax.ShapeDtypeStruct((ROWS, SUB, PK), I32),
        compiler_params=_cp(("arbitrary",)),
        name="experts",
    )(eo, es, nn, xb, perm, w_g, w_u, w_d)


def _combine_kernel(dest_ref, y_hbm, xmid_ref, wcol_ref, ga2_ref, perm_ref, o_ref, b1, b2, sem, *, n):
    i = pl.program_id(0)

    def copy(d, buf, slot, r):
        return pltpu.make_async_copy(y_hbm.at[pl.ds(d, 1)], buf.at[slot, pl.ds(r, 1)], sem.at[slot])

    def issue(tile, slot):
        def body(r, c):
            t = tile * TM + r
            copy(dest_ref[2 * t], b1, slot, r).start()
            copy(dest_ref[2 * t + 1], b2, slot, r).start(priority=1)
            return c
        lax.fori_loop(0, TM, body, 0)

    @pl.when(i == 0)
    def _():
        issue(0, 0)

    @pl.when(i + 1 < n)
    def _():
        issue(i + 1, (i + 1) % 2)

    slot = i % 2

    def drain(r, c):
        copy(0, b1, slot, 0).wait()
        copy(0, b2, slot, 0).wait()
        return c

    lax.fori_loop(0, TM, drain, 0)

    def unpack_group(buf, grp):
        w = buf[slot, grp * GRP:(grp + 1) * GRP].reshape(GRP * SUB, PK)
        hi = lax.bitcast_convert_type(w & -65536, F32).astype(BF16)
        lo = lax.bitcast_convert_type(w << 16, F32).astype(BF16)
        return jnp.dot(perm_ref[...], jnp.concatenate([hi, lo], axis=1), preferred_element_type=F32)

    half = D_MODEL // 2
    for grp in range(TM // GRP):
        rows = slice(grp * GRP, (grp + 1) * GRP)
        w1 = wcol_ref[rows, 0:1]
        w2 = wcol_ref[rows, 1:2]
        r1 = unpack_group(b1, grp)
        r2 = unpack_group(b2, grp)
        for s in range(SUB):
            sr = slice(s * GRP, (s + 1) * GRP)
            for c0, lanes in ((s * PK, slice(0, PK)), (half + s * PK, slice(PK, 2 * PK))):
                cs = slice(c0, c0 + PK)
                o_ref[rows, cs] = xmid_ref[rows, cs] + ga2_ref[:, cs] * (w1 * r1[sr, lanes] + w2 * r2[sr, lanes])


def _combine(dest, yb, xmid, wcol, modt, perm, n_tiles):
    tile = lambda n: pl.BlockSpec((TM, n), lambda i, d: (i, 0))
    return pl.pallas_call(
        functools.partial(_combine_kernel, n=n_tiles),
        grid_spec=pltpu.PrefetchScalarGridSpec(
            num_scalar_prefetch=1, grid=(n_tiles,),
            in_specs=[pl.BlockSpec(memory_space=pl.ANY), tile(D_MODEL), tile(128),
                      pl.BlockSpec((None, None, 1, D_MODEL), lambda i, d: (i // NT_LAT, 5, 0, 0)),
                      pl.BlockSpec((GRP * SUB, GRP * SUB), lambda i, d: (0, 0))],
            out_specs=tile(D_MODEL),
            scratch_shapes=[pltpu.VMEM((2, TM, SUB, PK), I32), pltpu.VMEM((2, TM, SUB, PK), I32),
                            pltpu.SemaphoreType.DMA((2,))]),
        out_shape=jax.ShapeDtypeStruct((n_tiles * TM, D_MODEL), F32),
        compiler_params=_cp(("arbitrary",)),
        name="combine",
    )(dest, yb, xmid, wcol, modt, perm)


def _perm_experts(a):
    return a.reshape(a.shape[:-1] + (N_GROUPS, EPG)).swapaxes(-1, -2).reshape(a.shape)


def kernel(x, c, ctx, c_ctx, w_mod, b_mod, g_norm1, g_norm2, w_in, w_fourier, g_q, g_k, sink,
           g_branch_f, g_branch_a, w_out, w_router, b_router, w_exp_gate, w_exp_up, w_exp_down,
           w_sh_gate, w_sh_up, w_sh_down):
    assert x.shape == (1, SEQ, D_MODEL) and ctx.shape == (1, CTX, D_MODEL)
    tb = _tables()
    xs = jnp.concatenate([x[0], ctx[0]], axis=0)

    cc = jnp.zeros((8, D_MODEL), F32).at[0].set(c[0]).at[1].set(c_ctx)
    mods = _modulation(cc, w_mod, b_mod)

    wr_t = _perm_experts(w_router).T
    wrh = wr_t.astype(BF16)
    wrl = (wr_t - wrh.astype(F32)).astype(BF16)
    br = jnp.broadcast_to(_perm_experts(b_router)[:, None], (N_EXPERTS, TM)).astype(F32)

    for l in range(DEPTH):
        last = l == DEPTH - 1
        modt = mods[l, 0:2].reshape(2, 6, 1, D_MODEL)
        row = lambda a: a[l].reshape(1, -1)
        u, q, k, v = _project(xs, row(g_norm1), modt, w_in[l].astype(BF16), row(g_q), row(g_k), tb)
        an = _attention(q, k, v, sink[l], row(g_branch_a))
        fn = _fourier(u, w_fourier[l].astype(BF16), row(g_branch_f), tb)
        xmid, h2, info, wcol, cnt = _merge(
            xs, fn, an, w_out[l].astype(BF16), modt, row(g_norm2),
            w_sh_gate[l].astype(BF16), w_sh_up[l].astype(BF16), w_sh_down[l].astype(BF16),
            wrh, wrl, br, tb)

        xs = _routed(l, info, cnt, h2, xmid, wcol, modt, w_exp_gate, w_exp_up, w_exp_down, tb,
                     NT_LAT if last else NT)
    return xs[None]


def _routed(l, info, cnt, h2, xmid, wcol, modt, w_g, w_u, w_d, tb, n_tiles):
    counts = cnt[:, 0].astype(I32).reshape(EPG, N_GROUPS).T.reshape(N_EXPERTS)
    padded = ((counts + BM - 1) // BM) * BM
    pad_end = jnp.cumsum(padded).astype(I32)
    pad_start = pad_end - padded
    start = jnp.sum(jnp.where(info[0:2, :, None] == jnp.arange(N_EXPERTS, dtype=I32),
                              pad_start, 0), axis=-1)
    dest = (start + info[2:4]).T.reshape(2 * T)
    ids = jnp.arange(N_EXPERTS, dtype=I32)
    block_row0 = jnp.arange(NB, dtype=I32)[:, None] * BM
    block_e = jnp.minimum(jnp.sum((pad_end <= block_row0).astype(I32), axis=-1), N_EXPERTS - 1)
    used = padded > 0
    ordinal = jnp.cumsum(used.astype(I32)) - 1
    eo = jnp.sum(jnp.where(block_e[:, None] == ids, ordinal, 0), axis=-1)
    es = jnp.sum(jnp.where(jnp.logical_and(used, ordinal == ids[:, None]), ids, 0), axis=-1)
    nn = jnp.stack([pad_end[N_EXPERTS - 1] // BM, ordinal[N_EXPERTS - 1] + 1]).astype(I32)
    xb = _dispatch(dest, pad_end, padded, counts, h2)
    yb = _experts(l, eo.astype(I32), es.astype(I32), nn, xb, tb["perm"], w_g, w_u, w_d)
    return _combine(dest, yb, xmid, wcol, modt, tb["perm"], n_tiles)
```

```python
import functools

import numpy as np
import jax
import jax.numpy as jnp
from jax import lax
from jax.experimental import pallas as pl
from jax.experimental.pallas import tpu as pltpu

F32 = jnp.float32
BF16 = jnp.bfloat16
I32 = jnp.int32

D_MODEL = 2048
SEQ = 8192
CTX = 256
T = SEQ + CTX
DEPTH = 4
GRID_W = 64
HEAD_DIM = 128
D_F = 512
N_FG = 4
N_HEADS = 12
N_KV = 4
Q_PER_KV = 3
D_ATTN = N_HEADS * HEAD_DIM
D_KV = N_KV * HEAD_DIM
D_IN = D_F + D_ATTN + 2 * D_KV
ROPE_BASE = 10000.0
N_EXPERTS = 32
N_GROUPS = 8
EPG = 4
D_EXPERT = 512
D_SHARED = 512
EPS = 1e-6
NEG = -1e30
LOG2E = 1.4426950408889634
WINDOW = 128

TM = 256
NT = T // TM
NT_LAT = SEQ // TM
QB = 128
NQB = T // QB
NQB_LAT = SEQ // QB
BM = 256
NB = (2 * T + BM - 1) // BM + N_EXPERTS
ROWS = NB * BM
N1 = 128
N2 = 64
SUB = 8
RL = D_MODEL // SUB
PK = D_MODEL // 2 // SUB
VMEM_LIMIT = 56 * 1024 * 1024


def _cp(sem):
    return pltpu.CompilerParams(dimension_semantics=sem, vmem_limit_bytes=VMEM_LIMIT)


def _silu(a):
    return a * jax.nn.sigmoid(a)


@functools.lru_cache(maxsize=None)
def _tables():
    t = np.arange(SEQ)
    row = (t // GRID_W).astype(np.float64)
    col = (t % GRID_W).astype(np.float64)
    inv = ROPE_BASE ** (-np.arange(0, HEAD_DIM // 2, 2, dtype=np.float64) / (HEAD_DIM // 2))
    ar = row[:, None] * inv
    ac = col[:, None] * inv
    ang = np.concatenate([ar, ar, ac, ac], axis=-1)
    cos = np.concatenate([np.cos(ang), np.ones((CTX, HEAD_DIM))], axis=0)
    sin = np.concatenate([np.sin(ang), np.zeros((CTX, HEAD_DIM))], axis=0)
    first = (np.arange(HEAD_DIM) % 64) < 32
    rope_a = np.where(first[None, :], -sin, 0.0)
    rope_b = np.where(first[None, :], 0.0, sin)

    def cs(n):
        k = np.arange(n)
        a = 2.0 * np.pi * np.outer(k, k) / n
        return np.cos(a), np.sin(a)

    c64, s64 = cs(N2)
    c128, s128 = cs(N1)
    c256, s256 = cs(CTX)
    m1 = np.concatenate([c64, -s64], axis=0)
    m2 = np.concatenate([c128, s128], axis=0)
    mc = np.concatenate([c256, s256], axis=0)
    csch = np.concatenate([c128, s128], axis=0)
    a = 2.0 * np.pi * np.outer(np.arange(N1), np.arange(N2)) / SEQ
    twr = np.broadcast_to(np.cos(a)[:, :, None], (N1, N2, 128))
    twi = np.broadcast_to(-np.sin(a)[:, :, None], (N1, N2, 128))
    tri = (np.arange(TM)[:, None] < np.arange(TM)[None, :]).astype(np.float32)
    kpos = np.arange(3 * QB)[:, None] - QB
    qpos = (np.arange(Q_PER_KV * QB) % QB)[None, :]
    band = np.where(np.abs(qpos - kpos) <= WINDOW, 0.0, NEG)
    perm = np.zeros((GRP * SUB, GRP * SUB), np.float32)
    rr, ss = np.meshgrid(np.arange(GRP), np.arange(SUB), indexing="ij")
    perm[(ss * GRP + rr).ravel(), (rr * SUB + ss).ravel()] = 1.0
    return dict(
        cos=jnp.asarray(cos, F32), rope_a=jnp.asarray(rope_a, F32), rope_b=jnp.asarray(rope_b, F32),
        m1=jnp.asarray(m1, BF16), m2=jnp.asarray(m2, BF16), mc=jnp.asarray(mc, BF16),
        csch=jnp.asarray(csch, BF16), twr=jnp.asarray(twr, F32), twi=jnp.asarray(twi, F32),
        tri=jnp.asarray(tri, BF16), perm=jnp.asarray(perm, BF16), band=jnp.asarray(band, F32))


MOD_TN = 1024


def _mod_kernel(cc_ref, w_ref, b_ref, o_ref):
    a = _silu(cc_ref[...])
    o_ref[...] = jnp.dot(a.astype(BF16), w_ref[...].astype(BF16),
                         preferred_element_type=F32) + b_ref[...]


def _modulation(cc, w_mod, b_mod):
    nl = w_mod.shape[0]
    n6 = w_mod.shape[2]
    return pl.pallas_call(
        _mod_kernel,
        grid=(nl, n6 // MOD_TN),
        in_specs=[pl.BlockSpec((8, D_MODEL), lambda l, j: (0, 0)),
                  pl.BlockSpec((None, D_MODEL, MOD_TN), lambda l, j: (l, 0, j)),
                  pl.BlockSpec((None, 1, MOD_TN), lambda l, j: (l, 0, j))],
        out_specs=pl.BlockSpec((None, 8, MOD_TN), lambda l, j: (l, 0, j)),
        out_shape=jax.ShapeDtypeStruct((nl, 8, n6), F32),
        compiler_params=_cp(("arbitrary", "arbitrary")),
        name="modulation",
    )(cc, w_mod, b_mod.reshape(nl, 1, n6))


def _mod_spec(chunk):
    return pl.BlockSpec((None, None, 1, D_MODEL), lambda i: (i // NT_LAT, chunk, 0, 0))


def _row_spec(n):
    return pl.BlockSpec((1, n), lambda i: (0, 0))


PCW = 256


def _proj_kernel(x_ref, g_ref, sc_ref, sh_ref, w_ref, gq_ref, gk_ref, cos_ref, ra_ref, rb_ref,
                 u_ref, q_ref, k_ref, v_ref, hb_ref):
    x = x_ref[...]
    ms = jnp.mean(x * x, axis=-1, keepdims=True)
    h = x * lax.rsqrt(ms + EPS) * g_ref[...]
    h = h * (1.0 + sc_ref[...]) + sh_ref[...]
    hb_ref[...] = h.astype(BF16)
    cos = cos_ref[...]
    ra = ra_ref[...]
    rb = rb_ref[...]

    def head(t, g):
        m = jnp.mean(t * t, axis=-1, keepdims=True)
        t = t * lax.rsqrt(m + EPS) * g
        return t * cos + pltpu.roll(t, 96, 1) * ra + pltpu.roll(t, 32, 1) * rb

    scale = HEAD_DIM ** -0.5 * LOG2E
    for c in range(D_IN // PCW):
        col = c * PCW
        p = jnp.dot(hb_ref[...], w_ref[:, col:col + PCW], preferred_element_type=F32)
        if col < D_F:
            u_ref[:, col:col + PCW] = p.astype(BF16)
        elif col < D_F + D_ATTN:
            o = col - D_F
            for j in range(PCW // HEAD_DIM):
                t = head(p[:, j * HEAD_DIM:(j + 1) * HEAD_DIM], gq_ref[...]) * scale
                q_ref[:, o + j * HEAD_DIM:o + (j + 1) * HEAD_DIM] = t.astype(BF16)
        elif col < D_F + D_ATTN + D_KV:
            o = col - D_F - D_ATTN
            for j in range(PCW // HEAD_DIM):
                t = head(p[:, j * HEAD_DIM:(j + 1) * HEAD_DIM], gk_ref[...])
                k_ref[:, o + j * HEAD_DIM:o + (j + 1) * HEAD_DIM] = t.astype(BF16)
        else:
            o = col - D_F - D_ATTN - D_KV
            v_ref[:, o:o + PCW] = p.astype(BF16)


def _project(xs, g1, modt, w_in_bf, gq, gk, tb):
    tile = lambda n: pl.BlockSpec((TM, n), lambda i: (i, 0))
    return pl.pallas_call(
        _proj_kernel,
        grid=(NT,),
        in_specs=[tile(D_MODEL), _row_spec(D_MODEL), _mod_spec(1), _mod_spec(0),
                  pl.BlockSpec((D_MODEL, D_IN), lambda i: (0, 0)),
                  _row_spec(HEAD_DIM), _row_spec(HEAD_DIM),
                  tile(HEAD_DIM), tile(HEAD_DIM), tile(HEAD_DIM)],
        out_specs=[tile(D_F), tile(D_ATTN), tile(D_KV), tile(D_KV)],
        out_shape=[jax.ShapeDtypeStruct((T, D_F), BF16), jax.ShapeDtypeStruct((T, D_ATTN), BF16),
                   jax.ShapeDtypeStruct((T, D_KV), BF16), jax.ShapeDtypeStruct((T, D_KV), BF16)],
        scratch_shapes=[pltpu.VMEM((TM, D_MODEL), BF16)],
        compiler_params=_cp(("arbitrary",)),
        name="norm_proj",
    )(xs, g1, modt, modt, w_in_bf, gq, gk, tb["cos"], tb["rope_a"], tb["rope_b"])


def _attn_kernel(sink_ref, q_ref, kp_ref, kc_ref, kn_ref, vp_ref, vc_ref, vn_ref, kx_ref, vx_ref,
                 ga_ref, band_ref, o_ref, a_scr):
    b = pl.program_id(0)
    lat = b < NQB_LAT
    off = lambda ok: jnp.where(ok, 0.0, NEG)
    bias = jnp.concatenate(
        [band_ref[0:QB] + off(jnp.logical_and(lat, b > 0)),
         band_ref[QB:2 * QB] + off(lat),
         band_ref[2 * QB:3 * QB] + off(b < NQB_LAT - 1)], axis=0)
    dn = (((1,), (1,)), ((), ()))
    for hk in range(N_KV):
        ks = slice(hk * HEAD_DIM, (hk + 1) * HEAD_DIM)
        qs = jnp.concatenate(
            [q_ref[:, (hk * Q_PER_KV + g) * HEAD_DIM:(hk * Q_PER_KV + g + 1) * HEAD_DIM]
             for g in range(Q_PER_KV)], axis=0)
        kw = jnp.concatenate([kp_ref[:, ks], kc_ref[:, ks], kn_ref[:, ks]], axis=0)
        vw = jnp.concatenate([vp_ref[:, ks], vc_ref[:, ks], vn_ref[:, ks]], axis=0)
        st = lax.dot_general(kw, qs, dn, preferred_element_type=F32) + bias
        sxt = lax.dot_general(kx_ref[:, ks], qs, dn, preferred_element_type=F32)
        sk = jnp.concatenate(
            [jnp.full((1, QB), sink_ref[hk * Q_PER_KV + g] * LOG2E, F32) for g in range(Q_PER_KV)],
            axis=1)
        m = jnp.maximum(jnp.maximum(jnp.max(st, axis=0, keepdims=True),
                                    jnp.max(sxt, axis=0, keepdims=True)), sk)
        e = jnp.exp2(st - m)
        ex = jnp.exp2(sxt - m)
        den = (jnp.sum(e, axis=0, keepdims=True) + jnp.sum(ex, axis=0, keepdims=True)
               + jnp.exp2(sk - m))
        dt = (((0,), (0,)), ((), ()))
        ot = (lax.dot_general(vw, e.astype(BF16), dt, preferred_element_type=F32)
              + lax.dot_general(vx_ref[:, ks], ex.astype(BF16), dt, preferred_element_type=F32)) / den
        for g in range(Q_PER_KV):
            h = hk * Q_PER_KV + g
            a_scr[h * HEAD_DIM:(h + 1) * HEAD_DIM, :] = ot[:, g * QB:(g + 1) * QB]
    at = a_scr[...]
    ms = jnp.mean(at * at, axis=0, keepdims=True)
    o_ref[...] = (at * lax.rsqrt(ms + EPS) * ga_ref[...]).T.astype(BF16)


def _attention(q, k, v, sink, g_a, band):
    kb = lambda f: pl.BlockSpec((QB, D_KV), lambda b: (f(b), 0))
    prev = lambda b: jnp.maximum(b - 1, 0)
    cur = lambda b: b
    nxt = lambda b: jnp.minimum(b + 1, NQB - 1)
    g_cols = jnp.broadcast_to(g_a.reshape(D_ATTN, 1), (D_ATTN, QB))
    return pl.pallas_call(
        _attn_kernel,
        grid=(NQB,),
        in_specs=[pl.BlockSpec(memory_space=pltpu.SMEM),
                  pl.BlockSpec((QB, D_ATTN), lambda b: (b, 0)),
                  kb(prev), kb(cur), kb(nxt), kb(prev), kb(cur), kb(nxt),
                  pl.BlockSpec((CTX, D_KV), lambda b: (SEQ // CTX, 0)),
                  pl.BlockSpec((CTX, D_KV), lambda b: (SEQ // CTX, 0)),
                  pl.BlockSpec((D_ATTN, QB), lambda b: (0, 0)),
                  pl.BlockSpec((3 * QB, Q_PER_KV * QB), lambda b: (0, 0))],
        out_specs=pl.BlockSpec((QB, D_ATTN), lambda b: (b, 0)),
        out_shape=jax.ShapeDtypeStruct((T, D_ATTN), BF16),
        scratch_shapes=[pltpu.VMEM((D_ATTN, QB), F32)],
        compiler_params=_cp(("arbitrary",)),
        name="attention",
    )(sink, q, k, k, k, v, v, v, k, v, g_cols, band)


F1_J = 8
F2_K = 4


def _f1_kernel(u_ref, m1_ref, twr_ref, twi_ref, z_ref):
    g = jnp.dot(m1_ref[...], u_ref[...], preferred_element_type=F32)
    for jl in range(F1_J):
        twr = twr_ref[jl]
        twi = twi_ref[jl]
        for lt in range(D_F // 128):
            c0 = jl * D_F + lt * 128
            gr = g[0:N2, c0:c0 + 128]
            gi = g[N2:2 * N2, c0:c0 + 128]
            z_ref[jl, 0:N2, lt * 128:(lt + 1) * 128] = (gr * twr - gi * twi).astype(BF16)
            z_ref[jl, N2:2 * N2, lt * 128:(lt + 1) * 128] = (gr * twi + gi * twr).astype(BF16)


def _f2_kernel(zr_ref, zi_ref, m2_ref, y_ref):
    pr = jnp.dot(m2_ref[...], zr_ref[...], preferred_element_type=F32)
    pi = jnp.dot(m2_ref[...], zi_ref[...], preferred_element_type=F32)
    yr = pr[0:N1] + pi[N1:2 * N1]
    yi = pi[0:N1] - pr[N1:2 * N1]
    for kl in range(F2_K):
        y_ref[:, kl * 2 * D_F:kl * 2 * D_F + D_F] = yr[:, kl * D_F:(kl + 1) * D_F].astype(BF16)
        y_ref[:, kl * 2 * D_F + D_F:(kl + 1) * 2 * D_F] = yi[:, kl * D_F:(kl + 1) * D_F].astype(BF16)


def _f3_kernel(y_ref, uc_ref, cs_ref, mc_ref, wf_ref, gf_ref, o_ref):
    i = pl.program_id(0)

    def finish(parts, scale):
        f = jnp.concatenate(parts, axis=1) * scale
        t = jnp.dot(f.astype(BF16), wf_ref[...], preferred_element_type=F32)
        ms = jnp.mean(t * t, axis=-1, keepdims=True)
        o_ref[...] = (t * lax.rsqrt(ms + EPS) * gf_ref[...]).astype(BF16)

    @pl.when(i < NT_LAT)
    def _():
        parts = []
        for g in range(N_FG):
            lhs = jnp.concatenate([y_ref[:, g * 128:(g + 1) * 128],
                                   y_ref[:, D_F + g * 128:D_F + (g + 1) * 128]], axis=1)
            parts.append(jnp.dot(lhs, cs_ref[...], preferred_element_type=F32))
        finish(parts, float((SEQ * 128) ** -0.5))

    @pl.when(i == NT_LAT)
    def _():
        pq = jnp.dot(mc_ref[...], uc_ref[...], preferred_element_type=F32)
        parts = []
        for g in range(N_FG):
            lhs = jnp.concatenate([pq[0:CTX, g * 128:(g + 1) * 128],
                                   -pq[CTX:2 * CTX, g * 128:(g + 1) * 128]], axis=1)
            parts.append(jnp.dot(lhs.astype(BF16), cs_ref[...], preferred_element_type=F32))
        finish(parts, float((CTX * 128) ** -0.5))


def _fourier(u, w_f_bf, g_f, tb):
    u2 = u.reshape(T // N1, N1 * D_F)
    z = pl.pallas_call(
        _f1_kernel,
        grid=(N1 // F1_J,),
        in_specs=[pl.BlockSpec((N2, F1_J * D_F), lambda c: (0, c)),
                  pl.BlockSpec((2 * N2, N2), lambda c: (0, 0)),
                  pl.BlockSpec((F1_J, N2, 128), lambda c: (c, 0, 0)),
                  pl.BlockSpec((F1_J, N2, 128), lambda c: (c, 0, 0))],
        out_specs=pl.BlockSpec((F1_J, 2 * N2, D_F), lambda c: (c, 0, 0)),
        out_shape=jax.ShapeDtypeStruct((N1, 2 * N2, D_F), BF16),
        compiler_params=_cp(("arbitrary",)),
        name="fourier_s1",
    )(u2, tb["m1"], tb["twr"], tb["twi"])
    z2 = z.reshape(N1, 2 * N2 * D_F)
    nk = N2 // F2_K
    y = pl.pallas_call(
        _f2_kernel,
        grid=(nk,),
        in_specs=[pl.BlockSpec((N1, F2_K * D_F), lambda k: (0, k)),
                  pl.BlockSpec((N1, F2_K * D_F), lambda k: (0, nk + k)),
                  pl.BlockSpec((2 * N1, N1), lambda k: (0, 0))],
        out_specs=pl.BlockSpec((N1, F2_K * 2 * D_F), lambda k: (0, k)),
        out_shape=jax.ShapeDtypeStruct((N1, N2 * 2 * D_F), BF16),
        compiler_params=_cp(("arbitrary",)),
        name="fourier_s2",
    )(z2, z2, tb["m2"])
    y2 = y.reshape(SEQ, 2 * D_F)
    return pl.pallas_call(
        _f3_kernel,
        grid=(NT,),
        in_specs=[pl.BlockSpec((TM, 2 * D_F), lambda i: (jnp.minimum(i, NT_LAT - 1), 0)),
                  pl.BlockSpec((CTX, D_F), lambda i: (SEQ // CTX, 0)),
                  pl.BlockSpec((2 * 128, 128), lambda i: (0, 0)),
                  pl.BlockSpec((2 * CTX, CTX), lambda i: (0, 0)),
                  pl.BlockSpec((D_F, D_F), lambda i: (0, 0)),
                  _row_spec(D_F)],
        out_specs=pl.BlockSpec((TM, D_F), lambda i: (i, 0)),
        out_shape=jax.ShapeDtypeStruct((T, D_F), BF16),
        compiler_params=_cp(("arbitrary",)),
        name="fourier_s3",
    )(y2, u, tb["csch"], tb["mc"], w_f_bf, g_f)


def _merge_kernel(x_ref, fn_ref, an_ref, wo_ref, ga1_ref, g2_ref, sc2_ref, sh2_ref, ga2_ref,
                  wsg_ref, wsu_ref, wsd_ref, wrh_ref, wrl_ref, br_ref, tri_ref,
                  xmid_ref, h2_ref, info_ref, wcol_ref, cnt_ref, carry_scr):
    i = pl.program_id(0)

    @pl.when(i == 0)
    def _():
        carry_scr[...] = jnp.zeros_like(carry_scr)

    m = (jnp.dot(fn_ref[...], wo_ref[0:D_F, :], preferred_element_type=F32)
         + jnp.dot(an_ref[...], wo_ref[D_F:D_MODEL, :], preferred_element_type=F32))
    x1 = x_ref[...] + ga1_ref[...] * m
    ms = jnp.mean(x1 * x1, axis=-1, keepdims=True)
    h = x1 * lax.rsqrt(ms + EPS) * g2_ref[...]
    h = h * (1.0 + sc2_ref[...]) + sh2_ref[...]
    hb = h.astype(BF16)
    hp = _pack_bf16_pairs(h)
    for s in range(SUB):
        h2_ref[:, s, :] = hp[:, s * PK:(s + 1) * PK]

    a = jnp.dot(hb, wsg_ref[...], preferred_element_type=F32)
    u = jnp.dot(hb, wsu_ref[...], preferred_element_type=F32)
    act = (_silu(a) * u).astype(BF16)
    ys = jnp.dot(act, wsd_ref[...], preferred_element_type=F32)
    xmid_ref[...] = x1 + ga2_ref[...] * ys

    hl = (h - hb.astype(F32)).astype(BF16)
    dn = (((1,), (1,)), ((), ()))
    lg = (lax.dot_general(wrh_ref[...], hb, dn, preferred_element_type=F32)
          + (lax.dot_general(wrl_ref[...], hb, dn, preferred_element_type=F32)
             + lax.dot_general(wrh_ref[...], hl, dn, preferred_element_type=F32)))
    score = jax.nn.sigmoid(lg)
    sel = score + br_ref[...]
    s = [sel[jj * N_GROUPS:(jj + 1) * N_GROUPS] for jj in range(EPG)]
    sr = [score[jj * N_GROUPS:(jj + 1) * N_GROUPS] for jj in range(EPG)]
    hi01, lo01 = jnp.maximum(s[0], s[1]), jnp.minimum(s[0], s[1])
    hi23, lo23 = jnp.maximum(s[2], s[3]), jnp.minimum(s[2], s[3])
    gscore = jnp.maximum(hi01, hi23) + jnp.maximum(jnp.minimum(hi01, hi23), jnp.maximum(lo01, lo23))
    gi = lax.broadcasted_iota(I32, (N_GROUPS, TM), 0)
    gmax = jnp.max(gscore, axis=0, keepdims=True)
    g_idx = jnp.min(jnp.where(gscore == gmax, gi, N_GROUPS), axis=0, keepdims=True)
    gsel = gi == g_idx
    v = [jnp.sum(jnp.where(gsel, s[jj], 0.0), axis=0, keepdims=True) for jj in range(EPG)]
    vr = [jnp.sum(jnp.where(gsel, sr[jj], 0.0), axis=0, keepdims=True) for jj in range(EPG)]

    def first_argmax(vals):
        best = vals[0]
        idx = jnp.zeros((1, TM), I32)
        for jj in range(1, EPG):
            upd = vals[jj] > best
            best = jnp.where(upd, vals[jj], best)
            idx = jnp.where(upd, jj, idx)
        return idx

    i1 = first_argmax(v)
    i2 = first_argmax([jnp.where(i1 == jj, -jnp.inf, v[jj]) for jj in range(EPG)])

    def pick(vals, idx):
        out = vals[0]
        for jj in range(1, EPG):
            out = jnp.where(idx == jj, vals[jj], out)
        return out

    w1 = pick(vr, i1)
    w2 = pick(vr, i2)
    wsum = w1 + w2
    w1 = w1 / wsum
    w2 = w2 / wsum

    ri = lax.broadcasted_iota(I32, (N_EXPERTS, TM), 0)
    oh1 = ri == i1 * N_GROUPS + g_idx
    oh2 = ri == i2 * N_GROUPS + g_idx
    oh = jnp.logical_or(oh1, oh2).astype(F32)
    pre = jnp.dot(oh.astype(BF16), tri_ref[...], preferred_element_type=F32) + carry_scr[:, 0:1]
    r1 = jnp.sum(jnp.where(oh1, pre, 0.0), axis=0, keepdims=True)
    r2 = jnp.sum(jnp.where(oh2, pre, 0.0), axis=0, keepdims=True)
    carry = carry_scr[...] + jnp.sum(oh, axis=1, keepdims=True)
    carry_scr[...] = carry
    cnt_ref[...] = carry

    zi = jnp.zeros((4, TM), I32)
    info_ref[...] = jnp.concatenate(
        [g_idx * EPG + i1, g_idx * EPG + i2, r1.astype(I32), r2.astype(I32), zi], axis=0)
    wrow = jnp.concatenate([w1, w2, jnp.zeros((126, TM), F32)], axis=0)
    wcol_ref[...] = wrow.T


def _merge(xs, fn, an, w_out_bf, modt, g2, wsg, wsu, wsd, wrh, wrl, br, tb):
    tile = lambda n: pl.BlockSpec((TM, n), lambda i: (i, 0))
    full = lambda a, b: pl.BlockSpec((a, b), lambda i: (0, 0))
    return pl.pallas_call(
        _merge_kernel,
        grid=(NT,),
        in_specs=[tile(D_MODEL), tile(D_F), tile(D_ATTN), full(D_MODEL, D_MODEL),
                  _mod_spec(2), _row_spec(D_MODEL), _mod_spec(4), _mod_spec(3), _mod_spec(5),
                  full(D_MODEL, D_SHARED), full(D_MODEL, D_SHARED), full(D_SHARED, D_MODEL),
                  full(N_EXPERTS, D_MODEL), full(N_EXPERTS, D_MODEL), full(N_EXPERTS, TM),
                  full(TM, TM)],
        out_specs=[tile(D_MODEL), pl.BlockSpec((TM, SUB, PK), lambda i: (i, 0, 0)),
                   pl.BlockSpec((8, TM), lambda i: (0, i)),
                   pl.BlockSpec((TM, 128), lambda i: (i, 0)),
                   pl.BlockSpec((N_EXPERTS, 128), lambda i: (0, 0))],
        out_shape=[jax.ShapeDtypeStruct((T, D_MODEL), F32), jax.ShapeDtypeStruct((T, SUB, PK), I32),
                   jax.ShapeDtypeStruct((8, T), I32), jax.ShapeDtypeStruct((T, 128), F32),
                   jax.ShapeDtypeStruct((N_EXPERTS, 128), F32)],
        scratch_shapes=[pltpu.VMEM((N_EXPERTS, 128), F32)],
        compiler_params=_cp(("arbitrary",)),
        name="merge_route",
    )(xs, fn, an, w_out_bf, modt, g2, modt, modt, modt, wsg, wsu, wsd, wrh, wrl, br, tb["tri"])


def _dispatch_kernel(dest_ref, pend_ref, padded_ref, h_ref, xb_hbm, zero_scr, sem, zsem):
    i = pl.program_id(0)

    def zero_copy(row0):
        return pltpu.make_async_copy(zero_scr, xb_hbm.at[pl.ds(row0, BM)], zsem)

    @pl.when(i == 0)
    def _():
        zero_scr[...] = jnp.zeros_like(zero_scr)
        n_used = pend_ref[N_EXPERTS - 1] // BM
        for e in range(N_EXPERTS):
            @pl.when(padded_ref[e] > 0)
            def _():
                zero_copy(pend_ref[e] - BM).start()
        for blk in range(NB - N_EXPERTS, NB):
            @pl.when(blk >= n_used)
            def _():
                zero_copy(blk * BM).start()
        for e in range(N_EXPERTS):
            @pl.when(padded_ref[e] > 0)
            def _():
                zero_copy(0).wait()
        for blk in range(NB - N_EXPERTS, NB):
            @pl.when(blk >= n_used)
            def _():
                zero_copy(0).wait()

    def copy(r, d):
        return pltpu.make_async_copy(h_ref.at[pl.ds(r, 1)], xb_hbm.at[pl.ds(d, 1)], sem)

    def issue(r, c):
        t = i * TM + r
        copy(r, dest_ref[2 * t]).start()
        copy(r, dest_ref[2 * t + 1]).start(priority=1)
        return c

    lax.fori_loop(0, TM, issue, 0)

    def drain(r, c):
        copy(0, 0).wait()
        copy(0, 0).wait()
        return c

    lax.fori_loop(0, TM, drain, 0)


def _dispatch(dest, pad_end, padded, h2):
    return pl.pallas_call(
        _dispatch_kernel,
        grid_spec=pltpu.PrefetchScalarGridSpec(
            num_scalar_prefetch=3, grid=(NT,),
            in_specs=[pl.BlockSpec((TM, SUB, PK), lambda i, d, pe, pd: (i, 0, 0))],
            out_specs=pl.BlockSpec(memory_space=pl.ANY),
            scratch_shapes=[pltpu.VMEM((BM, SUB, PK), I32), pltpu.SemaphoreType.DMA(()),
                            pltpu.SemaphoreType.DMA(())]),
        out_shape=jax.ShapeDtypeStruct((ROWS, SUB, PK), I32),
        compiler_params=pltpu.CompilerParams(dimension_semantics=("arbitrary",),
                                             has_side_effects=True),
        name="dispatch",
    )(dest, pad_end, padded, h2)


def _pack_bf16_pairs(h):
    bits = lax.bitcast_convert_type(h.astype(BF16).astype(F32), I32)
    half = h.shape[1] // 2
    return (bits[:, :half] & -65536) | lax.shift_right_logical(bits[:, half:], 16)


GRP = 32


def _expert_kernel(eo_ref, es_ref, n_ref, x_ref, perm_ref, wg_hbm, wu_hbm, wd_hbm, y_ref,
                   wgf, wuf, wdf, wgb, wub, wdb, xs_scr, wsem, *, layer):
    b = pl.program_id(0)
    k = eo_ref[b]
    first = jnp.logical_or(b == 0, k != eo_ref[jnp.maximum(b - 1, 0)])

    def copies(kk, slot):
        e = es_ref[kk]
        return (pltpu.make_async_copy(wg_hbm.at[layer, e], wgf.at[slot], wsem.at[slot, 0]),
                pltpu.make_async_copy(wu_hbm.at[layer, e], wuf.at[slot], wsem.at[slot, 1]),
                pltpu.make_async_copy(wd_hbm.at[layer, e], wdf.at[slot], wsem.at[slot, 2]))

    @pl.when(b == 0)
    def _():
        for c in copies(0, 0):
            c.start()

    @pl.when(jnp.logical_and(first, b < n_ref[0]))
    def _():
        slot = k % 2
        for c in copies(k, slot):
            c.wait()

        @pl.when(k + 1 < n_ref[1])
        def _():
            for c in copies(k + 1, 1 - slot):
                c.start()

        wgb[...] = wgf[slot].astype(BF16)
        wub[...] = wuf[slot].astype(BF16)
        wdb[...] = wdf[slot].astype(BF16)

    @pl.when(b < n_ref[0])
    def _():
        half = D_MODEL // 2
        for grp in range(BM // GRP):
            w = x_ref[grp * GRP:(grp + 1) * GRP].reshape(GRP * SUB, PK)
            hi = lax.bitcast_convert_type(w & -65536, F32).astype(BF16)
            lo = lax.bitcast_convert_type(w << 16, F32).astype(BF16)
            g = jnp.concatenate([hi, lo], axis=1)
            r = jnp.dot(perm_ref[...], g, preferred_element_type=F32).astype(BF16)
            rows = slice(grp * GRP, (grp + 1) * GRP)
            for s in range(SUB):
                xs_scr[rows, s * PK:(s + 1) * PK] = r[s * GRP:(s + 1) * GRP, 0:PK]
                xs_scr[rows, half + s * PK:half + (s + 1) * PK] = r[s * GRP:(s + 1) * GRP, PK:2 * PK]
        x = xs_scr[...]
        a = jnp.dot(x, wgb[...], preferred_element_type=F32)
        u = jnp.dot(x, wub[...], preferred_element_type=F32)
        act = (_silu(a) * u).astype(BF16)
        yp = _pack_bf16_pairs(jnp.dot(act, wdb[...], preferred_element_type=F32))
        for s in range(SUB):
            y_ref[:, s, :] = yp[:, s * PK:(s + 1) * PK]

    @pl.when(b >= n_ref[0])
    def _():
        y_ref[...] = jnp.zeros_like(y_ref)


def _experts(l, eo, es, nn, xb, perm, w_g, w_u, w_d):
    hbm = pl.BlockSpec(memory_space=pl.ANY)
    return pl.pallas_call(
        functools.partial(_expert_kernel, layer=l),
        grid_spec=pltpu.PrefetchScalarGridSpec(
            num_scalar_prefetch=3, grid=(NB,),
            in_specs=[pl.BlockSpec((BM, SUB, PK), lambda b, eo, es, n: (jnp.minimum(b, n[0] - 1), 0, 0)),
                      pl.BlockSpec((GRP * SUB, GRP * SUB), lambda b, eo, es, n: (0, 0)),
                      hbm, hbm, hbm],
            out_specs=pl.BlockSpec((BM, SUB, PK), lambda b, eo, es, n: (b, 0, 0)),
            scratch_shapes=[pltpu.VMEM((2, D_MODEL, D_EXPERT), F32),
                            pltpu.VMEM((2, D_MODEL, D_EXPERT), F32),
                            pltpu.VMEM((2, D_EXPERT, D_MODEL), F32),
                            pltpu.VMEM((D_MODEL, D_EXPERT), BF16),
                            pltpu.VMEM((D_MODEL, D_EXPERT), BF16),
                            pltpu.VMEM((D_EXPERT, D_MODEL), BF16),
                            pltpu.VMEM((BM, D_MODEL), BF16),
                            pltpu.SemaphoreType.DMA((2, 3))]),
        out_shape=jax.ShapeDtypeStruct((ROWS, SUB, PK), I32),
        compiler_params=_cp(("arbitrary",)),
        name="experts",
    )(eo, es, nn, xb, perm, w_g, w_u, w_d)


def _combine_kernel(dest_ref, y_hbm, xmid_ref, wcol_ref, ga2_ref, perm_ref, o_ref, b1, b2, sem, *, n):
    i = pl.program_id(0)

    def copy(d, buf, slot, r):
        return pltpu.make_async_copy(y_hbm.at[pl.ds(d, 1)], buf.at[slot, pl.ds(r, 1)], sem.at[slot])

    def issue(tile, slot):
        def body(r, c):
            t = tile * TM + r
            copy(dest_ref[2 * t], b1, slot, r).start()
            copy(dest_ref[2 * t + 1], b2, slot, r).start(priority=1)
            return c
        lax.fori_loop(0, TM, body, 0)

    @pl.when(i == 0)
    def _():
        issue(0, 0)

    @pl.when(i + 1 < n)
    def _():
        issue(i + 1, (i + 1) % 2)

    slot = i % 2

    def drain(r, c):
        copy(0, b1, slot, 0).wait()
        copy(0, b2, slot, 0).wait()
        return c

    lax.fori_loop(0, TM, drain, 0)

    def unpack_group(buf, grp):
        w = buf[slot, grp * GRP:(grp + 1) * GRP].reshape(GRP * SUB, PK)
        hi = lax.bitcast_convert_type(w & -65536, F32).astype(BF16)
        lo = lax.bitcast_convert_type(w << 16, F32).astype(BF16)
        return jnp.dot(perm_ref[...], jnp.concatenate([hi, lo], axis=1), preferred_element_type=F32)

    half = D_MODEL // 2
    for grp in range(TM // GRP):
        rows = slice(grp * GRP, (grp + 1) * GRP)
        w1 = wcol_ref[rows, 0:1]
        w2 = wcol_ref[rows, 1:2]
        r1 = unpack_group(b1, grp)
        r2 = unpack_group(b2, grp)
        for s in range(SUB):
            sr = slice(s * GRP, (s + 1) * GRP)
            for c0, lanes in ((s * PK, slice(0, PK)), (half + s * PK, slice(PK, 2 * PK))):
                cs = slice(c0, c0 + PK)
                o_ref[rows, cs] = xmid_ref[rows, cs] + ga2_ref[:, cs] * (w1 * r1[sr, lanes] + w2 * r2[sr, lanes])


def _combine(dest, yb, xmid, wcol, modt, perm, n_tiles):
    tile = lambda n: pl.BlockSpec((TM, n), lambda i, d: (i, 0))
    return pl.pallas_call(
        functools.partial(_combine_kernel, n=n_tiles),
        grid_spec=pltpu.PrefetchScalarGridSpec(
            num_scalar_prefetch=1, grid=(n_tiles,),
            in_specs=[pl.BlockSpec(memory_space=pl.ANY), tile(D_MODEL), tile(128),
                      pl.BlockSpec((None, None, 1, D_MODEL), lambda i, d: (i // NT_LAT, 5, 0, 0)),
                      pl.BlockSpec((GRP * SUB, GRP * SUB), lambda i, d: (0, 0))],
            out_specs=tile(D_MODEL),
            scratch_shapes=[pltpu.VMEM((2, TM, SUB, PK), I32), pltpu.VMEM((2, TM, SUB, PK), I32),
                            pltpu.SemaphoreType.DMA((2,))]),
        out_shape=jax.ShapeDtypeStruct((n_tiles * TM, D_MODEL), F32),
        compiler_params=_cp(("arbitrary",)),
        name="combine",
    )(dest, yb, xmid, wcol, modt, perm)


def _perm_experts(a):
    return a.reshape(a.shape[:-1] + (N_GROUPS, EPG)).swapaxes(-1, -2).reshape(a.shape)


def kernel(x, c, ctx, c_ctx, w_mod, b_mod, g_norm1, g_norm2, w_in, w_fourier, g_q, g_k, sink,
           g_branch_f, g_branch_a, w_out, w_router, b_router, w_exp_gate, w_exp_up, w_exp_down,
           w_sh_gate, w_sh_up, w_sh_down):
    assert x.shape == (1, SEQ, D_MODEL) and ctx.shape == (1, CTX, D_MODEL)
    tb = _tables()
    xs = jnp.concatenate([x[0], ctx[0]], axis=0)

    cc = jnp.zeros((8, D_MODEL), F32).at[0].set(c[0]).at[1].set(c_ctx)
    mods = _modulation(cc, w_mod, b_mod)

    wr_t = _perm_experts(w_router).T
    wrh = wr_t.astype(BF16)
    wrl = (wr_t - wrh.astype(F32)).astype(BF16)
    br = jnp.broadcast_to(_perm_experts(b_router)[:, None], (N_EXPERTS, TM)).astype(F32)

    for l in range(DEPTH):
        last = l == DEPTH - 1
        modt = mods[l, 0:2].reshape(2, 6, 1, D_MODEL)
        row = lambda a: a[l].reshape(1, -1)
        u, q, k, v = _project(xs, row(g_norm1), modt, w_in[l].astype(BF16), row(g_q), row(g_k), tb)
        an = _attention(q, k, v, sink[l], row(g_branch_a), tb["band"])
        fn = _fourier(u, w_fourier[l].astype(BF16), row(g_branch_f), tb)
        xmid, h2, info, wcol, cnt = _merge(
            xs, fn, an, w_out[l].astype(BF16), modt, row(g_norm2),
            w_sh_gate[l].astype(BF16), w_sh_up[l].astype(BF16), w_sh_down[l].astype(BF16),
            wrh, wrl, br, tb)

        xs = _routed(l, info, cnt, h2, xmid, wcol, modt, w_exp_gate, w_exp_up, w_exp_down, tb,
                     NT_LAT if last else NT)
    return xs[None]


def _routed(l, info, cnt, h2, xmid, wcol, modt, w_g, w_u, w_d, tb, n_tiles):
    counts = cnt[:, 0].astype(I32).reshape(EPG, N_GROUPS).T.reshape(N_EXPERTS)
    padded = ((counts + BM - 1) // BM) * BM
    pad_end = jnp.cumsum(padded).astype(I32)
    pad_start = pad_end - padded
    start = jnp.sum(jnp.where(info[0:2, :, None] == jnp.arange(N_EXPERTS, dtype=I32),
                              pad_start, 0), axis=-1)
    dest = (start + info[2:4]).T.reshape(2 * T)
    ids = jnp.arange(N_EXPERTS, dtype=I32)
    block_row0 = jnp.arange(NB, dtype=I32)[:, None] * BM
    block_e = jnp.minimum(jnp.sum((pad_end <= block_row0).astype(I32), axis=-1), N_EXPERTS - 1)
    used = padded > 0
    ordinal = jnp.cumsum(used.astype(I32)) - 1
    eo = jnp.sum(jnp.where(block_e[:, None] == ids, ordinal, 0), axis=-1)
    es = jnp.sum(jnp.where(jnp.logical_and(used, ordinal == ids[:, None]), ids, 0), axis=-1)
    nn = jnp.stack([pad_end[N_EXPERTS - 1] // BM, ordinal[N_EXPERTS - 1] + 1]).astype(I32)
    xb = _dispatch(dest, pad_end, padded, h2)
    yb = _experts(l, eo.astype(I32), es.astype(I32), nn, xb, tb["perm"], w_g, w_u, w_d)
    return _combine(dest, yb, xmid, wcol, modt, tb["perm"], n_tiles)
```

```python
import functools

import numpy as np
import jax
import jax.numpy as jnp
from jax import lax
from jax.experimental import pallas as pl
from jax.experimental.pallas import tpu as pltpu

F32 = jnp.float32
BF16 = jnp.bfloat16
I32 = jnp.int32

D_MODEL = 2048
SEQ = 8192
CTX = 256
T = SEQ + CTX
DEPTH = 4
GRID_W = 64
HEAD_DIM = 128
D_F = 512
N_FG = 4
N_HEADS = 12
N_KV = 4
Q_PER_KV = 3
D_ATTN = N_HEADS * HEAD_DIM
D_KV = N_KV * HEAD_DIM
D_IN = D_F + D_ATTN + 2 * D_KV
ROPE_BASE = 10000.0
N_EXPERTS = 32
N_GROUPS = 8
EPG = 4
D_EXPERT = 512
D_SHARED = 512
EPS = 1e-6
NEG = -1e30
LOG2E = 1.4426950408889634
WINDOW = 128

TM = 256
NT = T // TM
NT_LAT = SEQ // TM
QB = 128
NQB = T // QB
NQB_LAT = SEQ // QB
BM = 256
NB = (2 * T + BM - 1) // BM + N_EXPERTS
ROWS = NB * BM
N1 = 128
N2 = 64
SUB = 8
RL = D_MODEL // SUB
PK = D_MODEL // 2 // SUB
VMEM_LIMIT = 56 * 1024 * 1024


def _cp(sem):
    return pltpu.CompilerParams(dimension_semantics=sem, vmem_limit_bytes=VMEM_LIMIT)


def _silu(a):
    return a * jax.nn.sigmoid(a)


@functools.lru_cache(maxsize=None)
def _tables():
    t = np.arange(SEQ)
    row = (t // GRID_W).astype(np.float64)
    col = (t % GRID_W).astype(np.float64)
    inv = ROPE_BASE ** (-np.arange(0, HEAD_DIM // 2, 2, dtype=np.float64) / (HEAD_DIM // 2))
    ar = row[:, None] * inv
    ac = col[:, None] * inv
    ang = np.concatenate([ar, ar, ac, ac], axis=-1)
    cos = np.concatenate([np.cos(ang), np.ones((CTX, HEAD_DIM))], axis=0)
    sin = np.concatenate([np.sin(ang), np.zeros((CTX, HEAD_DIM))], axis=0)
    first = (np.arange(HEAD_DIM) % 64) < 32
    rope_a = np.where(first[None, :], -sin, 0.0)
    rope_b = np.where(first[None, :], 0.0, sin)

    def cs(n):
        k = np.arange(n)
        a = 2.0 * np.pi * np.outer(k, k) / n
        return np.cos(a), np.sin(a)

    c64, s64 = cs(N2)
    c128, s128 = cs(N1)
    c256, s256 = cs(CTX)
    m1 = np.concatenate([c64, -s64], axis=0)
    m2 = np.concatenate([c128, s128], axis=0)
    mc = np.concatenate([c256, s256], axis=0)
    csch = np.concatenate([c128, s128], axis=0)
    a = 2.0 * np.pi * np.outer(np.arange(N1), np.arange(N2)) / SEQ
    twr = np.broadcast_to(np.cos(a)[:, :, None], (N1, N2, 128))
    twi = np.broadcast_to(-np.sin(a)[:, :, None], (N1, N2, 128))
    tri = (np.arange(TM)[:, None] < np.arange(TM)[None, :]).astype(np.float32)
    kpos = np.arange(3 * QB)[:, None] - QB
    qpos = (np.arange(Q_PER_KV * QB) % QB)[None, :]
    band = np.where(np.abs(qpos - kpos) <= WINDOW, 0.0, NEG)
    perm = np.zeros((GRP * SUB, GRP * SUB), np.float32)
    rr, ss = np.meshgrid(np.arange(GRP), np.arange(SUB), indexing="ij")
    perm[(ss * GRP + rr).ravel(), (rr * SUB + ss).ravel()] = 1.0
    return dict(
        cos=jnp.asarray(cos, F32), rope_a=jnp.asarray(rope_a, F32), rope_b=jnp.asarray(rope_b, F32),
        m1=jnp.asarray(m1, BF16), m2=jnp.asarray(m2, BF16), mc=jnp.asarray(mc, BF16),
        csch=jnp.asarray(csch, BF16), twr=jnp.asarray(twr, F32), twi=jnp.asarray(twi, F32),
        tri=jnp.asarray(tri, BF16), perm=jnp.asarray(perm, BF16), permt=jnp.asarray(perm.T, BF16),
        band=jnp.asarray(band, F32))


MOD_TN = 1024


def _mod_kernel(cc_ref, w_ref, b_ref, o_ref):
    a = _silu(cc_ref[...])
    o_ref[...] = jnp.dot(a.astype(BF16), w_ref[...].astype(BF16),
                         preferred_element_type=F32) + b_ref[...]


def _modulation(cc, w_mod, b_mod):
    nl = w_mod.shape[0]
    n6 = w_mod.shape[2]
    return pl.pallas_call(
        _mod_kernel,
        grid=(nl, n6 // MOD_TN),
        in_specs=[pl.BlockSpec((8, D_MODEL), lambda l, j: (0, 0)),
                  pl.BlockSpec((None, D_MODEL, MOD_TN), lambda l, j: (l, 0, j)),
                  pl.BlockSpec((None, 1, MOD_TN), lambda l, j: (l, 0, j))],
        out_specs=pl.BlockSpec((None, 8, MOD_TN), lambda l, j: (l, 0, j)),
        out_shape=jax.ShapeDtypeStruct((nl, 8, n6), F32),
        compiler_params=_cp(("arbitrary", "arbitrary")),
        name="modulation",
    )(cc, w_mod, b_mod.reshape(nl, 1, n6))


def _mod_spec(chunk):
    return pl.BlockSpec((None, None, 1, D_MODEL), lambda i: (i // NT_LAT, chunk, 0, 0))


def _row_spec(n):
    return pl.BlockSpec((1, n), lambda i: (0, 0))


PCW = 256


def _proj_kernel(x_ref, c_ref, g_ref, sc_ref, sh_ref, w_ref, gq_ref, gk_ref, cos_ref, ra_ref, rb_ref,
                 u_ref, q_ref, k_ref, v_ref, hb_ref):
    x = jnp.where(pl.program_id(0) == NT_LAT, c_ref[...], x_ref[...])
    ms = jnp.mean(x * x, axis=-1, keepdims=True)
    h = x * lax.rsqrt(ms + EPS) * g_ref[...]
    h = h * (1.0 + sc_ref[...]) + sh_ref[...]
    hb_ref[...] = h.astype(BF16)
    cos = cos_ref[...]
    ra = ra_ref[...]
    rb = rb_ref[...]

    def head(t, g):
        m = jnp.mean(t * t, axis=-1, keepdims=True)
        t = t * lax.rsqrt(m + EPS) * g
        return t * cos + pltpu.roll(t, 96, 1) * ra + pltpu.roll(t, 32, 1) * rb

    scale = HEAD_DIM ** -0.5 * LOG2E
    for c in range(D_IN // PCW):
        col = c * PCW
        p = jnp.dot(hb_ref[...], w_ref[:, col:col + PCW], preferred_element_type=F32)
        if col < D_F:
            u_ref[:, col:col + PCW] = p.astype(BF16)
        elif col < D_F + D_ATTN:
            o = col - D_F
            for j in range(PCW // HEAD_DIM):
                t = head(p[:, j * HEAD_DIM:(j + 1) * HEAD_DIM], gq_ref[...]) * scale
                q_ref[:, o + j * HEAD_DIM:o + (j + 1) * HEAD_DIM] = t.astype(BF16)
        elif col < D_F + D_ATTN + D_KV:
            o = col - D_F - D_ATTN
            for j in range(PCW // HEAD_DIM):
                t = head(p[:, j * HEAD_DIM:(j + 1) * HEAD_DIM], gk_ref[...])
                k_ref[:, o + j * HEAD_DIM:o + (j + 1) * HEAD_DIM] = t.astype(BF16)
        else:
            o = col - D_F - D_ATTN - D_KV
            v_ref[:, o:o + PCW] = p.astype(BF16)


def _project(xa, ca, c_blk, g1, modt, w_in_bf, gq, gk, tb):
    tile = lambda n: pl.BlockSpec((TM, n), lambda i: (i, 0))
    return pl.pallas_call(
        _proj_kernel,
        grid=(NT,),
        in_specs=_stream_specs(c_blk) + [
                  _row_spec(D_MODEL), _mod_spec(1), _mod_spec(0),
                  pl.BlockSpec((D_MODEL, D_IN), lambda i: (0, 0)),
                  _row_spec(HEAD_DIM), _row_spec(HEAD_DIM),
                  tile(HEAD_DIM), tile(HEAD_DIM), tile(HEAD_DIM)],
        out_specs=[tile(D_F), tile(D_ATTN), tile(D_KV), tile(D_KV)],
        out_shape=[jax.ShapeDtypeStruct((T, D_F), BF16), jax.ShapeDtypeStruct((T, D_ATTN), BF16),
                   jax.ShapeDtypeStruct((T, D_KV), BF16), jax.ShapeDtypeStruct((T, D_KV), BF16)],
        scratch_shapes=[pltpu.VMEM((TM, D_MODEL), BF16)],
        compiler_params=_cp(("arbitrary",)),
        name="norm_proj",
    )(xa, ca, g1, modt, modt, w_in_bf, gq, gk, tb["cos"], tb["rope_a"], tb["rope_b"])


def _attn_kernel(sink_ref, q_ref, kp_ref, kc_ref, kn_ref, vp_ref, vc_ref, vn_ref, kx_ref, vx_ref,
                 ga_ref, band_ref, o_ref, a_scr):
    b = pl.program_id(0)
    lat = b < NQB_LAT
    off = lambda ok: jnp.where(ok, 0.0, NEG)
    bias = jnp.concatenate(
        [band_ref[0:QB] + off(jnp.logical_and(lat, b > 0)),
         band_ref[QB:2 * QB] + off(lat),
         band_ref[2 * QB:3 * QB] + off(b < NQB_LAT - 1)], axis=0)
    dn = (((1,), (1,)), ((), ()))
    for hk in range(N_KV):
        ks = slice(hk * HEAD_DIM, (hk + 1) * HEAD_DIM)
        qs = jnp.concatenate(
            [q_ref[:, (hk * Q_PER_KV + g) * HEAD_DIM:(hk * Q_PER_KV + g + 1) * HEAD_DIM]
             for g in range(Q_PER_KV)], axis=0)
        kw = jnp.concatenate([kp_ref[:, ks], kc_ref[:, ks], kn_ref[:, ks]], axis=0)
        vw = jnp.concatenate([vp_ref[:, ks], vc_ref[:, ks], vn_ref[:, ks]], axis=0)
        st = lax.dot_general(kw, qs, dn, preferred_element_type=F32) + bias
        sxt = lax.dot_general(kx_ref[:, ks], qs, dn, preferred_element_type=F32)
        sk = jnp.concatenate(
            [jnp.full((1, QB), sink_ref[hk * Q_PER_KV + g] * LOG2E, F32) for g in range(Q_PER_KV)],
            axis=1)
        m = jnp.maximum(jnp.maximum(jnp.max(st, axis=0, keepdims=True),
                                    jnp.max(sxt, axis=0, keepdims=True)), sk)
        e = jnp.exp2(st - m)
        ex = jnp.exp2(sxt - m)
        den = (jnp.sum(e, axis=0, keepdims=True) + jnp.sum(ex, axis=0, keepdims=True)
               + jnp.exp2(sk - m))
        dt = (((0,), (0,)), ((), ()))
        ot = (lax.dot_general(vw, e.astype(BF16), dt, preferred_element_type=F32)
              + lax.dot_general(vx_ref[:, ks], ex.astype(BF16), dt, preferred_element_type=F32)) / den
        for g in range(Q_PER_KV):
            h = hk * Q_PER_KV + g
            a_scr[h * HEAD_DIM:(h + 1) * HEAD_DIM, :] = ot[:, g * QB:(g + 1) * QB]
    at = a_scr[...]
    ms = jnp.mean(at * at, axis=0, keepdims=True)
    o_ref[...] = (at * lax.rsqrt(ms + EPS) * ga_ref[...]).T.astype(BF16)


def _attention(q, k, v, sink, g_a, band):
    kb = lambda f: pl.BlockSpec((QB, D_KV), lambda b: (f(b), 0))
    prev = lambda b: jnp.maximum(b - 1, 0)
    cur = lambda b: b
    nxt = lambda b: jnp.minimum(b + 1, NQB - 1)
    g_cols = jnp.broadcast_to(g_a.reshape(D_ATTN, 1), (D_ATTN, QB))
    return pl.pallas_call(
        _attn_kernel,
        grid=(NQB,),
        in_specs=[pl.BlockSpec(memory_space=pltpu.SMEM),
                  pl.BlockSpec((QB, D_ATTN), lambda b: (b, 0)),
                  kb(prev), kb(cur), kb(nxt), kb(prev), kb(cur), kb(nxt),
                  pl.BlockSpec((CTX, D_KV), lambda b: (SEQ // CTX, 0)),
                  pl.BlockSpec((CTX, D_KV), lambda b: (SEQ // CTX, 0)),
                  pl.BlockSpec((D_ATTN, QB), lambda b: (0, 0)),
                  pl.BlockSpec((3 * QB, Q_PER_KV * QB), lambda b: (0, 0))],
        out_specs=pl.BlockSpec((QB, D_ATTN), lambda b: (b, 0)),
        out_shape=jax.ShapeDtypeStruct((T, D_ATTN), BF16),
        scratch_shapes=[pltpu.VMEM((D_ATTN, QB), F32)],
        compiler_params=_cp(("arbitrary",)),
        name="attention",
    )(sink, q, k, k, k, v, v, v, k, v, g_cols, band)


F1_J = 8
F2_K = 4


def _f1_kernel(u_ref, m1_ref, twr_ref, twi_ref, z_ref):
    g = jnp.dot(m1_ref[...], u_ref[...], preferred_element_type=F32)
    for jl in range(F1_J):
        twr = twr_ref[jl]
        twi = twi_ref[jl]
        for lt in range(D_F // 128):
            c0 = jl * D_F + lt * 128
            gr = g[0:N2, c0:c0 + 128]
            gi = g[N2:2 * N2, c0:c0 + 128]
            z_ref[jl, 0:N2, lt * 128:(lt + 1) * 128] = (gr * twr - gi * twi).astype(BF16)
            z_ref[jl, N2:2 * N2, lt * 128:(lt + 1) * 128] = (gr * twi + gi * twr).astype(BF16)


def _f2_kernel(zr_ref, zi_ref, m2_ref, y_ref):
    pr = jnp.dot(m2_ref[...], zr_ref[...], preferred_element_type=F32)
    pi = jnp.dot(m2_ref[...], zi_ref[...], preferred_element_type=F32)
    yr = pr[0:N1] + pi[N1:2 * N1]
    yi = pi[0:N1] - pr[N1:2 * N1]
    for kl in range(F2_K):
        y_ref[:, kl * 2 * D_F:kl * 2 * D_F + D_F] = yr[:, kl * D_F:(kl + 1) * D_F].astype(BF16)
        y_ref[:, kl * 2 * D_F + D_F:(kl + 1) * 2 * D_F] = yi[:, kl * D_F:(kl + 1) * D_F].astype(BF16)


def _f3_kernel(y_ref, uc_ref, cs_ref, mc_ref, wf_ref, gf_ref, o_ref):
    i = pl.program_id(0)

    def finish(parts, scale):
        f = jnp.concatenate(parts, axis=1) * scale
        t = jnp.dot(f.astype(BF16), wf_ref[...], preferred_element_type=F32)
        ms = jnp.mean(t * t, axis=-1, keepdims=True)
        o_ref[...] = (t * lax.rsqrt(ms + EPS) * gf_ref[...]).astype(BF16)

    @pl.when(i < NT_LAT)
    def _():
        parts = []
        for g in range(N_FG):
            lhs = jnp.concatenate([y_ref[:, g * 128:(g + 1) * 128],
                                   y_ref[:, D_F + g * 128:D_F + (g + 1) * 128]], axis=1)
            parts.append(jnp.dot(lhs, cs_ref[...], preferred_element_type=F32))
        finish(parts, float((SEQ * 128) ** -0.5))

    @pl.when(i == NT_LAT)
    def _():
        pq = jnp.dot(mc_ref[...], uc_ref[...], preferred_element_type=F32)
        parts = []
        for g in range(N_FG):
            lhs = jnp.concatenate([pq[0:CTX, g * 128:(g + 1) * 128],
                                   -pq[CTX:2 * CTX, g * 128:(g + 1) * 128]], axis=1)
            parts.append(jnp.dot(lhs.astype(BF16), cs_ref[...], preferred_element_type=F32))
        finish(parts, float((CTX * 128) ** -0.5))


def _fourier(u, w_f_bf, g_f, tb):
    u2 = u.reshape(T // N1, N1 * D_F)
    z = pl.pallas_call(
        _f1_kernel,
        grid=(N1 // F1_J,),
        in_specs=[pl.BlockSpec((N2, F1_J * D_F), lambda c: (0, c)),
                  pl.BlockSpec((2 * N2, N2), lambda c: (0, 0)),
                  pl.BlockSpec((F1_J, N2, 128), lambda c: (c, 0, 0)),
                  pl.BlockSpec((F1_J, N2, 128), lambda c: (c, 0, 0))],
        out_specs=pl.BlockSpec((F1_J, 2 * N2, D_F), lambda c: (c, 0, 0)),
        out_shape=jax.ShapeDtypeStruct((N1, 2 * N2, D_F), BF16),
        compiler_params=_cp(("arbitrary",)),
        name="fourier_s1",
    )(u2, tb["m1"], tb["twr"], tb["twi"])
    z2 = z.reshape(N1, 2 * N2 * D_F)
    nk = N2 // F2_K
    y = pl.pallas_call(
        _f2_kernel,
        grid=(nk,),
        in_specs=[pl.BlockSpec((N1, F2_K * D_F), lambda k: (0, k)),
                  pl.BlockSpec((N1, F2_K * D_F), lambda k: (0, nk + k)),
                  pl.BlockSpec((2 * N1, N1), lambda k: (0, 0))],
        out_specs=pl.BlockSpec((N1, F2_K * 2 * D_F), lambda k: (0, k)),
        out_shape=jax.ShapeDtypeStruct((N1, N2 * 2 * D_F), BF16),
        compiler_params=_cp(("arbitrary",)),
        name="fourier_s2",
    )(z2, z2, tb["m2"])
    y2 = y.reshape(SEQ, 2 * D_F)
    return pl.pallas_call(
        _f3_kernel,
        grid=(NT,),
        in_specs=[pl.BlockSpec((TM, 2 * D_F), lambda i: (jnp.minimum(i, NT_LAT - 1), 0)),
                  pl.BlockSpec((CTX, D_F), lambda i: (SEQ // CTX, 0)),
                  pl.BlockSpec((2 * 128, 128), lambda i: (0, 0)),
                  pl.BlockSpec((2 * CTX, CTX), lambda i: (0, 0)),
                  pl.BlockSpec((D_F, D_F), lambda i: (0, 0)),
                  _row_spec(D_F)],
        out_specs=pl.BlockSpec((TM, D_F), lambda i: (i, 0)),
        out_shape=jax.ShapeDtypeStruct((T, D_F), BF16),
        compiler_params=_cp(("arbitrary",)),
        name="fourier_s3",
    )(y2, u, tb["csch"], tb["mc"], w_f_bf, g_f)


def _merge_kernel(x_ref, c_ref, fn_ref, an_ref, wo_ref, ga1_ref, g2_ref, sc2_ref, sh2_ref, ga2_ref,
                  wsg_ref, wsu_ref, wsd_ref, wrh_ref, wrl_ref, br_ref, tri_ref,
                  xmid_ref, h2_ref, info_ref, wcol_ref, cnt_ref, carry_scr):
    i = pl.program_id(0)

    @pl.when(i == 0)
    def _():
        carry_scr[...] = jnp.zeros_like(carry_scr)

    m = (jnp.dot(fn_ref[...], wo_ref[0:D_F, :], preferred_element_type=F32)
         + jnp.dot(an_ref[...], wo_ref[D_F:D_MODEL, :], preferred_element_type=F32))
    x1 = jnp.where(i == NT_LAT, c_ref[...], x_ref[...]) + ga1_ref[...] * m
    ms = jnp.mean(x1 * x1, axis=-1, keepdims=True)
    h = x1 * lax.rsqrt(ms + EPS) * g2_ref[...]
    h = h * (1.0 + sc2_ref[...]) + sh2_ref[...]
    hb = h.astype(BF16)
    h2_ref[...] = hb

    a = jnp.dot(hb, wsg_ref[...], preferred_element_type=F32)
    u = jnp.dot(hb, wsu_ref[...], preferred_element_type=F32)
    act = (_silu(a) * u).astype(BF16)
    ys = jnp.dot(act, wsd_ref[...], preferred_element_type=F32)
    xmid_ref[...] = x1 + ga2_ref[...] * ys

    hl = (h - hb.astype(F32)).astype(BF16)
    dn = (((1,), (1,)), ((), ()))
    lg = (lax.dot_general(wrh_ref[...], hb, dn, preferred_element_type=F32)
          + (lax.dot_general(wrl_ref[...], hb, dn, preferred_element_type=F32)
             + lax.dot_general(wrh_ref[...], hl, dn, preferred_element_type=F32)))
    score = jax.nn.sigmoid(lg)
    sel = score + br_ref[...]
    s = [sel[jj * N_GROUPS:(jj + 1) * N_GROUPS] for jj in range(EPG)]
    sr = [score[jj * N_GROUPS:(jj + 1) * N_GROUPS] for jj in range(EPG)]
    hi01, lo01 = jnp.maximum(s[0], s[1]), jnp.minimum(s[0], s[1])
    hi23, lo23 = jnp.maximum(s[2], s[3]), jnp.minimum(s[2], s[3])
    gscore = jnp.maximum(hi01, hi23) + jnp.maximum(jnp.minimum(hi01, hi23), jnp.maximum(lo01, lo23))
    gi = lax.broadcasted_iota(I32, (N_GROUPS, TM), 0)
    gmax = jnp.max(gscore, axis=0, keepdims=True)
    g_idx = jnp.min(jnp.where(gscore == gmax, gi, N_GROUPS), axis=0, keepdims=True)
    gsel = gi == g_idx
    v = [jnp.sum(jnp.where(gsel, s[jj], 0.0), axis=0, keepdims=True) for jj in range(EPG)]
    vr = [jnp.sum(jnp.where(gsel, sr[jj], 0.0), axis=0, keepdims=True) for jj in range(EPG)]

    def first_argmax(vals):
        best = vals[0]
        idx = jnp.zeros((1, TM), I32)
        for jj in range(1, EPG):
            upd = vals[jj] > best
            best = jnp.where(upd, vals[jj], best)
            idx = jnp.where(upd, jj, idx)
        return idx

    i1 = first_argmax(v)
    i2 = first_argmax([jnp.where(i1 == jj, -jnp.inf, v[jj]) for jj in range(EPG)])

    def pick(vals, idx):
        out = vals[0]
        for jj in range(1, EPG):
            out = jnp.where(idx == jj, vals[jj], out)
        return out

    w1 = pick(vr, i1)
    w2 = pick(vr, i2)
    wsum = w1 + w2
    w1 = w1 / wsum
    w2 = w2 / wsum

    ri = lax.broadcasted_iota(I32, (N_EXPERTS, TM), 0)
    oh1 = ri == i1 * N_GROUPS + g_idx
    oh2 = ri == i2 * N_GROUPS + g_idx
    oh = jnp.logical_or(oh1, oh2).astype(F32)
    pre = jnp.dot(oh.astype(BF16), tri_ref[...], preferred_element_type=F32) + carry_scr[:, 0:1]
    r1 = jnp.sum(jnp.where(oh1, pre, 0.0), axis=0, keepdims=True)
    r2 = jnp.sum(jnp.where(oh2, pre, 0.0), axis=0, keepdims=True)
    carry = carry_scr[...] + jnp.sum(oh, axis=1, keepdims=True)
    carry_scr[...] = carry
    cnt_ref[...] = carry

    zi = jnp.zeros((4, TM), I32)
    info_ref[...] = jnp.concatenate(
        [g_idx * EPG + i1, g_idx * EPG + i2, r1.astype(I32), r2.astype(I32), zi], axis=0)
    wrow = jnp.concatenate([w1, w2, jnp.zeros((126, TM), F32)], axis=0)
    wcol_ref[...] = wrow.T


def _stream_specs(c_blk):
    return [pl.BlockSpec((TM, D_MODEL), lambda i: (jnp.minimum(i, NT_LAT - 1), 0)),
            pl.BlockSpec((CTX, D_MODEL), lambda i: (c_blk, 0))]


def _merge(xa, ca, c_blk, fn, an, w_out_bf, modt, g2, wsg, wsu, wsd, wrh, wrl, br, tb):
    tile = lambda n: pl.BlockSpec((TM, n), lambda i: (i, 0))
    full = lambda a, b: pl.BlockSpec((a, b), lambda i: (0, 0))
    return pl.pallas_call(
        _merge_kernel,
        grid=(NT,),
        in_specs=_stream_specs(c_blk) + [
                  tile(D_F), tile(D_ATTN), full(D_MODEL, D_MODEL),
                  _mod_spec(2), _row_spec(D_MODEL), _mod_spec(4), _mod_spec(3), _mod_spec(5),
                  full(D_MODEL, D_SHARED), full(D_MODEL, D_SHARED), full(D_SHARED, D_MODEL),
                  full(N_EXPERTS, D_MODEL), full(N_EXPERTS, D_MODEL), full(N_EXPERTS, TM),
                  full(TM, TM)],
        out_specs=[tile(D_MODEL), tile(D_MODEL),
                   pl.BlockSpec((8, TM), lambda i: (0, i)),
                   pl.BlockSpec((TM, 128), lambda i: (i, 0)),
                   pl.BlockSpec((N_EXPERTS, 128), lambda i: (0, 0))],
        out_shape=[jax.ShapeDtypeStruct((T, D_MODEL), F32), jax.ShapeDtypeStruct((T, D_MODEL), BF16),
                   jax.ShapeDtypeStruct((8, T), I32), jax.ShapeDtypeStruct((T, 128), F32),
                   jax.ShapeDtypeStruct((N_EXPERTS, 128), F32)],
        scratch_shapes=[pltpu.VMEM((N_EXPERTS, 128), F32)],
        compiler_params=_cp(("arbitrary",)),
        name="merge_route",
    )(xa, ca, fn, an, w_out_bf, modt, g2, modt, modt, modt, wsg, wsu, wsd, wrh, wrl, br, tb["tri"])


def _dispatch_kernel(dest_ref, pend_ref, padded_ref, h_ref, permt_ref, xb_hbm, zero_scr, slab, sem, zsem,
                     *, n):
    i = pl.program_id(0)
    slot = i % 2

    def zero_copy(row0):
        return pltpu.make_async_copy(zero_scr, xb_hbm.at[pl.ds(row0, BM)], zsem)

    @pl.when(i == 0)
    def _():
        zero_scr[...] = jnp.zeros_like(zero_scr)
        n_used = pend_ref[N_EXPERTS - 1] // BM
        for e in range(N_EXPERTS):
            @pl.when(padded_ref[e] > 0)
            def _():
                zero_copy(pend_ref[e] - BM).start()
        for blk in range(NB - N_EXPERTS, NB):
            @pl.when(blk >= n_used)
            def _():
                zero_copy(blk * BM).start()
        for e in range(N_EXPERTS):
            @pl.when(padded_ref[e] > 0)
            def _():
                zero_copy(0).wait()
        for blk in range(NB - N_EXPERTS, NB):
            @pl.when(blk >= n_used)
            def _():
                zero_copy(0).wait()

    half = D_MODEL // 2
    for grp in range(TM // GRP):
        rows = slice(grp * GRP, (grp + 1) * GRP)
        x = jnp.concatenate(
            [jnp.concatenate([h_ref[rows, s * PK:(s + 1) * PK],
                              h_ref[rows, half + s * PK:half + (s + 1) * PK]], axis=1)
             for s in range(SUB)], axis=0)
        bits = lax.bitcast_convert_type(
            jnp.dot(permt_ref[...], x, preferred_element_type=F32), I32)
        w = (bits[:, 0:PK] & -65536) | lax.shift_right_logical(bits[:, PK:2 * PK], 16)
        slab[slot, rows] = w.reshape(GRP, SUB, PK)

    def copy(sl, r, d):
        return pltpu.make_async_copy(slab.at[sl, pl.ds(r, 1)], xb_hbm.at[pl.ds(d, 1)], sem.at[sl])

    def drain(sl):
        def body(r, c):
            copy(sl, 0, 0).wait()
            copy(sl, 0, 0).wait()
            return c
        lax.fori_loop(0, TM, body, 0)

    @pl.when(i > 0)
    def _():
        drain(1 - slot)

    def issue(r, c):
        t = i * TM + r
        copy(slot, r, dest_ref[2 * t]).start()
        copy(slot, r, dest_ref[2 * t + 1]).start(priority=1)
        return c

    lax.fori_loop(0, TM, issue, 0)

    @pl.when(i == n - 1)
    def _():
        drain(slot)


def _dispatch(dest, pad_end, padded, h2, permt):
    return pl.pallas_call(
        functools.partial(_dispatch_kernel, n=NT),
        grid_spec=pltpu.PrefetchScalarGridSpec(
            num_scalar_prefetch=3, grid=(NT,),
            in_specs=[pl.BlockSpec((TM, D_MODEL), lambda i, d, pe, pd: (i, 0)),
                      pl.BlockSpec((GRP * SUB, GRP * SUB), lambda i, d, pe, pd: (0, 0))],
            out_specs=pl.BlockSpec(memory_space=pl.ANY),
            scratch_shapes=[pltpu.VMEM((BM, SUB, PK), I32), pltpu.VMEM((2, TM, SUB, PK), I32),
                            pltpu.SemaphoreType.DMA((2,)), pltpu.SemaphoreType.DMA(())]),
        out_shape=jax.ShapeDtypeStruct((ROWS, SUB, PK), I32),
        compiler_params=pltpu.CompilerParams(dimension_semantics=("arbitrary",),
                                             has_side_effects=True, vmem_limit_bytes=VMEM_LIMIT),
        name="dispatch",
    )(dest, pad_end, padded, h2, permt)


def _pack_bf16_pairs(h):
    bits = lax.bitcast_convert_type(h.astype(BF16).astype(F32), I32)
    half = h.shape[1] // 2
    return (bits[:, :half] & -65536) | lax.shift_right_logical(bits[:, half:], 16)


GRP = 32


def _expert_kernel(eo_ref, es_ref, n_ref, x_ref, perm_ref, wg_hbm, wu_hbm, wd_hbm, y_ref,
                   wgf, wuf, wdf, wgb, wub, wdb, xs_scr, wsem, *, layer):
    b = pl.program_id(0)
    k = eo_ref[b]
    first = jnp.logical_or(b == 0, k != eo_ref[jnp.maximum(b - 1, 0)])

    def copies(kk, slot):
        e = es_ref[kk]
        return (pltpu.make_async_copy(wg_hbm.at[layer, e], wgf.at[slot], wsem.at[slot, 0]),
                pltpu.make_async_copy(wu_hbm.at[layer, e], wuf.at[slot], wsem.at[slot, 1]),
                pltpu.make_async_copy(wd_hbm.at[layer, e], wdf.at[slot], wsem.at[slot, 2]))

    @pl.when(b == 0)
    def _():
        for c in copies(0, 0):
            c.start()

    @pl.when(jnp.logical_and(first, b < n_ref[0]))
    def _():
        slot = k % 2
        for c in copies(k, slot):
            c.wait()

        @pl.when(k + 1 < n_ref[1])
        def _():
            for c in copies(k + 1, 1 - slot):
                c.start()

        wgb[...] = wgf[slot].astype(BF16)
        wub[...] = wuf[slot].astype(BF16)
        wdb[...] = wdf[slot].astype(BF16)

    @pl.when(b < n_ref[0])
    def _():
        half = D_MODEL // 2
        for grp in range(BM // GRP):
            w = x_ref[grp * GRP:(grp + 1) * GRP].reshape(GRP * SUB, PK)
            hi = lax.bitcast_convert_type(w & -65536, F32).astype(BF16)
            lo = lax.bitcast_convert_type(w << 16, F32).astype(BF16)
            g = jnp.concatenate([hi, lo], axis=1)
            r = jnp.dot(perm_ref[...], g, preferred_element_type=F32).astype(BF16)
            rows = slice(grp * GRP, (grp + 1) * GRP)
            for s in range(SUB):
                xs_scr[rows, s * PK:(s + 1) * PK] = r[s * GRP:(s + 1) * GRP, 0:PK]
                xs_scr[rows, half + s * PK:half + (s + 1) * PK] = r[s * GRP:(s + 1) * GRP, PK:2 * PK]
        x = xs_scr[...]
        a = jnp.dot(x, wgb[...], preferred_element_type=F32)
        u = jnp.dot(x, wub[...], preferred_element_type=F32)
        act = (_silu(a) * u).astype(BF16)
        yp = _pack_bf16_pairs(jnp.dot(act, wdb[...], preferred_element_type=F32))
        for s in range(SUB):
            y_ref[:, s, :] = yp[:, s * PK:(s + 1) * PK]

    @pl.when(b >= n_ref[0])
    def _():
        y_ref[...] = jnp.zeros_like(y_ref)


def _experts(l, eo, es, nn, xb, perm, w_g, w_u, w_d):
    hbm = pl.BlockSpec(memory_space=pl.ANY)
    return pl.pallas_call(
        functools.partial(_expert_kernel, layer=l),
        grid_spec=pltpu.PrefetchScalarGridSpec(
            num_scalar_prefetch=3, grid=(NB,),
            in_specs=[pl.BlockSpec((BM, SUB, PK), lambda b, eo, es, n: (jnp.minimum(b, n[0] - 1), 0, 0)),
                      pl.BlockSpec((GRP * SUB, GRP * SUB), lambda b, eo, es, n: (0, 0)),
                      hbm, hbm, hbm],
            out_specs=pl.BlockSpec((BM, SUB, PK), lambda b, eo, es, n: (b, 0, 0)),
            scratch_shapes=[pltpu.VMEM((2, D_MODEL, D_EXPERT), F32),
                            pltpu.VMEM((2, D_MODEL, D_EXPERT), F32),
                            pltpu.VMEM((2, D_EXPERT, D_MODEL), F32),
                            pltpu.VMEM((D_MODEL, D_EXPERT), BF16),
                            pltpu.VMEM((D_MODEL, D_EXPERT), BF16),
                            pltpu.VMEM((D_EXPERT, D_MODEL), BF16),
                            pltpu.VMEM((BM, D_MODEL), BF16),
                            pltpu.SemaphoreType.DMA((2, 3))]),
        out_shape=jax.ShapeDtypeStruct((ROWS, SUB, PK), I32),
        compiler_params=_cp(("arbitrary",)),
        name="experts",
    )(eo, es, nn, xb, perm, w_g, w_u, w_d)


def _combine_kernel(dest_ref, y_hbm, xmid_ref, wcol_ref, ga2_ref, perm_ref, o_ref, b1, b2, sem, *, n):
    i = pl.program_id(0)

    def copy(d, buf, slot, r):
        return pltpu.make_async_copy(y_hbm.at[pl.ds(d, 1)], buf.at[slot, pl.ds(r, 1)], sem.at[slot])

    def issue(tile, slot):
        def body(r, c):
            t = tile * TM + r
            copy(dest_ref[2 * t], b1, slot, r).start()
            copy(dest_ref[2 * t + 1], b2, slot, r).start(priority=1)
            return c
        lax.fori_loop(0, TM, body, 0)

    @pl.when(i == 0)
    def _():
        issue(0, 0)

    @pl.when(i + 1 < n)
    def _():
        issue(i + 1, (i + 1) % 2)

    slot = i % 2

    def drain(r, c):
        copy(0, b1, slot, 0).wait()
        copy(0, b2, slot, 0).wait()
        return c

    lax.fori_loop(0, TM, drain, 0)

    def unpack_group(buf, grp):
        w = buf[slot, grp * GRP:(grp + 1) * GRP].reshape(GRP * SUB, PK)
        hi = lax.bitcast_convert_type(w & -65536, F32).astype(BF16)
        lo = lax.bitcast_convert_type(w << 16, F32).astype(BF16)
        return jnp.dot(perm_ref[...], jnp.concatenate([hi, lo], axis=1), preferred_element_type=F32)

    half = D_MODEL // 2
    for grp in range(TM // GRP):
        rows = slice(grp * GRP, (grp + 1) * GRP)
        w1 = wcol_ref[rows, 0:1]
        w2 = wcol_ref[rows, 1:2]
        r1 = unpack_group(b1, grp)
        r2 = unpack_group(b2, grp)
        for s in range(SUB):
            sr = slice(s * GRP, (s + 1) * GRP)
            for c0, lanes in ((s * PK, slice(0, PK)), (half + s * PK, slice(PK, 2 * PK))):
                cs = slice(c0, c0 + PK)
                o_ref[rows, cs] = xmid_ref[rows, cs] + ga2_ref[:, cs] * (w1 * r1[sr, lanes] + w2 * r2[sr, lanes])


def _combine(dest, yb, xmid, wcol, modt, perm, n_tiles):
    tile = lambda n: pl.BlockSpec((TM, n), lambda i, d: (i, 0))
    return pl.pallas_call(
        functools.partial(_combine_kernel, n=n_tiles),
        grid_spec=pltpu.PrefetchScalarGridSpec(
            num_scalar_prefetch=1, grid=(n_tiles,),
            in_specs=[pl.BlockSpec(memory_space=pl.ANY), tile(D_MODEL), tile(128),
                      pl.BlockSpec((None, None, 1, D_MODEL), lambda i, d: (i // NT_LAT, 5, 0, 0)),
                      pl.BlockSpec((GRP * SUB, GRP * SUB), lambda i, d: (0, 0))],
            out_specs=tile(D_MODEL),
            scratch_shapes=[pltpu.VMEM((2, TM, SUB, PK), I32), pltpu.VMEM((2, TM, SUB, PK), I32),
                            pltpu.SemaphoreType.DMA((2,))]),
        out_shape=jax.ShapeDtypeStruct((n_tiles * TM, D_MODEL), F32),
        compiler_params=_cp(("arbitrary",)),
        name="combine",
    )(dest, yb, xmid, wcol, modt, perm)


def _perm_experts(a):
    return a.reshape(a.shape[:-1] + (N_GROUPS, EPG)).swapaxes(-1, -2).reshape(a.shape)


def kernel(x, c, ctx, c_ctx, w_mod, b_mod, g_norm1, g_norm2, w_in, w_fourier, g_q, g_k, sink,
           g_branch_f, g_branch_a, w_out, w_router, b_router, w_exp_gate, w_exp_up, w_exp_down,
           w_sh_gate, w_sh_up, w_sh_down):
    assert x.shape == (1, SEQ, D_MODEL) and ctx.shape == (1, CTX, D_MODEL)
    tb = _tables()
    stream = (x[0], ctx[0], 0)

    cc = jnp.zeros((8, D_MODEL), F32).at[0].set(c[0]).at[1].set(c_ctx)
    mods = _modulation(cc, w_mod, b_mod)

    wr_t = _perm_experts(w_router).T
    wrh = wr_t.astype(BF16)
    wrl = (wr_t - wrh.astype(F32)).astype(BF16)
    br = jnp.broadcast_to(_perm_experts(b_router)[:, None], (N_EXPERTS, TM)).astype(F32)

    for l in range(DEPTH):
        last = l == DEPTH - 1
        modt = mods[l, 0:2].reshape(2, 6, 1, D_MODEL)
        row = lambda a: a[l].reshape(1, -1)
        u, q, k, v = _project(*stream, row(g_norm1), modt, w_in[l].astype(BF16), row(g_q), row(g_k), tb)
        an = _attention(q, k, v, sink[l], row(g_branch_a), tb["band"])
        fn = _fourier(u, w_fourier[l].astype(BF16), row(g_branch_f), tb)
        xmid, h2, info, wcol, cnt = _merge(
            *stream, fn, an, w_out[l].astype(BF16), modt, row(g_norm2),
            w_sh_gate[l].astype(BF16), w_sh_up[l].astype(BF16), w_sh_down[l].astype(BF16),
            wrh, wrl, br, tb)

        xs = _routed(l, info, cnt, h2, xmid, wcol, modt, w_exp_gate, w_exp_up, w_exp_down, tb,
                     NT_LAT if last else NT)
        stream = (xs, xs, NT_LAT)
    return xs[None]


def _routed(l, info, cnt, h2, xmid, wcol, modt, w_g, w_u, w_d, tb, n_tiles):
    counts = cnt[:, 0].astype(I32).reshape(EPG, N_GROUPS).T.reshape(N_EXPERTS)
    padded = ((counts + BM - 1) // BM) * BM
    pad_end = jnp.cumsum(padded).astype(I32)
    pad_start = pad_end - padded
    start = jnp.sum(jnp.where(info[0:2, :, None] == jnp.arange(N_EXPERTS, dtype=I32),
                              pad_start, 0), axis=-1)
    dest = (start + info[2:4]).T.reshape(2 * T)
    ids = jnp.arange(N_EXPERTS, dtype=I32)
    block_row0 = jnp.arange(NB, dtype=I32)[:, None] * BM
    block_e = jnp.minimum(jnp.sum((pad_end <= block_row0).astype(I32), axis=-1), N_EXPERTS - 1)
    used = padded > 0
    ordinal = jnp.cumsum(used.astype(I32)) - 1
    eo = jnp.sum(jnp.where(block_e[:, None] == ids, ordinal, 0), axis=-1)
    es = jnp.sum(jnp.where(jnp.logical_and(used, ordinal == ids[:, None]), ids, 0), axis=-1)
    nn = jnp.stack([pad_end[N_EXPERTS - 1] // BM, ordinal[N_EXPERTS - 1] + 1]).astype(I32)
    xb = _dispatch(dest, pad_end, padded, h2, tb["permt"])
    yb = _experts(l, eo.astype(I32), es.astype(I32), nn, xb, tb["perm"], w_g, w_u, w_d)
    return _combine(dest, yb, xmid, wcol, modt, tb["perm"], n_tiles)
```

```python
import functools

import numpy as np
import jax
import jax.numpy as jnp
from jax import lax
from jax.experimental import pallas as pl
from jax.experimental.pallas import tpu as pltpu

F32 = jnp.float32
BF16 = jnp.bfloat16
I32 = jnp.int32

D_MODEL = 2048
SEQ = 8192
CTX = 256
T = SEQ + CTX
DEPTH = 4
GRID_W = 64
HEAD_DIM = 128
D_F = 512
N_FG = 4
N_HEADS = 12
N_KV = 4
Q_PER_KV = 3
D_ATTN = N_HEADS * HEAD_DIM
D_KV = N_KV * HEAD_DIM
D_IN = D_F + D_ATTN + 2 * D_KV
ROPE_BASE = 10000.0
N_EXPERTS = 32
N_GROUPS = 8
EPG = 4
D_EXPERT = 512
D_SHARED = 512
EPS = 1e-6
NEG = -1e30
LOG2E = 1.4426950408889634
WINDOW = 128

TM = 256
NT = T // TM
NT_LAT = SEQ // TM
QB = 128
NQB = T // QB
NQB_LAT = SEQ // QB
BM = 256
NB = (2 * T + BM - 1) // BM + N_EXPERTS
ROWS = NB * BM
N1 = 128
N2 = 64
SUB = 8
RL = D_MODEL // SUB
PK = D_MODEL // 2 // SUB
VMEM_LIMIT = 56 * 1024 * 1024


def _cp(sem):
    return pltpu.CompilerParams(dimension_semantics=sem, vmem_limit_bytes=VMEM_LIMIT)


def _silu(a):
    return a * jax.nn.sigmoid(a)


@functools.lru_cache(maxsize=None)
def _tables():
    t = np.arange(SEQ)
    row = (t // GRID_W).astype(np.float64)
    col = (t % GRID_W).astype(np.float64)
    inv = ROPE_BASE ** (-np.arange(0, HEAD_DIM // 2, 2, dtype=np.float64) / (HEAD_DIM // 2))
    ar = row[:, None] * inv
    ac = col[:, None] * inv
    ang = np.concatenate([ar, ar, ac, ac], axis=-1)
    cos = np.concatenate([np.cos(ang), np.ones((CTX, HEAD_DIM))], axis=0)
    sin = np.concatenate([np.sin(ang), np.zeros((CTX, HEAD_DIM))], axis=0)
    first = (np.arange(HEAD_DIM) % 64) < 32
    rope_a = np.where(first[None, :], -sin, 0.0)
    rope_b = np.where(first[None, :], 0.0, sin)

    def cs(n):
        k = np.arange(n)
        a = 2.0 * np.pi * np.outer(k, k) / n
        return np.cos(a), np.sin(a)

    c64, s64 = cs(N2)
    c128, s128 = cs(N1)
    c256, s256 = cs(CTX)
    m1 = np.concatenate([c64, -s64], axis=0)
    m2 = np.concatenate([c128, s128], axis=0)
    mc = np.concatenate([c256, s256], axis=0)
    csch = np.concatenate([c128, s128], axis=0)
    a = 2.0 * np.pi * np.outer(np.arange(N1), np.arange(N2)) / SEQ
    twr = np.broadcast_to(np.cos(a)[:, :, None], (N1, N2, 128))
    twi = np.broadcast_to(-np.sin(a)[:, :, None], (N1, N2, 128))
    tri = (np.arange(TM)[:, None] < np.arange(TM)[None, :]).astype(np.float32)
    kpos = np.arange(3 * QB)[:, None] - QB
    qpos = (np.arange(Q_PER_KV * QB) % QB)[None, :]
    band = np.where(np.abs(qpos - kpos) <= WINDOW, 0.0, NEG)
    perm = np.zeros((GRP * SUB, GRP * SUB), np.float32)
    rr, ss = np.meshgrid(np.arange(GRP), np.arange(SUB), indexing="ij")
    perm[(ss * GRP + rr).ravel(), (rr * SUB + ss).ravel()] = 1.0
    return dict(
        cos=jnp.asarray(cos, F32), rope_a=jnp.asarray(rope_a, F32), rope_b=jnp.asarray(rope_b, F32),
        m1=jnp.asarray(m1, BF16), m2=jnp.asarray(m2, BF16), mc=jnp.asarray(mc, BF16),
        csch=jnp.asarray(csch, BF16), twr=jnp.asarray(twr, F32), twi=jnp.asarray(twi, F32),
        tri=jnp.asarray(tri, BF16), perm=jnp.asarray(perm, BF16), permt=jnp.asarray(perm.T, BF16),
        band=jnp.asarray(band, F32))


MOD_TN = 1024


def _mod_kernel(cc_ref, w_ref, b_ref, o_ref):
    a = _silu(cc_ref[...])
    o_ref[...] = jnp.dot(a.astype(BF16), w_ref[...].astype(BF16),
                         preferred_element_type=F32) + b_ref[...]


def _modulation(cc, w_mod, b_mod):
    nl = w_mod.shape[0]
    n6 = w_mod.shape[2]
    return pl.pallas_call(
        _mod_kernel,
        grid=(nl, n6 // MOD_TN),
        in_specs=[pl.BlockSpec((8, D_MODEL), lambda l, j: (0, 0)),
                  pl.BlockSpec((None, D_MODEL, MOD_TN), lambda l, j: (l, 0, j)),
                  pl.BlockSpec((None, 1, MOD_TN), lambda l, j: (l, 0, j))],
        out_specs=pl.BlockSpec((None, 8, MOD_TN), lambda l, j: (l, 0, j)),
        out_shape=jax.ShapeDtypeStruct((nl, 8, n6), F32),
        compiler_params=_cp(("arbitrary", "arbitrary")),
        name="modulation",
    )(cc, w_mod, b_mod.reshape(nl, 1, n6))


def _mod_spec(chunk):
    return pl.BlockSpec((None, None, 1, D_MODEL), lambda i: (i // NT_LAT, chunk, 0, 0))


def _row_spec(n):
    return pl.BlockSpec((1, n), lambda i: (0, 0))


PCW = 256


def _proj_kernel(x_ref, c_ref, g_ref, sc_ref, sh_ref, w_ref, gq_ref, gk_ref, cos_ref, ra_ref, rb_ref,
                 u_ref, q_ref, k_ref, v_ref, hb_ref, *, split):
    x = _stream_tile(x_ref, c_ref, split)
    ms = jnp.mean(x * x, axis=-1, keepdims=True)
    h = x * lax.rsqrt(ms + EPS) * g_ref[...]
    h = h * (1.0 + sc_ref[...]) + sh_ref[...]
    hb_ref[...] = h.astype(BF16)
    cos = cos_ref[...]
    ra = ra_ref[...]
    rb = rb_ref[...]

    def head(t, g):
        m = jnp.mean(t * t, axis=-1, keepdims=True)
        t = t * lax.rsqrt(m + EPS) * g
        return t * cos + pltpu.roll(t, 96, 1) * ra + pltpu.roll(t, 32, 1) * rb

    scale = HEAD_DIM ** -0.5 * LOG2E
    for c in range(D_IN // PCW):
        col = c * PCW
        p = jnp.dot(hb_ref[...], w_ref[:, col:col + PCW], preferred_element_type=F32)
        if col < D_F:
            u_ref[:, col:col + PCW] = p.astype(BF16)
        elif col < D_F + D_ATTN:
            o = col - D_F
            for j in range(PCW // HEAD_DIM):
                t = head(p[:, j * HEAD_DIM:(j + 1) * HEAD_DIM], gq_ref[...]) * scale
                q_ref[:, o + j * HEAD_DIM:o + (j + 1) * HEAD_DIM] = t.astype(BF16)
        elif col < D_F + D_ATTN + D_KV:
            o = col - D_F - D_ATTN
            for j in range(PCW // HEAD_DIM):
                t = head(p[:, j * HEAD_DIM:(j + 1) * HEAD_DIM], gk_ref[...])
                k_ref[:, o + j * HEAD_DIM:o + (j + 1) * HEAD_DIM] = t.astype(BF16)
        else:
            o = col - D_F - D_ATTN - D_KV
            v_ref[:, o:o + PCW] = p.astype(BF16)


def _project(xa, ca, c_blk, g1, modt, w_in_bf, gq, gk, tb):
    tile = lambda n: pl.BlockSpec((TM, n), lambda i: (i, 0))
    return pl.pallas_call(
        functools.partial(_proj_kernel, split=xa is not ca),
        grid=(NT,),
        in_specs=_stream_specs(xa is not ca, c_blk) + [
                  _row_spec(D_MODEL), _mod_spec(1), _mod_spec(0),
                  pl.BlockSpec((D_MODEL, D_IN), lambda i: (0, 0)),
                  _row_spec(HEAD_DIM), _row_spec(HEAD_DIM),
                  tile(HEAD_DIM), tile(HEAD_DIM), tile(HEAD_DIM)],
        out_specs=[tile(D_F), tile(D_ATTN), tile(D_KV), tile(D_KV)],
        out_shape=[jax.ShapeDtypeStruct((T, D_F), BF16), jax.ShapeDtypeStruct((T, D_ATTN), BF16),
                   jax.ShapeDtypeStruct((T, D_KV), BF16), jax.ShapeDtypeStruct((T, D_KV), BF16)],
        scratch_shapes=[pltpu.VMEM((TM, D_MODEL), BF16)],
        compiler_params=_cp(("arbitrary",)),
        name="norm_proj",
    )(xa, ca, g1, modt, modt, w_in_bf, gq, gk, tb["cos"], tb["rope_a"], tb["rope_b"])


def _attn_kernel(sink_ref, q_ref, kp_ref, kc_ref, kn_ref, vp_ref, vc_ref, vn_ref, kx_ref, vx_ref,
                 ga_ref, band_ref, o_ref, a_scr):
    b = pl.program_id(0)
    lat = b < NQB_LAT
    off = lambda ok: jnp.where(ok, 0.0, NEG)
    bias = jnp.concatenate(
        [band_ref[0:QB] + off(jnp.logical_and(lat, b > 0)),
         band_ref[QB:2 * QB] + off(lat),
         band_ref[2 * QB:3 * QB] + off(b < NQB_LAT - 1)], axis=0)
    dn = (((1,), (1,)), ((), ()))
    for hk in range(N_KV):
        ks = slice(hk * HEAD_DIM, (hk + 1) * HEAD_DIM)
        qs = jnp.concatenate(
            [q_ref[:, (hk * Q_PER_KV + g) * HEAD_DIM:(hk * Q_PER_KV + g + 1) * HEAD_DIM]
             for g in range(Q_PER_KV)], axis=0)
        kw = jnp.concatenate([kp_ref[:, ks], kc_ref[:, ks], kn_ref[:, ks]], axis=0)
        vw = jnp.concatenate([vp_ref[:, ks], vc_ref[:, ks], vn_ref[:, ks]], axis=0)
        st = lax.dot_general(kw, qs, dn, preferred_element_type=F32) + bias
        sxt = lax.dot_general(kx_ref[:, ks], qs, dn, preferred_element_type=F32)
        sk = jnp.concatenate(
            [jnp.full((1, QB), sink_ref[hk * Q_PER_KV + g] * LOG2E, F32) for g in range(Q_PER_KV)],
            axis=1)
        m = jnp.maximum(jnp.maximum(jnp.max(st, axis=0, keepdims=True),
                                    jnp.max(sxt, axis=0, keepdims=True)), sk)
        e = jnp.exp2(st - m)
        ex = jnp.exp2(sxt - m)
        den = (jnp.sum(e, axis=0, keepdims=True) + jnp.sum(ex, axis=0, keepdims=True)
               + jnp.exp2(sk - m))
        dt = (((0,), (0,)), ((), ()))
        ot = (lax.dot_general(vw, e.astype(BF16), dt, preferred_element_type=F32)
              + lax.dot_general(vx_ref[:, ks], ex.astype(BF16), dt, preferred_element_type=F32)) / den
        for g in range(Q_PER_KV):
            h = hk * Q_PER_KV + g
            a_scr[h * HEAD_DIM:(h + 1) * HEAD_DIM, :] = ot[:, g * QB:(g + 1) * QB]
    at = a_scr[...]
    ms = jnp.mean(at * at, axis=0, keepdims=True)
    o_ref[...] = (at * lax.rsqrt(ms + EPS) * ga_ref[...]).T.astype(BF16)


def _attention(q, k, v, sink, g_a, band):
    kb = lambda f: pl.BlockSpec((QB, D_KV), lambda b: (f(b), 0))
    prev = lambda b: jnp.maximum(b - 1, 0)
    cur = lambda b: b
    nxt = lambda b: jnp.minimum(b + 1, NQB - 1)
    g_cols = jnp.broadcast_to(g_a.reshape(D_ATTN, 1), (D_ATTN, QB))
    return pl.pallas_call(
        _attn_kernel,
        grid=(NQB,),
        in_specs=[pl.BlockSpec(memory_space=pltpu.SMEM),
                  pl.BlockSpec((QB, D_ATTN), lambda b: (b, 0)),
                  kb(prev), kb(cur), kb(nxt), kb(prev), kb(cur), kb(nxt),
                  pl.BlockSpec((CTX, D_KV), lambda b: (SEQ // CTX, 0)),
                  pl.BlockSpec((CTX, D_KV), lambda b: (SEQ // CTX, 0)),
                  pl.BlockSpec((D_ATTN, QB), lambda b: (0, 0)),
                  pl.BlockSpec((3 * QB, Q_PER_KV * QB), lambda b: (0, 0))],
        out_specs=pl.BlockSpec((QB, D_ATTN), lambda b: (b, 0)),
        out_shape=jax.ShapeDtypeStruct((T, D_ATTN), BF16),
        scratch_shapes=[pltpu.VMEM((D_ATTN, QB), F32)],
        compiler_params=_cp(("arbitrary",)),
        name="attention",
    )(sink, q, k, k, k, v, v, v, k, v, g_cols, band)


F1_J = 8
F2_K = 4


def _f1_kernel(u_ref, m1_ref, twr_ref, twi_ref, z_ref):
    g = jnp.dot(m1_ref[...], u_ref[...], preferred_element_type=F32)
    for jl in range(F1_J):
        twr = twr_ref[jl]
        twi = twi_ref[jl]
        for lt in range(D_F // 128):
            c0 = jl * D_F + lt * 128
            gr = g[0:N2, c0:c0 + 128]
            gi = g[N2:2 * N2, c0:c0 + 128]
            z_ref[jl, 0:N2, lt * 128:(lt + 1) * 128] = (gr * twr - gi * twi).astype(BF16)
            z_ref[jl, N2:2 * N2, lt * 128:(lt + 1) * 128] = (gr * twi + gi * twr).astype(BF16)


def _f2_kernel(zr_ref, zi_ref, m2_ref, y_ref):
    pr = jnp.dot(m2_ref[...], zr_ref[...], preferred_element_type=F32)
    pi = jnp.dot(m2_ref[...], zi_ref[...], preferred_element_type=F32)
    yr = pr[0:N1] + pi[N1:2 * N1]
    yi = pi[0:N1] - pr[N1:2 * N1]
    for kl in range(F2_K):
        y_ref[:, kl * 2 * D_F:kl * 2 * D_F + D_F] = yr[:, kl * D_F:(kl + 1) * D_F].astype(BF16)
        y_ref[:, kl * 2 * D_F + D_F:(kl + 1) * 2 * D_F] = yi[:, kl * D_F:(kl + 1) * D_F].astype(BF16)


def _f3_kernel(y_ref, uc_ref, cs_ref, mc_ref, wf_ref, gf_ref, o_ref):
    i = pl.program_id(0)

    def finish(parts, scale):
        f = jnp.concatenate(parts, axis=1) * scale
        t = jnp.dot(f.astype(BF16), wf_ref[...], preferred_element_type=F32)
        ms = jnp.mean(t * t, axis=-1, keepdims=True)
        o_ref[...] = (t * lax.rsqrt(ms + EPS) * gf_ref[...]).astype(BF16)

    @pl.when(i < NT_LAT)
    def _():
        parts = []
        for g in range(N_FG):
            lhs = jnp.concatenate([y_ref[:, g * 128:(g + 1) * 128],
                                   y_ref[:, D_F + g * 128:D_F + (g + 1) * 128]], axis=1)
            parts.append(jnp.dot(lhs, cs_ref[...], preferred_element_type=F32))
        finish(parts, float((SEQ * 128) ** -0.5))

    @pl.when(i == NT_LAT)
    def _():
        pq = jnp.dot(mc_ref[...], uc_ref[...], preferred_element_type=F32)
        parts = []
        for g in range(N_FG):
            lhs = jnp.concatenate([pq[0:CTX, g * 128:(g + 1) * 128],
                                   -pq[CTX:2 * CTX, g * 128:(g + 1) * 128]], axis=1)
            parts.append(jnp.dot(lhs.astype(BF16), cs_ref[...], preferred_element_type=F32))
        finish(parts, float((CTX * 128) ** -0.5))


def _fourier(u, w_f_bf, g_f, tb):
    u2 = u.reshape(T // N1, N1 * D_F)
    z = pl.pallas_call(
        _f1_kernel,
        grid=(N1 // F1_J,),
        in_specs=[pl.BlockSpec((N2, F1_J * D_F), lambda c: (0, c)),
                  pl.BlockSpec((2 * N2, N2), lambda c: (0, 0)),
                  pl.BlockSpec((F1_J, N2, 128), lambda c: (c, 0, 0)),
                  pl.BlockSpec((F1_J, N2, 128), lambda c: (c, 0, 0))],
        out_specs=pl.BlockSpec((F1_J, 2 * N2, D_F), lambda c: (c, 0, 0)),
        out_shape=jax.ShapeDtypeStruct((N1, 2 * N2, D_F), BF16),
        compiler_params=_cp(("arbitrary",)),
        name="fourier_s1",
    )(u2, tb["m1"], tb["twr"], tb["twi"])
    z2 = z.reshape(N1, 2 * N2 * D_F)
    nk = N2 // F2_K
    y = pl.pallas_call(
        _f2_kernel,
        grid=(nk,),
        in_specs=[pl.BlockSpec((N1, F2_K * D_F), lambda k: (0, k)),
                  pl.BlockSpec((N1, F2_K * D_F), lambda k: (0, nk + k)),
                  pl.BlockSpec((2 * N1, N1), lambda k: (0, 0))],
        out_specs=pl.BlockSpec((N1, F2_K * 2 * D_F), lambda k: (0, k)),
        out_shape=jax.ShapeDtypeStruct((N1, N2 * 2 * D_F), BF16),
        compiler_params=_cp(("arbitrary",)),
        name="fourier_s2",
    )(z2, z2, tb["m2"])
    y2 = y.reshape(SEQ, 2 * D_F)
    return pl.pallas_call(
        _f3_kernel,
        grid=(NT,),
        in_specs=[pl.BlockSpec((TM, 2 * D_F), lambda i: (jnp.minimum(i, NT_LAT - 1), 0)),
                  pl.BlockSpec((CTX, D_F), lambda i: (SEQ // CTX, 0)),
                  pl.BlockSpec((2 * 128, 128), lambda i: (0, 0)),
                  pl.BlockSpec((2 * CTX, CTX), lambda i: (0, 0)),
                  pl.BlockSpec((D_F, D_F), lambda i: (0, 0)),
                  _row_spec(D_F)],
        out_specs=pl.BlockSpec((TM, D_F), lambda i: (i, 0)),
        out_shape=jax.ShapeDtypeStruct((T, D_F), BF16),
        compiler_params=_cp(("arbitrary",)),
        name="fourier_s3",
    )(y2, u, tb["csch"], tb["mc"], w_f_bf, g_f)


def _merge_kernel(x_ref, c_ref, fn_ref, an_ref, wo_ref, ga1_ref, g2_ref, sc2_ref, sh2_ref, ga2_ref,
                  wsg_ref, wsu_ref, wsd_ref, wrh_ref, wrl_ref, br_ref, tri_ref,
                  xmid_ref, h2_ref, info_ref, wcol_ref, cnt_ref, carry_scr, *, split):
    i = pl.program_id(0)

    @pl.when(i == 0)
    def _():
        carry_scr[...] = jnp.zeros_like(carry_scr)

    m = (jnp.dot(fn_ref[...], wo_ref[0:D_F, :], preferred_element_type=F32)
         + jnp.dot(an_ref[...], wo_ref[D_F:D_MODEL, :], preferred_element_type=F32))
    x1 = _stream_tile(x_ref, c_ref, split) + ga1_ref[...] * m
    ms = jnp.mean(x1 * x1, axis=-1, keepdims=True)
    h = x1 * lax.rsqrt(ms + EPS) * g2_ref[...]
    h = h * (1.0 + sc2_ref[...]) + sh2_ref[...]
    hb = h.astype(BF16)
    h2_ref[...] = hb

    a = jnp.dot(hb, wsg_ref[...], preferred_element_type=F32)
    u = jnp.dot(hb, wsu_ref[...], preferred_element_type=F32)
    act = (_silu(a) * u).astype(BF16)
    ys = jnp.dot(act, wsd_ref[...], preferred_element_type=F32)
    xmid_ref[...] = x1 + ga2_ref[...] * ys

    hl = (h - hb.astype(F32)).astype(BF16)
    dn = (((1,), (1,)), ((), ()))
    both = lax.dot_general(jnp.concatenate([wrh_ref[...], wrl_ref[...]], axis=0), hb, dn,
                           preferred_element_type=F32)
    lg = both[0:N_EXPERTS] + (both[N_EXPERTS:2 * N_EXPERTS]
                              + lax.dot_general(wrh_ref[...], hl, dn, preferred_element_type=F32))
    score = jax.nn.sigmoid(lg)
    sel = score + br_ref[...]
    s = [sel[jj * N_GROUPS:(jj + 1) * N_GROUPS] for jj in range(EPG)]
    sr = [score[jj * N_GROUPS:(jj + 1) * N_GROUPS] for jj in range(EPG)]
    hi01, lo01 = jnp.maximum(s[0], s[1]), jnp.minimum(s[0], s[1])
    hi23, lo23 = jnp.maximum(s[2], s[3]), jnp.minimum(s[2], s[3])
    gscore = jnp.maximum(hi01, hi23) + jnp.maximum(jnp.minimum(hi01, hi23), jnp.maximum(lo01, lo23))
    gi = lax.broadcasted_iota(I32, (N_GROUPS, TM), 0)
    gmax = jnp.max(gscore, axis=0, keepdims=True)
    g_idx = jnp.min(jnp.where(gscore == gmax, gi, N_GROUPS), axis=0, keepdims=True)
    gsel = gi == g_idx
    v = [jnp.sum(jnp.where(gsel, s[jj], 0.0), axis=0, keepdims=True) for jj in range(EPG)]
    vr = [jnp.sum(jnp.where(gsel, sr[jj], 0.0), axis=0, keepdims=True) for jj in range(EPG)]

    def first_argmax(vals):
        best = vals[0]
        idx = jnp.zeros((1, TM), I32)
        for jj in range(1, EPG):
            upd = vals[jj] > best
            best = jnp.where(upd, vals[jj], best)
            idx = jnp.where(upd, jj, idx)
        return idx

    i1 = first_argmax(v)
    i2 = first_argmax([jnp.where(i1 == jj, -jnp.inf, v[jj]) for jj in range(EPG)])

    def pick(vals, idx):
        out = vals[0]
        for jj in range(1, EPG):
            out = jnp.where(idx == jj, vals[jj], out)
        return out

    w1 = pick(vr, i1)
    w2 = pick(vr, i2)
    wsum = w1 + w2
    w1 = w1 / wsum
    w2 = w2 / wsum

    ri = lax.broadcasted_iota(I32, (N_EXPERTS, TM), 0)
    oh1 = ri == i1 * N_GROUPS + g_idx
    oh2 = ri == i2 * N_GROUPS + g_idx
    oh = jnp.logical_or(oh1, oh2).astype(F32)
    pre = jnp.dot(oh.astype(BF16), tri_ref[...], preferred_element_type=F32) + carry_scr[:, 0:1]
    r1 = jnp.sum(jnp.where(oh1, pre, 0.0), axis=0, keepdims=True)
    r2 = jnp.sum(jnp.where(oh2, pre, 0.0), axis=0, keepdims=True)
    carry = carry_scr[...] + jnp.sum(oh, axis=1, keepdims=True)
    carry_scr[...] = carry
    cnt_ref[...] = carry

    zi = jnp.zeros((4, TM), I32)
    info_ref[...] = jnp.concatenate(
        [g_idx * EPG + i1, g_idx * EPG + i2, r1.astype(I32), r2.astype(I32), zi], axis=0)
    wrow = jnp.concatenate([w1, w2, jnp.zeros((126, TM), F32)], axis=0)
    wcol_ref[...] = wrow.T


def _stream_specs(split, c_blk):
    if split:
        return [pl.BlockSpec((TM, D_MODEL), lambda i: (jnp.minimum(i, NT_LAT - 1), 0)),
                pl.BlockSpec((CTX, D_MODEL), lambda i: (c_blk, 0))]
    return [pl.BlockSpec((TM, D_MODEL), lambda i: (i, 0)),
            pl.BlockSpec((8, D_MODEL), lambda i: (0, 0))]


def _stream_tile(x_ref, c_ref, split):
    if split:
        return jnp.where(pl.program_id(0) == NT_LAT, c_ref[...], x_ref[...])
    return x_ref[...]


def _merge(xa, ca, c_blk, fn, an, w_out_bf, modt, g2, wsg, wsu, wsd, wrh, wrl, br, tb):
    tile = lambda n: pl.BlockSpec((TM, n), lambda i: (i, 0))
    full = lambda a, b: pl.BlockSpec((a, b), lambda i: (0, 0))
    return pl.pallas_call(
        functools.partial(_merge_kernel, split=xa is not ca),
        grid=(NT,),
        in_specs=_stream_specs(xa is not ca, c_blk) + [
                  tile(D_F), tile(D_ATTN), full(D_MODEL, D_MODEL),
                  _mod_spec(2), _row_spec(D_MODEL), _mod_spec(4), _mod_spec(3), _mod_spec(5),
                  full(D_MODEL, D_SHARED), full(D_MODEL, D_SHARED), full(D_SHARED, D_MODEL),
                  full(N_EXPERTS, D_MODEL), full(N_EXPERTS, D_MODEL), full(N_EXPERTS, TM),
                  full(TM, TM)],
        out_specs=[tile(D_MODEL), tile(D_MODEL),
                   pl.BlockSpec((8, TM), lambda i: (0, i)),
                   pl.BlockSpec((TM, 128), lambda i: (i, 0)),
                   pl.BlockSpec((N_EXPERTS, 128), lambda i: (0, 0))],
        out_shape=[jax.ShapeDtypeStruct((T, D_MODEL), F32), jax.ShapeDtypeStruct((T, D_MODEL), BF16),
                   jax.ShapeDtypeStruct((8, T), I32), jax.ShapeDtypeStruct((T, 128), F32),
                   jax.ShapeDtypeStruct((N_EXPERTS, 128), F32)],
        scratch_shapes=[pltpu.VMEM((N_EXPERTS, 128), F32)],
        compiler_params=_cp(("arbitrary",)),
        name="merge_route",
    )(xa, ca, fn, an, w_out_bf, modt, g2, modt, modt, modt, wsg, wsu, wsd, wrh, wrl, br, tb["tri"])


def _dispatch_kernel(dest_ref, pend_ref, padded_ref, h_ref, permt_ref, xb_hbm, zero_scr, slab, sem, zsem,
                     *, n):
    i = pl.program_id(0)
    slot = i % 2

    def zero_copy(row0):
        return pltpu.make_async_copy(zero_scr, xb_hbm.at[pl.ds(row0, BM)], zsem)

    @pl.when(i == 0)
    def _():
        zero_scr[...] = jnp.zeros_like(zero_scr)
        n_used = pend_ref[N_EXPERTS - 1] // BM
        for e in range(N_EXPERTS):
            @pl.when(padded_ref[e] > 0)
            def _():
                zero_copy(pend_ref[e] - BM).start()
        for blk in range(NB - N_EXPERTS, NB):
            @pl.when(blk >= n_used)
            def _():
                zero_copy(blk * BM).start()
        for e in range(N_EXPERTS):
            @pl.when(padded_ref[e] > 0)
            def _():
                zero_copy(0).wait()
        for blk in range(NB - N_EXPERTS, NB):
            @pl.when(blk >= n_used)
            def _():
                zero_copy(0).wait()

    half = D_MODEL // 2
    for grp in range(TM // GRP):
        rows = slice(grp * GRP, (grp + 1) * GRP)
        x = jnp.concatenate(
            [jnp.concatenate([h_ref[rows, s * PK:(s + 1) * PK],
                              h_ref[rows, half + s * PK:half + (s + 1) * PK]], axis=1)
             for s in range(SUB)], axis=0)
        bits = lax.bitcast_convert_type(
            jnp.dot(permt_ref[...], x, preferred_element_type=F32), I32)
        w = (bits[:, 0:PK] & -65536) | lax.shift_right_logical(bits[:, PK:2 * PK], 16)
        slab[slot, rows] = w.reshape(GRP, SUB, PK)

    def copy(sl, r, d):
        return pltpu.make_async_copy(slab.at[sl, pl.ds(r, 1)], xb_hbm.at[pl.ds(d, 1)], sem.at[sl])

    def drain(sl):
        def body(r, c):
            copy(sl, 0, 0).wait()
            copy(sl, 0, 0).wait()
            return c
        lax.fori_loop(0, TM, body, 0)

    def issue(r, c):
        t = i * TM + r
        copy(slot, r, dest_ref[2 * t]).start()
        copy(slot, r, dest_ref[2 * t + 1]).start(priority=1)
        return c

    lax.fori_loop(0, TM, issue, 0)

    @pl.when(i > 0)
    def _():
        drain(1 - slot)

    @pl.when(i == n - 1)
    def _():
        drain(slot)


def _dispatch(dest, pad_end, padded, h2, permt):
    return pl.pallas_call(
        functools.partial(_dispatch_kernel, n=NT),
        grid_spec=pltpu.PrefetchScalarGridSpec(
            num_scalar_prefetch=3, grid=(NT,),
            in_specs=[pl.BlockSpec((TM, D_MODEL), lambda i, d, pe, pd: (i, 0)),
                      pl.BlockSpec((GRP * SUB, GRP * SUB), lambda i, d, pe, pd: (0, 0))],
            out_specs=pl.BlockSpec(memory_space=pl.ANY),
            scratch_shapes=[pltpu.VMEM((BM, SUB, PK), I32), pltpu.VMEM((2, TM, SUB, PK), I32),
                            pltpu.SemaphoreType.DMA((2,)), pltpu.SemaphoreType.DMA(())]),
        out_shape=jax.ShapeDtypeStruct((ROWS, SUB, PK), I32),
        compiler_params=pltpu.CompilerParams(dimension_semantics=("arbitrary",),
                                             has_side_effects=True, vmem_limit_bytes=VMEM_LIMIT),
        name="dispatch",
    )(dest, pad_end, padded, h2, permt)


def _pack_bf16_pairs(h):
    bits = lax.bitcast_convert_type(h.astype(BF16).astype(F32), I32)
    half = h.shape[1] // 2
    return (bits[:, :half] & -65536) | lax.shift_right_logical(bits[:, half:], 16)


GRP = 32


def _expert_kernel(eo_ref, es_ref, n_ref, x_ref, perm_ref, wg_hbm, wu_hbm, wd_hbm, y_ref,
                   wgf, wuf, wdf, wgb, wub, wdb, xs_scr, wsem, *, layer):
    b = pl.program_id(0)
    k = eo_ref[b]
    first = jnp.logical_or(b == 0, k != eo_ref[jnp.maximum(b - 1, 0)])

    def copies(kk, slot):
        e = es_ref[kk]
        return (pltpu.make_async_copy(wg_hbm.at[layer, e], wgf.at[slot], wsem.at[slot, 0]),
                pltpu.make_async_copy(wu_hbm.at[layer, e], wuf.at[slot], wsem.at[slot, 1]),
                pltpu.make_async_copy(wd_hbm.at[layer, e], wdf.at[slot], wsem.at[slot, 2]))

    @pl.when(b == 0)
    def _():
        for c in copies(0, 0):
            c.start()

    @pl.when(jnp.logical_and(first, b < n_ref[0]))
    def _():
        slot = k % 2
        for c in copies(k, slot):
            c.wait()

        @pl.when(k + 1 < n_ref[1])
        def _():
            for c in copies(k + 1, 1 - slot):
                c.start()

        wgb[...] = wgf[slot].astype(BF16)
        wub[...] = wuf[slot].astype(BF16)
        wdb[...] = wdf[slot].astype(BF16)

    @pl.when(b < n_ref[0])
    def _():
        half = D_MODEL // 2
        for grp in range(BM // GRP):
            w = x_ref[grp * GRP:(grp + 1) * GRP].reshape(GRP * SUB, PK)
            hi = lax.bitcast_convert_type(w & -65536, F32).astype(BF16)
            lo = lax.bitcast_convert_type(w << 16, F32).astype(BF16)
            g = jnp.concatenate([hi, lo], axis=1)
            r = jnp.dot(perm_ref[...], g, preferred_element_type=F32).astype(BF16)
            rows = slice(grp * GRP, (grp + 1) * GRP)
            for s in range(SUB):
                xs_scr[rows, s * PK:(s + 1) * PK] = r[s * GRP:(s + 1) * GRP, 0:PK]
                xs_scr[rows, half + s * PK:half + (s + 1) * PK] = r[s * GRP:(s + 1) * GRP, PK:2 * PK]
        x = xs_scr[...]
        a = jnp.dot(x, wgb[...], preferred_element_type=F32)
        u = jnp.dot(x, wub[...], preferred_element_type=F32)
        act = (_silu(a) * u).astype(BF16)
        yp = _pack_bf16_pairs(jnp.dot(act, wdb[...], preferred_element_type=F32))
        for s in range(SUB):
            y_ref[:, s, :] = yp[:, s * PK:(s + 1) * PK]

    @pl.when(b >= n_ref[0])
    def _():
        y_ref[...] = jnp.zeros_like(y_ref)


def _experts(l, eo, es, nn, xb, perm, w_g, w_u, w_d):
    hbm = pl.BlockSpec(memory_space=pl.ANY)
    return pl.pallas_call(
        functools.partial(_expert_kernel, layer=l),
        grid_spec=pltpu.PrefetchScalarGridSpec(
            num_scalar_prefetch=3, grid=(NB,),
            in_specs=[pl.BlockSpec((BM, SUB, PK), lambda b, eo, es, n: (jnp.minimum(b, n[0] - 1), 0, 0)),
                      pl.BlockSpec((GRP * SUB, GRP * SUB), lambda b, eo, es, n: (0, 0)),
                      hbm, hbm, hbm],
            out_specs=pl.BlockSpec((BM, SUB, PK), lambda b, eo, es, n: (b, 0, 0)),
            scratch_shapes=[pltpu.VMEM((2, D_MODEL, D_EXPERT), F32),
                            pltpu.VMEM((2, D_MODEL, D_EXPERT), F32),
                            pltpu.VMEM((2, D_EXPERT, D_MODEL), F32),
                            pltpu.VMEM((D_MODEL, D_EXPERT), BF16),
                            pltpu.VMEM((D_MODEL, D_EXPERT), BF16),
                            pltpu.VMEM((D_EXPERT, D_MODEL), BF16),
                            pltpu.VMEM((BM, D_MODEL), BF16),
                            pltpu.SemaphoreType.DMA((2, 3))]),
        out_shape=jax.ShapeDtypeStruct((ROWS, SUB, PK), I32),
        compiler_params=_cp(("arbitrary",)),
        name="experts",
    )(eo, es, nn, xb, perm, w_g, w_u, w_d)


def _combine_kernel(dest_ref, y_hbm, xmid_ref, wcol_ref, ga2_ref, perm_ref, o_ref, b1, b2, sem, *, n):
    i = pl.program_id(0)

    def copy(d, buf, slot, r):
        return pltpu.make_async_copy(y_hbm.at[pl.ds(d, 1)], buf.at[slot, pl.ds(r, 1)], sem.at[slot])

    def issue(tile, slot):
        def body(r, c):
            t = tile * TM + r
            copy(dest_ref[2 * t], b1, slot, r).start()
            copy(dest_ref[2 * t + 1], b2, slot, r).start(priority=1)
            return c
        lax.fori_loop(0, TM, body, 0)

    @pl.when(i == 0)
    def _():
        issue(0, 0)

    @pl.when(i + 1 < n)
    def _():
        issue(i + 1, (i + 1) % 2)

    slot = i % 2

    def drain(r, c):
        copy(0, b1, slot, 0).wait()
        copy(0, b2, slot, 0).wait()
        return c

    lax.fori_loop(0, TM, drain, 0)

    def unpack_group(buf, grp):
        w = buf[slot, grp * GRP:(grp + 1) * GRP].reshape(GRP * SUB, PK)
        hi = lax.bitcast_convert_type(w & -65536, F32).astype(BF16)
        lo = lax.bitcast_convert_type(w << 16, F32).astype(BF16)
        return jnp.dot(perm_ref[...], jnp.concatenate([hi, lo], axis=1), preferred_element_type=F32)

    half = D_MODEL // 2
    for grp in range(TM // GRP):
        rows = slice(grp * GRP, (grp + 1) * GRP)
        w1 = wcol_ref[rows, 0:1]
        w2 = wcol_ref[rows, 1:2]
        r1 = unpack_group(b1, grp)
        r2 = unpack_group(b2, grp)
        for s in range(SUB):
            sr = slice(s * GRP, (s + 1) * GRP)
            for c0, lanes in ((s * PK, slice(0, PK)), (half + s * PK, slice(PK, 2 * PK))):
                cs = slice(c0, c0 + PK)
                o_ref[rows, cs] = xmid_ref[rows, cs] + ga2_ref[:, cs] * (w1 * r1[sr, lanes] + w2 * r2[sr, lanes])


def _combine(dest, yb, xmid, wcol, modt, perm, n_tiles):
    tile = lambda n: pl.BlockSpec((TM, n), lambda i, d: (i, 0))
    return pl.pallas_call(
        functools.partial(_combine_kernel, n=n_tiles),
        grid_spec=pltpu.PrefetchScalarGridSpec(
            num_scalar_prefetch=1, grid=(n_tiles,),
            in_specs=[pl.BlockSpec(memory_space=pl.ANY), tile(D_MODEL), tile(128),
                      pl.BlockSpec((None, None, 1, D_MODEL), lambda i, d: (i // NT_LAT, 5, 0, 0)),
                      pl.BlockSpec((GRP * SUB, GRP * SUB), lambda i, d: (0, 0))],
            out_specs=tile(D_MODEL),
            scratch_shapes=[pltpu.VMEM((2, TM, SUB, PK), I32), pltpu.VMEM((2, TM, SUB, PK), I32),
                            pltpu.SemaphoreType.DMA((2,))]),
        out_shape=jax.ShapeDtypeStruct((n_tiles * TM, D_MODEL), F32),
        compiler_params=_cp(("arbitrary",)),
        name="combine",
    )(dest, yb, xmid, wcol, modt, perm)


def _perm_experts(a):
    return a.reshape(a.shape[:-1] + (N_GROUPS, EPG)).swapaxes(-1, -2).reshape(a.shape)


def kernel(x, c, ctx, c_ctx, w_mod, b_mod, g_norm1, g_norm2, w_in, w_fourier, g_q, g_k, sink,
           g_branch_f, g_branch_a, w_out, w_router, b_router, w_exp_gate, w_exp_up, w_exp_down,
           w_sh_gate, w_sh_up, w_sh_down):
    assert x.shape == (1, SEQ, D_MODEL) and ctx.shape == (1, CTX, D_MODEL)
    tb = _tables()
    stream = (x[0], ctx[0], 0)

    cc = jnp.zeros((8, D_MODEL), F32).at[0].set(c[0]).at[1].set(c_ctx)
    mods = _modulation(cc, w_mod, b_mod)

    wr_t = _perm_experts(w_router).T
    wrh = wr_t.astype(BF16)
    wrl = (wr_t - wrh.astype(F32)).astype(BF16)
    br = jnp.broadcast_to(_perm_experts(b_router)[:, None], (N_EXPERTS, TM)).astype(F32)

    for l in range(DEPTH):
        last = l == DEPTH - 1
        modt = mods[l, 0:2].reshape(2, 6, 1, D_MODEL)
        row = lambda a: a[l].reshape(1, -1)
        u, q, k, v = _project(*stream, row(g_norm1), modt, w_in[l].astype(BF16), row(g_q), row(g_k), tb)
        an = _attention(q, k, v, sink[l], row(g_branch_a), tb["band"])
        fn = _fourier(u, w_fourier[l].astype(BF16), row(g_branch_f), tb)
        xmid, h2, info, wcol, cnt = _merge(
            *stream, fn, an, w_out[l].astype(BF16), modt, row(g_norm2),
            w_sh_gate[l].astype(BF16), w_sh_up[l].astype(BF16), w_sh_down[l].astype(BF16),
            wrh, wrl, br, tb)

        xs = _routed(l, info, cnt, h2, xmid, wcol, modt, w_exp_gate, w_exp_up, w_exp_down, tb,
                     NT_LAT if last else NT)
        stream = (xs, xs, NT_LAT)
    return xs[None]


def _routed(l, info, cnt, h2, xmid, wcol, modt, w_g, w_u, w_d, tb, n_tiles):
    counts = cnt[:, 0].astype(I32).reshape(EPG, N_GROUPS).T.reshape(N_EXPERTS)
    padded = ((counts + BM - 1) // BM) * BM
    pad_end = jnp.cumsum(padded).astype(I32)
    pad_start = pad_end - padded
    start = jnp.sum(jnp.where(info[0:2, :, None] == jnp.arange(N_EXPERTS, dtype=I32),
                              pad_start, 0), axis=-1)
    dest = (start + info[2:4]).T.reshape(2 * T)
    ids = jnp.arange(N_EXPERTS, dtype=I32)
    block_row0 = jnp.arange(NB, dtype=I32)[:, None] * BM
    block_e = jnp.minimum(jnp.sum((pad_end <= block_row0).astype(I32), axis=-1), N_EXPERTS - 1)
    used = padded > 0
    ordinal = jnp.cumsum(used.astype(I32)) - 1
    eo = jnp.sum(jnp.where(block_e[:, None] == ids, ordinal, 0), axis=-1)
    es = jnp.sum(jnp.where(jnp.logical_and(used, ordinal == ids[:, None]), ids, 0), axis=-1)
    nn = jnp.stack([pad_end[N_EXPERTS - 1] // BM, ordinal[N_EXPERTS - 1] + 1]).astype(I32)
    xb = _dispatch(dest, pad_end, padded, h2, tb["permt"])
    yb = _experts(l, eo.astype(I32), es.astype(I32), nn, xb, tb["perm"], w_g, w_u, w_d)
    return _combine(dest, yb, xmid, wcol, modt, tb["perm"], n_tiles)
```

```python
import functools

import numpy as np
import jax
import jax.numpy as jnp
from jax import lax
from jax.experimental import pallas as pl
from jax.experimental.pallas import tpu as pltpu

F32 = jnp.float32
BF16 = jnp.bfloat16
I32 = jnp.int32

D_MODEL = 2048
SEQ = 8192
CTX = 256
T = SEQ + CTX
DEPTH = 4
GRID_W = 64
HEAD_DIM = 128
D_F = 512
N_FG = 4
N_HEADS = 12
N_KV = 4
Q_PER_KV = 3
D_ATTN = N_HEADS * HEAD_DIM
D_KV = N_KV * HEAD_DIM
D_IN = D_F + D_ATTN + 2 * D_KV
ROPE_BASE = 10000.0
N_EXPERTS = 32
N_GROUPS = 8
EPG = 4
D_EXPERT = 512
D_SHARED = 512
EPS = 1e-6
NEG = -1e30
LOG2E = 1.4426950408889634
WINDOW = 128

TM = 256
NT = T // TM
NT_LAT = SEQ // TM
QB = 128
NQB = T // QB
NQB_LAT = SEQ // QB
BM = 256
NB = (2 * T + BM - 1) // BM + N_EXPERTS
ROWS = NB * BM
N1 = 128
N2 = 64
SUB = 8
RL = D_MODEL // SUB
PK = D_MODEL // 2 // SUB
VMEM_LIMIT = 56 * 1024 * 1024


def _cp(sem):
    return pltpu.CompilerParams(dimension_semantics=sem, vmem_limit_bytes=VMEM_LIMIT)


def _silu(a):
    return a * jax.nn.sigmoid(a)


@functools.lru_cache(maxsize=None)
def _tables():
    t = np.arange(SEQ)
    row = (t // GRID_W).astype(np.float64)
    col = (t % GRID_W).astype(np.float64)
    inv = ROPE_BASE ** (-np.arange(0, HEAD_DIM // 2, 2, dtype=np.float64) / (HEAD_DIM // 2))
    ar = row[:, None] * inv
    ac = col[:, None] * inv
    ang = np.concatenate([ar, ar, ac, ac], axis=-1)
    cos = np.concatenate([np.cos(ang), np.ones((CTX, HEAD_DIM))], axis=0)
    sin = np.concatenate([np.sin(ang), np.zeros((CTX, HEAD_DIM))], axis=0)
    first = (np.arange(HEAD_DIM) % 64) < 32
    rope_a = np.where(first[None, :], -sin, 0.0)
    rope_b = np.where(first[None, :], 0.0, sin)

    def cs(n):
        k = np.arange(n)
        a = 2.0 * np.pi * np.outer(k, k) / n
        return np.cos(a), np.sin(a)

    c64, s64 = cs(N2)
    c128, s128 = cs(N1)
    c256, s256 = cs(CTX)
    m1 = np.concatenate([c64, -s64], axis=0)
    m2 = np.concatenate([c128, s128], axis=0)
    mc = np.concatenate([c256, s256], axis=0)
    csch = np.concatenate([c128, s128], axis=0)
    a = 2.0 * np.pi * np.outer(np.arange(N1), np.arange(N2)) / SEQ
    twr = np.broadcast_to(np.cos(a)[:, :, None], (N1, N2, 128))
    twi = np.broadcast_to(-np.sin(a)[:, :, None], (N1, N2, 128))
    tri = (np.arange(TM)[:, None] < np.arange(TM)[None, :]).astype(np.float32)
    kpos = np.arange(3 * QB)[:, None] - QB
    qpos = (np.arange(Q_PER_KV * QB) % QB)[None, :]
    band = np.where(np.abs(qpos - kpos) <= WINDOW, 0.0, NEG)
    perm = np.zeros((GRP * SUB, GRP * SUB), np.float32)
    rr, ss = np.meshgrid(np.arange(GRP), np.arange(SUB), indexing="ij")
    perm[(ss * GRP + rr).ravel(), (rr * SUB + ss).ravel()] = 1.0
    return dict(
        cos=jnp.asarray(cos, F32), rope_a=jnp.asarray(rope_a, F32), rope_b=jnp.asarray(rope_b, F32),
        m1=jnp.asarray(m1, BF16), m2=jnp.asarray(m2, BF16), mc=jnp.asarray(mc, BF16),
        csch=jnp.asarray(csch, BF16), twr=jnp.asarray(twr, F32), twi=jnp.asarray(twi, F32),
        tri=jnp.asarray(tri, BF16), perm=jnp.asarray(perm, BF16), band=jnp.asarray(band, F32))


MOD_TN = 1024


def _mod_kernel(cc_ref, w_ref, b_ref, o_ref):
    a = _silu(cc_ref[...])
    o_ref[...] = jnp.dot(a.astype(BF16), w_ref[...].astype(BF16),
                         preferred_element_type=F32) + b_ref[...]


def _modulation(cc, w_mod, b_mod):
    nl = w_mod.shape[0]
    n6 = w_mod.shape[2]
    return pl.pallas_call(
        _mod_kernel,
        grid=(nl, n6 // MOD_TN),
        in_specs=[pl.BlockSpec((8, D_MODEL), lambda l, j: (0, 0)),
                  pl.BlockSpec((None, D_MODEL, MOD_TN), lambda l, j: (l, 0, j)),
                  pl.BlockSpec((None, 1, MOD_TN), lambda l, j: (l, 0, j))],
        out_specs=pl.BlockSpec((None, 8, MOD_TN), lambda l, j: (l, 0, j)),
        out_shape=jax.ShapeDtypeStruct((nl, 8, n6), F32),
        compiler_params=_cp(("arbitrary", "arbitrary")),
        name="modulation",
    )(cc, w_mod, b_mod.reshape(nl, 1, n6))


def _mod_spec(chunk):
    return pl.BlockSpec((None, None, 1, D_MODEL), lambda i: (i // NT_LAT, chunk, 0, 0))


def _row_spec(n):
    return pl.BlockSpec((1, n), lambda i: (0, 0))


PCW = 256


def _proj_kernel(x_ref, c_ref, g_ref, sc_ref, sh_ref, w_ref, gq_ref, gk_ref, cos_ref, ra_ref, rb_ref,
                 u_ref, q_ref, k_ref, v_ref, hb_ref, *, split):
    x = _stream_tile(x_ref, c_ref, split)
    ms = jnp.mean(x * x, axis=-1, keepdims=True)
    h = x * lax.rsqrt(ms + EPS) * g_ref[...]
    h = h * (1.0 + sc_ref[...]) + sh_ref[...]
    hb_ref[...] = h.astype(BF16)
    cos = cos_ref[...]
    ra = ra_ref[...]
    rb = rb_ref[...]

    def head(t, g):
        m = jnp.mean(t * t, axis=-1, keepdims=True)
        t = t * lax.rsqrt(m + EPS) * g
        return t * cos + pltpu.roll(t, 96, 1) * ra + pltpu.roll(t, 32, 1) * rb

    scale = HEAD_DIM ** -0.5 * LOG2E
    for c in range(D_IN // PCW):
        col = c * PCW
        p = jnp.dot(hb_ref[...], w_ref[:, col:col + PCW], preferred_element_type=F32)
        if col < D_F:
            u_ref[:, col:col + PCW] = p
        elif col < D_F + D_ATTN:
            o = col - D_F
            for j in range(PCW // HEAD_DIM):
                t = head(p[:, j * HEAD_DIM:(j + 1) * HEAD_DIM], gq_ref[...]) * scale
                q_ref[:, o + j * HEAD_DIM:o + (j + 1) * HEAD_DIM] = t.astype(BF16)
        elif col < D_F + D_ATTN + D_KV:
            o = col - D_F - D_ATTN
            for j in range(PCW // HEAD_DIM):
                t = head(p[:, j * HEAD_DIM:(j + 1) * HEAD_DIM], gk_ref[...])
                k_ref[:, o + j * HEAD_DIM:o + (j + 1) * HEAD_DIM] = t.astype(BF16)
        else:
            o = col - D_F - D_ATTN - D_KV
            v_ref[:, o:o + PCW] = p.astype(BF16)


def _layer_spec(l, a, b):
    return pl.BlockSpec((None, a, b), lambda i: (l, 0, 0))


def _project(l, xa, ca, c_blk, g1, modt, w_in_bf, gq, gk, tb):
    tile = lambda n: pl.BlockSpec((TM, n), lambda i: (i, 0))
    return pl.pallas_call(
        functools.partial(_proj_kernel, split=xa is not ca),
        grid=(NT,),
        in_specs=_stream_specs(xa is not ca, c_blk) + [
                  _row_spec(D_MODEL), _mod_spec(1), _mod_spec(0),
                  _layer_spec(l, D_MODEL, D_IN),
                  _row_spec(HEAD_DIM), _row_spec(HEAD_DIM),
                  tile(HEAD_DIM), tile(HEAD_DIM), tile(HEAD_DIM)],
        out_specs=[tile(D_F), tile(D_ATTN), tile(D_KV), tile(D_KV)],
        out_shape=[jax.ShapeDtypeStruct((T, D_F), F32), jax.ShapeDtypeStruct((T, D_ATTN), BF16),
                   jax.ShapeDtypeStruct((T, D_KV), BF16), jax.ShapeDtypeStruct((T, D_KV), BF16)],
        scratch_shapes=[pltpu.VMEM((TM, D_MODEL), BF16)],
        compiler_params=_cp(("arbitrary",)),
        name="norm_proj",
    )(xa, ca, g1, modt, modt, w_in_bf, gq, gk, tb["cos"], tb["rope_a"], tb["rope_b"])


def _attn_kernel(sink_ref, q_ref, kp_ref, kc_ref, kn_ref, vp_ref, vc_ref, vn_ref, kx_ref, vx_ref,
                 ga_ref, band_ref, o_ref, a_scr):
    b = pl.program_id(0)
    lat = b < NQB_LAT
    off = lambda ok: jnp.where(ok, 0.0, NEG)
    bias = jnp.concatenate(
        [band_ref[0:QB] + off(jnp.logical_and(lat, b > 0)),
         band_ref[QB:2 * QB] + off(lat),
         band_ref[2 * QB:3 * QB] + off(b < NQB_LAT - 1)], axis=0)
    dn = (((1,), (1,)), ((), ()))
    for hk in range(N_KV):
        ks = slice(hk * HEAD_DIM, (hk + 1) * HEAD_DIM)
        qs = jnp.concatenate(
            [q_ref[:, (hk * Q_PER_KV + g) * HEAD_DIM:(hk * Q_PER_KV + g + 1) * HEAD_DIM]
             for g in range(Q_PER_KV)], axis=0)
        kw = jnp.concatenate([kp_ref[:, ks], kc_ref[:, ks], kn_ref[:, ks]], axis=0)
        vw = jnp.concatenate([vp_ref[:, ks], vc_ref[:, ks], vn_ref[:, ks]], axis=0)
        st = lax.dot_general(kw, qs, dn, preferred_element_type=F32) + bias
        sxt = lax.dot_general(kx_ref[:, ks], qs, dn, preferred_element_type=F32)
        sk = jnp.concatenate(
            [jnp.full((1, QB), sink_ref[hk * Q_PER_KV + g] * LOG2E, F32) for g in range(Q_PER_KV)],
            axis=1)
        m = jnp.maximum(jnp.maximum(jnp.max(st, axis=0, keepdims=True),
                                    jnp.max(sxt, axis=0, keepdims=True)), sk)
        e = jnp.exp2(st - m)
        ex = jnp.exp2(sxt - m)
        den = (jnp.sum(e, axis=0, keepdims=True) + jnp.sum(ex, axis=0, keepdims=True)
               + jnp.exp2(sk - m))
        dt = (((0,), (0,)), ((), ()))
        ot = (lax.dot_general(vw, e.astype(BF16), dt, preferred_element_type=F32)
              + lax.dot_general(vx_ref[:, ks], ex.astype(BF16), dt, preferred_element_type=F32)) / den
        for g in range(Q_PER_KV):
            h = hk * Q_PER_KV + g
            a_scr[h * HEAD_DIM:(h + 1) * HEAD_DIM, :] = ot[:, g * QB:(g + 1) * QB]
    at = a_scr[...]
    ms = jnp.mean(at * at, axis=0, keepdims=True)
    o_ref[...] = (at * lax.rsqrt(ms + EPS) * ga_ref[...]).T.astype(BF16)


def _attention(q, k, v, sink, g_a, band):
    kb = lambda f: pl.BlockSpec((QB, D_KV), lambda b: (f(b), 0))
    prev = lambda b: jnp.maximum(b - 1, 0)
    cur = lambda b: b
    nxt = lambda b: jnp.minimum(b + 1, NQB - 1)
    g_cols = jnp.broadcast_to(g_a.reshape(D_ATTN, 1), (D_ATTN, QB))
    return pl.pallas_call(
        _attn_kernel,
        grid=(NQB,),
        in_specs=[pl.BlockSpec(memory_space=pltpu.SMEM),
                  pl.BlockSpec((QB, D_ATTN), lambda b: (b, 0)),
                  kb(prev), kb(cur), kb(nxt), kb(prev), kb(cur), kb(nxt),
                  pl.BlockSpec((CTX, D_KV), lambda b: (SEQ // CTX, 0)),
                  pl.BlockSpec((CTX, D_KV), lambda b: (SEQ // CTX, 0)),
                  pl.BlockSpec((D_ATTN, QB), lambda b: (0, 0)),
                  pl.BlockSpec((3 * QB, Q_PER_KV * QB), lambda b: (0, 0))],
        out_specs=pl.BlockSpec((QB, D_ATTN), lambda b: (b, 0)),
        out_shape=jax.ShapeDtypeStruct((T, D_ATTN), BF16),
        scratch_shapes=[pltpu.VMEM((D_ATTN, QB), F32)],
        compiler_params=_cp(("arbitrary",)),
        name="attention",
    )(sink, q, k, k, k, v, v, v, k, v, g_cols, band)


F1_J = 8
F2_K = 4


def _f1_kernel(u_ref, m1_ref, twr_ref, twi_ref, z_ref):
    for jl in range(F1_J):
        g = jnp.dot(m1_ref[...], u_ref[:, jl, :].astype(BF16), preferred_element_type=F32)
        twr = twr_ref[jl]
        twi = twi_ref[jl]
        for lt in range(D_F // 128):
            c0 = lt * 128
            gr = g[0:N2, c0:c0 + 128]
            gi = g[N2:2 * N2, c0:c0 + 128]
            z_ref[jl, 0:N2, lt * 128:(lt + 1) * 128] = (gr * twr - gi * twi).astype(BF16)
            z_ref[jl, N2:2 * N2, lt * 128:(lt + 1) * 128] = (gr * twi + gi * twr).astype(BF16)


def _f2_kernel(zr_ref, zi_ref, m2_ref, y_ref):
    pr = jnp.dot(m2_ref[...], zr_ref[...], preferred_element_type=F32)
    pi = jnp.dot(m2_ref[...], zi_ref[...], preferred_element_type=F32)
    yr = pr[0:N1] + pi[N1:2 * N1]
    yi = pi[0:N1] - pr[N1:2 * N1]
    for kl in range(F2_K):
        y_ref[:, kl * 2 * D_F:kl * 2 * D_F + D_F] = yr[:, kl * D_F:(kl + 1) * D_F].astype(BF16)
        y_ref[:, kl * 2 * D_F + D_F:(kl + 1) * 2 * D_F] = yi[:, kl * D_F:(kl + 1) * D_F].astype(BF16)


def _f3_kernel(y_ref, uc_ref, cs_ref, mc_ref, wf_ref, gf_ref, o_ref):
    i = pl.program_id(0)

    def finish(parts, scale):
        f = jnp.concatenate(parts, axis=1) * scale
        t = jnp.dot(f.astype(BF16), wf_ref[...], preferred_element_type=F32)
        ms = jnp.mean(t * t, axis=-1, keepdims=True)
        o_ref[...] = (t * lax.rsqrt(ms + EPS) * gf_ref[...]).astype(BF16)

    @pl.when(i < NT_LAT)
    def _():
        parts = []
        for g in range(N_FG):
            lhs = jnp.concatenate([y_ref[:, g * 128:(g + 1) * 128],
                                   y_ref[:, D_F + g * 128:D_F + (g + 1) * 128]], axis=1)
            parts.append(jnp.dot(lhs, cs_ref[...], preferred_element_type=F32))
        finish(parts, float((SEQ * 128) ** -0.5))

    @pl.when(i == NT_LAT)
    def _():
        pq = jnp.dot(mc_ref[...], uc_ref[...].astype(BF16), preferred_element_type=F32)
        parts = []
        for g in range(N_FG):
            lhs = jnp.concatenate([pq[0:CTX, g * 128:(g + 1) * 128],
                                   -pq[CTX:2 * CTX, g * 128:(g + 1) * 128]], axis=1)
            parts.append(jnp.dot(lhs.astype(BF16), cs_ref[...], preferred_element_type=F32))
        finish(parts, float((CTX * 128) ** -0.5))


def _fourier(l, u, w_f_bf, g_f, tb):
    u2 = u.reshape(T // N1, N1, D_F)
    z = pl.pallas_call(
        _f1_kernel,
        grid=(N1 // F1_J,),
        in_specs=[pl.BlockSpec((N2, F1_J, D_F), lambda c: (0, c, 0)),
                  pl.BlockSpec((2 * N2, N2), lambda c: (0, 0)),
                  pl.BlockSpec((F1_J, N2, 128), lambda c: (c, 0, 0)),
                  pl.BlockSpec((F1_J, N2, 128), lambda c: (c, 0, 0))],
        out_specs=pl.BlockSpec((F1_J, 2 * N2, D_F), lambda c: (c, 0, 0)),
        out_shape=jax.ShapeDtypeStruct((N1, 2 * N2, D_F), BF16),
        compiler_params=_cp(("arbitrary",)),
        name="fourier_s1",
    )(u2, tb["m1"], tb["twr"], tb["twi"])
    z2 = z.reshape(N1, 2 * N2 * D_F)
    nk = N2 // F2_K
    y = pl.pallas_call(
        _f2_kernel,
        grid=(nk,),
        in_specs=[pl.BlockSpec((N1, F2_K * D_F), lambda k: (0, k)),
                  pl.BlockSpec((N1, F2_K * D_F), lambda k: (0, nk + k)),
                  pl.BlockSpec((2 * N1, N1), lambda k: (0, 0))],
        out_specs=pl.BlockSpec((N1, F2_K * 2 * D_F), lambda k: (0, k)),
        out_shape=jax.ShapeDtypeStruct((N1, N2 * 2 * D_F), BF16),
        compiler_params=_cp(("arbitrary",)),
        name="fourier_s2",
    )(z2, z2, tb["m2"])
    y2 = y.reshape(SEQ, 2 * D_F)
    return pl.pallas_call(
        _f3_kernel,
        grid=(NT,),
        in_specs=[pl.BlockSpec((TM, 2 * D_F), lambda i: (jnp.minimum(i, NT_LAT - 1), 0)),
                  pl.BlockSpec((CTX, D_F), lambda i: (SEQ // CTX, 0)),
                  pl.BlockSpec((2 * 128, 128), lambda i: (0, 0)),
                  pl.BlockSpec((2 * CTX, CTX), lambda i: (0, 0)),
                  _layer_spec(l, D_F, D_F),
                  _row_spec(D_F)],
        out_specs=pl.BlockSpec((TM, D_F), lambda i: (i, 0)),
        out_shape=jax.ShapeDtypeStruct((T, D_F), BF16),
        compiler_params=_cp(("arbitrary",)),
        name="fourier_s3",
    )(y2, u, tb["csch"], tb["mc"], w_f_bf, g_f)


def _merge_kernel(x_ref, c_ref, fn_ref, an_ref, wo_ref, ga1_ref, g2_ref, sc2_ref, sh2_ref, ga2_ref,
                  wsg_ref, wsu_ref, wsd_ref, wrh_ref, wrl_ref, br_ref, tri_ref,
                  xmid_ref, h2_ref, info_ref, wcol_ref, cnt_ref, carry_scr, *, split):
    i = pl.program_id(0)

    @pl.when(i == 0)
    def _():
        carry_scr[...] = jnp.zeros_like(carry_scr)

    m = (jnp.dot(fn_ref[...], wo_ref[0:D_F, :], preferred_element_type=F32)
         + jnp.dot(an_ref[...], wo_ref[D_F:D_MODEL, :], preferred_element_type=F32))
    x1 = _stream_tile(x_ref, c_ref, split) + ga1_ref[...] * m
    ms = jnp.mean(x1 * x1, axis=-1, keepdims=True)
    h = x1 * lax.rsqrt(ms + EPS) * g2_ref[...]
    h = h * (1.0 + sc2_ref[...]) + sh2_ref[...]
    hb = h.astype(BF16)
    hp = _pack_bf16_pairs(h)
    for s in range(SUB):
        h2_ref[:, s, :] = hp[:, s * PK:(s + 1) * PK]

    a = jnp.dot(hb, wsg_ref[...], preferred_element_type=F32)
    u = jnp.dot(hb, wsu_ref[...], preferred_element_type=F32)
    act = (_silu(a) * u).astype(BF16)
    ys = jnp.dot(act, wsd_ref[...], preferred_element_type=F32)
    xmid_ref[...] = x1 + ga2_ref[...] * ys

    hl = (h - hb.astype(F32)).astype(BF16)
    dn = (((1,), (1,)), ((), ()))
    both = lax.dot_general(jnp.concatenate([wrh_ref[...], wrl_ref[...]], axis=0), hb, dn,
                           preferred_element_type=F32)
    lg = both[0:N_EXPERTS] + (both[N_EXPERTS:2 * N_EXPERTS]
                              + lax.dot_general(wrh_ref[...], hl, dn, preferred_element_type=F32))
    score = jax.nn.sigmoid(lg)
    sel = score + br_ref[...]
    s = [sel[jj * N_GROUPS:(jj + 1) * N_GROUPS] for jj in range(EPG)]
    sr = [score[jj * N_GROUPS:(jj + 1) * N_GROUPS] for jj in range(EPG)]
    hi01, lo01 = jnp.maximum(s[0], s[1]), jnp.minimum(s[0], s[1])
    hi23, lo23 = jnp.maximum(s[2], s[3]), jnp.minimum(s[2], s[3])
    gscore = jnp.maximum(hi01, hi23) + jnp.maximum(jnp.minimum(hi01, hi23), jnp.maximum(lo01, lo23))
    gi = lax.broadcasted_iota(I32, (N_GROUPS, TM), 0)
    gmax = jnp.max(gscore, axis=0, keepdims=True)
    g_idx = jnp.min(jnp.where(gscore == gmax, gi, N_GROUPS), axis=0, keepdims=True)
    gsel = gi == g_idx
    v = [jnp.sum(jnp.where(gsel, s[jj], 0.0), axis=0, keepdims=True) for jj in range(EPG)]
    vr = [jnp.sum(jnp.where(gsel, sr[jj], 0.0), axis=0, keepdims=True) for jj in range(EPG)]

    def first_argmax(vals):
        best = vals[0]
        idx = jnp.zeros((1, TM), I32)
        for jj in range(1, EPG):
            upd = vals[jj] > best
            best = jnp.where(upd, vals[jj], best)
            idx = jnp.where(upd, jj, idx)
        return idx

    i1 = first_argmax(v)
    i2 = first_argmax([jnp.where(i1 == jj, -jnp.inf, v[jj]) for jj in range(EPG)])

    def pick(vals, idx):
        out = vals[0]
        for jj in range(1, EPG):
            out = jnp.where(idx == jj, vals[jj], out)
        return out

    w1 = pick(vr, i1)
    w2 = pick(vr, i2)
    wsum = w1 + w2
    w1 = w1 / wsum
    w2 = w2 / wsum

    ri = lax.broadcasted_iota(I32, (N_EXPERTS, TM), 0)
    oh1 = ri == i1 * N_GROUPS + g_idx
    oh2 = ri == i2 * N_GROUPS + g_idx
    oh = jnp.logical_or(oh1, oh2).astype(F32)
    pre = jnp.dot(oh.astype(BF16), tri_ref[...], preferred_element_type=F32) + carry_scr[:, 0:1]
    r1 = jnp.sum(jnp.where(oh1, pre, 0.0), axis=0, keepdims=True)
    r2 = jnp.sum(jnp.where(oh2, pre, 0.0), axis=0, keepdims=True)
    carry = carry_scr[...] + jnp.sum(oh, axis=1, keepdims=True)
    carry_scr[...] = carry
    cnt_ref[...] = carry

    zi = jnp.zeros((4, TM), I32)
    info_ref[...] = jnp.concatenate(
        [g_idx * EPG + i1, g_idx * EPG + i2, r1.astype(I32), r2.astype(I32), zi], axis=0)
    wrow = jnp.concatenate([w1, w2, jnp.zeros((126, TM), F32)], axis=0)
    wcol_ref[...] = wrow.T


def _stream_specs(split, c_blk):
    if split:
        return [pl.BlockSpec((TM, D_MODEL), lambda i: (jnp.minimum(i, NT_LAT - 1), 0)),
                pl.BlockSpec((CTX, D_MODEL), lambda i: (c_blk, 0))]
    return [pl.BlockSpec((TM, D_MODEL), lambda i: (i, 0)),
            pl.BlockSpec((8, D_MODEL), lambda i: (0, 0))]


def _stream_tile(x_ref, c_ref, split):
    if split:
        return jnp.where(pl.program_id(0) == NT_LAT, c_ref[...], x_ref[...])
    return x_ref[...]


def _merge(l, xa, ca, c_blk, fn, an, w_out_bf, modt, g2, wsg, wsu, wsd, wrh, wrl, br, tb):
    tile = lambda n: pl.BlockSpec((TM, n), lambda i: (i, 0))
    full = lambda a, b: pl.BlockSpec((a, b), lambda i: (0, 0))
    return pl.pallas_call(
        functools.partial(_merge_kernel, split=xa is not ca),
        grid=(NT,),
        in_specs=_stream_specs(xa is not ca, c_blk) + [
                  tile(D_F), tile(D_ATTN), _layer_spec(l, D_MODEL, D_MODEL),
                  _mod_spec(2), _row_spec(D_MODEL), _mod_spec(4), _mod_spec(3), _mod_spec(5),
                  _layer_spec(l, D_MODEL, D_SHARED), _layer_spec(l, D_MODEL, D_SHARED),
                  _layer_spec(l, D_SHARED, D_MODEL),
                  full(N_EXPERTS, D_MODEL), full(N_EXPERTS, D_MODEL), full(N_EXPERTS, TM),
                  full(TM, TM)],
        out_specs=[tile(D_MODEL), pl.BlockSpec((TM, SUB, PK), lambda i: (i, 0, 0)),
                   pl.BlockSpec((8, TM), lambda i: (0, i)),
                   pl.BlockSpec((TM, 128), lambda i: (i, 0)),
                   pl.BlockSpec((N_EXPERTS, 128), lambda i: (0, 0))],
        out_shape=[jax.ShapeDtypeStruct((T, D_MODEL), F32), jax.ShapeDtypeStruct((T, SUB, PK), I32),
                   jax.ShapeDtypeStruct((8, T), I32), jax.ShapeDtypeStruct((T, 128), F32),
                   jax.ShapeDtypeStruct((N_EXPERTS, 128), F32)],
        scratch_shapes=[pltpu.VMEM((N_EXPERTS, 128), F32)],
        compiler_params=_cp(("arbitrary",)),
        name="merge_route",
    )(xa, ca, fn, an, w_out_bf, modt, g2, modt, modt, modt, wsg, wsu, wsd, wrh, wrl, br, tb["tri"])


def _dispatch_kernel(dest_ref, pend_ref, padded_ref, h_ref, xb_hbm, zero_scr, slab, sem, zsem, *, n):
    i = pl.program_id(0)
    slot = i % 2

    def zero_copy(row0):
        return pltpu.make_async_copy(zero_scr, xb_hbm.at[pl.ds(row0, BM)], zsem)

    @pl.when(i == 0)
    def _():
        zero_scr[...] = jnp.zeros_like(zero_scr)
        n_used = pend_ref[N_EXPERTS - 1] // BM
        for e in range(N_EXPERTS):
            @pl.when(padded_ref[e] > 0)
            def _():
                zero_copy(pend_ref[e] - BM).start()
        for blk in range(NB - N_EXPERTS, NB):
            @pl.when(blk >= n_used)
            def _():
                zero_copy(blk * BM).start()
        for e in range(N_EXPERTS):
            @pl.when(padded_ref[e] > 0)
            def _():
                zero_copy(0).wait()
        for blk in range(NB - N_EXPERTS, NB):
            @pl.when(blk >= n_used)
            def _():
                zero_copy(0).wait()

    slab[slot] = h_ref[...]

    def copy(sl, r, d):
        return pltpu.make_async_copy(slab.at[sl, pl.ds(r, 1)], xb_hbm.at[pl.ds(d, 1)], sem.at[sl])

    def drain(sl):
        def body(r, c):
            copy(sl, 0, 0).wait()
            copy(sl, 0, 0).wait()
            return c
        lax.fori_loop(0, TM, body, 0)

    def issue(r, c):
        t = i * TM + r
        copy(slot, r, dest_ref[2 * t]).start()
        copy(slot, r, dest_ref[2 * t + 1]).start(priority=1)
        return c

    lax.fori_loop(0, TM, issue, 0)

    @pl.when(i > 0)
    def _():
        drain(1 - slot)

    @pl.when(i == n - 1)
    def _():
        drain(slot)


def _dispatch(dest, pad_end, padded, h2):
    return pl.pallas_call(
        functools.partial(_dispatch_kernel, n=NT),
        grid_spec=pltpu.PrefetchScalarGridSpec(
            num_scalar_prefetch=3, grid=(NT,),
            in_specs=[pl.BlockSpec((TM, SUB, PK), lambda i, d, pe, pd: (i, 0, 0))],
            out_specs=pl.BlockSpec(memory_space=pl.ANY),
            scratch_shapes=[pltpu.VMEM((BM, SUB, PK), I32), pltpu.VMEM((2, TM, SUB, PK), I32),
                            pltpu.SemaphoreType.DMA((2,)), pltpu.SemaphoreType.DMA(())]),
        out_shape=jax.ShapeDtypeStruct((ROWS, SUB, PK), I32),
        compiler_params=pltpu.CompilerParams(dimension_semantics=("arbitrary",),
                                             has_side_effects=True, vmem_limit_bytes=VMEM_LIMIT),
        name="dispatch",
    )(dest, pad_end, padded, h2)


def _pack_bf16_pairs(h):
    bits = lax.bitcast_convert_type(h.astype(BF16).astype(F32), I32)
    half = h.shape[1] // 2
    return (bits[:, :half] & -65536) | lax.shift_right_logical(bits[:, half:], 16)


GRP = 32


def _expert_kernel(eo_ref, es_ref, n_ref, x_ref, perm_ref, wg_hbm, wu_hbm, wd_hbm, y_ref,
                   wgf, wuf, wdf, wgb, wub, wdb, xs_scr, wsem, *, layer):
    b = pl.program_id(0)
    k = eo_ref[b]
    first = jnp.logical_or(b == 0, k != eo_ref[jnp.maximum(b - 1, 0)])

    def copies(kk, slot):
        e = es_ref[kk]
        return (pltpu.make_async_copy(wg_hbm.at[layer, e], wgf.at[slot], wsem.at[slot, 0]),
                pltpu.make_async_copy(wu_hbm.at[layer, e], wuf.at[slot], wsem.at[slot, 1]),
                pltpu.make_async_copy(wd_hbm.at[layer, e], wdf.at[slot], wsem.at[slot, 2]))

    @pl.when(b == 0)
    def _():
        for c in copies(0, 0):
            c.start()

    @pl.when(jnp.logical_and(first, b < n_ref[0]))
    def _():
        slot = k % 2
        for c in copies(k, slot):
            c.wait()

        @pl.when(k + 1 < n_ref[1])
        def _():
            for c in copies(k + 1, 1 - slot):
                c.start()

        wgb[...] = wgf[slot].astype(BF16)
        wub[...] = wuf[slot].astype(BF16)
        wdb[...] = wdf[slot].astype(BF16)

    @pl.when(b < n_ref[0])
    def _():
        half = D_MODEL // 2
        for grp in range(BM // GRP):
            w = x_ref[grp * GRP:(grp + 1) * GRP].reshape(GRP * SUB, PK)
            hi = lax.bitcast_convert_type(w & -65536, F32).astype(BF16)
            lo = lax.bitcast_convert_type(w << 16, F32).astype(BF16)
            g = jnp.concatenate([hi, lo], axis=1)
            r = jnp.dot(perm_ref[...], g, preferred_element_type=F32).astype(BF16)
            rows = slice(grp * GRP, (grp + 1) * GRP)
            for s in range(SUB):
                xs_scr[rows, s * PK:(s + 1) * PK] = r[s * GRP:(s + 1) * GRP, 0:PK]
                xs_scr[rows, half + s * PK:half + (s + 1) * PK] = r[s * GRP:(s + 1) * GRP, PK:2 * PK]
        x = xs_scr[...]
        a = jnp.dot(x, wgb[...], preferred_element_type=F32)
        u = jnp.dot(x, wub[...], preferred_element_type=F32)
        act = (_silu(a) * u).astype(BF16)
        yp = _pack_bf16_pairs(jnp.dot(act, wdb[...], preferred_element_type=F32))
        for s in range(SUB):
            y_ref[:, s, :] = yp[:, s * PK:(s + 1) * PK]

    @pl.when(b >= n_ref[0])
    def _():
        y_ref[...] = jnp.zeros_like(y_ref)


def _experts(l, eo, es, nn, xb, perm, w_g, w_u, w_d):
    hbm = pl.BlockSpec(memory_space=pl.ANY)
    return pl.pallas_call(
        functools.partial(_expert_kernel, layer=l),
        grid_spec=pltpu.PrefetchScalarGridSpec(
            num_scalar_prefetch=3, grid=(NB,),
            in_specs=[pl.BlockSpec((BM, SUB, PK), lambda b, eo, es, n: (jnp.minimum(b, n[0] - 1), 0, 0)),
                      pl.BlockSpec((GRP * SUB, GRP * SUB), lambda b, eo, es, n: (0, 0)),
                      hbm, hbm, hbm],
            out_specs=pl.BlockSpec((BM, SUB, PK), lambda b, eo, es, n: (b, 0, 0)),
            scratch_shapes=[pltpu.VMEM((2, D_MODEL, D_EXPERT), F32),
                            pltpu.VMEM((2, D_MODEL, D_EXPERT), F32),
                            pltpu.VMEM((2, D_EXPERT, D_MODEL), F32),
                            pltpu.VMEM((D_MODEL, D_EXPERT), BF16),
                            pltpu.VMEM((D_MODEL, D_EXPERT), BF16),
                            pltpu.VMEM((D_EXPERT, D_MODEL), BF16),
                            pltpu.VMEM((BM, D_MODEL), BF16),
                            pltpu.SemaphoreType.DMA((2, 3))]),
        out_shape=jax.ShapeDtypeStruct((ROWS, SUB, PK), I32),
        compiler_params=_cp(("arbitrary",)),
        name="experts",
    )(eo, es, nn, xb, perm, w_g, w_u, w_d)


def _combine_kernel(dest_ref, y_hbm, xmid_ref, wcol_ref, ga2_ref, perm_ref, o_ref, b1, b2, sem, *, n):
    i = pl.program_id(0)

    def copy(d, buf, slot, r):
        return pltpu.make_async_copy(y_hbm.at[pl.ds(d, 1)], buf.at[slot, pl.ds(r, 1)], sem.at[slot])

    def issue(tile, slot):
        def body(r, c):
            t = tile * TM + r
            copy(dest_ref[2 * t], b1, slot, r).start()
            copy(dest_ref[2 * t + 1], b2, slot, r).start(priority=1)
            return c
        lax.fori_loop(0, TM, body, 0)

    @pl.when(i == 0)
    def _():
        issue(0, 0)

    @pl.when(i + 1 < n)
    def _():
        issue(i + 1, (i + 1) % 2)

    slot = i % 2

    def drain(r, c):
        copy(0, b1, slot, 0).wait()
        copy(0, b2, slot, 0).wait()
        return c

    lax.fori_loop(0, TM, drain, 0)

    def unpack_group(buf, grp):
        w = buf[slot, grp * GRP:(grp + 1) * GRP].reshape(GRP * SUB, PK)
        hi = lax.bitcast_convert_type(w & -65536, F32).astype(BF16)
        lo = lax.bitcast_convert_type(w << 16, F32).astype(BF16)
        return jnp.dot(perm_ref[...], jnp.concatenate([hi, lo], axis=1), preferred_element_type=F32)

    half = D_MODEL // 2
    for grp in range(TM // GRP):
        rows = slice(grp * GRP, (grp + 1) * GRP)
        w1 = wcol_ref[rows, 0:1]
        w2 = wcol_ref[rows, 1:2]
        r1 = unpack_group(b1, grp)
        r2 = unpack_group(b2, grp)
        for s in range(SUB):
            sr = slice(s * GRP, (s + 1) * GRP)
            for c0, lanes in ((s * PK, slice(0, PK)), (half + s * PK, slice(PK, 2 * PK))):
                cs = slice(c0, c0 + PK)
                o_ref[rows, cs] = xmid_ref[rows, cs] + ga2_ref[:, cs] * (w1 * r1[sr, lanes] + w2 * r2[sr, lanes])


def _combine(dest, yb, xmid, wcol, modt, perm, n_tiles):
    tile = lambda n: pl.BlockSpec((TM, n), lambda i, d: (i, 0))
    return pl.pallas_call(
        functools.partial(_combine_kernel, n=n_tiles),
        grid_spec=pltpu.PrefetchScalarGridSpec(
            num_scalar_prefetch=1, grid=(n_tiles,),
            in_specs=[pl.BlockSpec(memory_space=pl.ANY), tile(D_MODEL), tile(128),
                      pl.BlockSpec((None, None, 1, D_MODEL), lambda i, d: (i // NT_LAT, 5, 0, 0)),
                      pl.BlockSpec((GRP * SUB, GRP * SUB), lambda i, d: (0, 0))],
            out_specs=tile(D_MODEL),
            scratch_shapes=[pltpu.VMEM((2, TM, SUB, PK), I32), pltpu.VMEM((2, TM, SUB, PK), I32),
                            pltpu.SemaphoreType.DMA((2,))]),
        out_shape=jax.ShapeDtypeStruct((n_tiles * TM, D_MODEL), F32),
        compiler_params=_cp(("arbitrary",)),
        name="combine",
    )(dest, yb, xmid, wcol, modt, perm)


def _perm_experts(a):
    return a.reshape(a.shape[:-1] + (N_GROUPS, EPG)).swapaxes(-1, -2).reshape(a.shape)


def kernel(x, c, ctx, c_ctx, w_mod, b_mod, g_norm1, g_norm2, w_in, w_fourier, g_q, g_k, sink,
           g_branch_f, g_branch_a, w_out, w_router, b_router, w_exp_gate, w_exp_up, w_exp_down,
           w_sh_gate, w_sh_up, w_sh_down):
    assert x.shape == (1, SEQ, D_MODEL) and ctx.shape == (1, CTX, D_MODEL)
    tb = _tables()
    stream = (x[0], ctx[0], 0)

    cc = jnp.zeros((8, D_MODEL), F32).at[0].set(c[0]).at[1].set(c_ctx)
    mods = _modulation(cc, w_mod, b_mod)

    wr_t = _perm_experts(w_router).T
    wrh = wr_t.astype(BF16)
    wrl = (wr_t - wrh.astype(F32)).astype(BF16)
    br = jnp.broadcast_to(_perm_experts(b_router)[:, None], (N_EXPERTS, TM)).astype(F32)

    w_in_b, w_f_b, w_out_b = w_in.astype(BF16), w_fourier.astype(BF16), w_out.astype(BF16)
    w_sg_b, w_su_b, w_sd_b = w_sh_gate.astype(BF16), w_sh_up.astype(BF16), w_sh_down.astype(BF16)

    for l in range(DEPTH):
        last = l == DEPTH - 1
        modt = mods[l, 0:2].reshape(2, 6, 1, D_MODEL)
        row = lambda a: a[l].reshape(1, -1)
        u, q, k, v = _project(l, *stream, row(g_norm1), modt, w_in_b, row(g_q), row(g_k), tb)
        an = _attention(q, k, v, sink[l], row(g_branch_a), tb["band"])
        fn = _fourier(l, u, w_f_b, row(g_branch_f), tb)
        xmid, h2, info, wcol, cnt = _merge(
            l, *stream, fn, an, w_out_b, modt, row(g_norm2), w_sg_b, w_su_b, w_sd_b,
            wrh, wrl, br, tb)

        xs = _routed(l, info, cnt, h2, xmid, wcol, modt, w_exp_gate, w_exp_up, w_exp_down, tb,
                     NT_LAT if last else NT)
        stream = (xs, xs, NT_LAT)
    return xs[None]


def _routed(l, info, cnt, h2, xmid, wcol, modt, w_g, w_u, w_d, tb, n_tiles):
    counts = cnt[:, 0].astype(I32).reshape(EPG, N_GROUPS).T.reshape(N_EXPERTS)
    padded = ((counts + BM - 1) // BM) * BM
    pad_end = jnp.cumsum(padded).astype(I32)
    pad_start = pad_end - padded
    start = jnp.sum(jnp.where(info[0:2, :, None] == jnp.arange(N_EXPERTS, dtype=I32),
                              pad_start, 0), axis=-1)
    dest = (start + info[2:4]).T.reshape(2 * T)
    ids = jnp.arange(N_EXPERTS, dtype=I32)
    block_row0 = jnp.arange(NB, dtype=I32)[:, None] * BM
    block_e = jnp.minimum(jnp.sum((pad_end <= block_row0).astype(I32), axis=-1), N_EXPERTS - 1)
    used = padded > 0
    ordinal = jnp.cumsum(used.astype(I32)) - 1
    eo = jnp.sum(jnp.where(block_e[:, None] == ids, ordinal, 0), axis=-1)
    es = jnp.sum(jnp.where(jnp.logical_and(used, ordinal == ids[:, None]), ids, 0), axis=-1)
    nn = jnp.stack([pad_end[N_EXPERTS - 1] // BM, ordinal[N_EXPERTS - 1] + 1]).astype(I32)
    xb = _dispatch(dest, pad_end, padded, h2)
    yb = _experts(l, eo.astype(I32), es.astype(I32), nn, xb, tb["perm"], w_g, w_u, w_d)
    return _combine(dest, yb, xmid, wcol, modt, tb["perm"], n_tiles)
```

```python
import functools

import numpy as np
import jax
import jax.numpy as jnp
from jax import lax
from jax.experimental import pallas as pl
from jax.experimental.pallas import tpu as pltpu

F32 = jnp.float32
BF16 = jnp.bfloat16
I32 = jnp.int32

D_MODEL = 2048
SEQ = 8192
CTX = 256
T = SEQ + CTX
DEPTH = 4
GRID_W = 64
HEAD_DIM = 128
D_F = 512
N_FG = 4
N_HEADS = 12
N_KV = 4
Q_PER_KV = 3
D_ATTN = N_HEADS * HEAD_DIM
D_KV = N_KV * HEAD_DIM
D_IN = D_F + D_ATTN + 2 * D_KV
ROPE_BASE = 10000.0
N_EXPERTS = 32
N_GROUPS = 8
EPG = 4
D_EXPERT = 512
D_SHARED = 512
EPS = 1e-6
NEG = -1e30
LOG2E = 1.4426950408889634
WINDOW = 128

TM = 256
NT = T // TM
NT_LAT = SEQ // TM
QB = 128
NQB = T // QB
NQB_LAT = SEQ // QB
BM = 256
NB = (2 * T + BM - 1) // BM + N_EXPERTS
ROWS = NB * BM
N1 = 128
N2 = 64
SUB = 8
RL = D_MODEL // SUB
PK = D_MODEL // 2 // SUB
VMEM_LIMIT = 56 * 1024 * 1024


def _cp(sem):
    return pltpu.CompilerParams(dimension_semantics=sem, vmem_limit_bytes=VMEM_LIMIT)


def _silu(a):
    return a * jax.nn.sigmoid(a)


@functools.lru_cache(maxsize=None)
def _tables():
    t = np.arange(SEQ)
    row = (t // GRID_W).astype(np.float64)
    col = (t % GRID_W).astype(np.float64)
    inv = ROPE_BASE ** (-np.arange(0, HEAD_DIM // 2, 2, dtype=np.float64) / (HEAD_DIM // 2))
    ar = row[:, None] * inv
    ac = col[:, None] * inv
    ang = np.concatenate([ar, ar, ac, ac], axis=-1)
    cos = np.concatenate([np.cos(ang), np.ones((CTX, HEAD_DIM))], axis=0)
    sin = np.concatenate([np.sin(ang), np.zeros((CTX, HEAD_DIM))], axis=0)
    first = (np.arange(HEAD_DIM) % 64) < 32
    rope_a = np.where(first[None, :], -sin, 0.0)
    rope_b = np.where(first[None, :], 0.0, sin)

    def cs(n):
        k = np.arange(n)
        a = 2.0 * np.pi * np.outer(k, k) / n
        return np.cos(a), np.sin(a)

    c64, s64 = cs(N2)
    c128, s128 = cs(N1)
    c256, s256 = cs(CTX)
    m1 = np.concatenate([c64, -s64], axis=0)
    m2 = np.concatenate([c128, s128], axis=0)
    mc = np.concatenate([c256, s256], axis=0)
    csch = np.concatenate([c128, s128], axis=0)
    a = 2.0 * np.pi * np.outer(np.arange(N1), np.arange(N2)) / SEQ
    twr = np.broadcast_to(np.cos(a)[:, :, None], (N1, N2, 128))
    twi = np.broadcast_to(-np.sin(a)[:, :, None], (N1, N2, 128))
    tri = (np.arange(TM)[:, None] < np.arange(TM)[None, :]).astype(np.float32)
    kpos = np.arange(3 * QB)[:, None] - QB
    qpos = (np.arange(Q_PER_KV * QB) % QB)[None, :]
    band = np.where(np.abs(qpos - kpos) <= WINDOW, 0.0, NEG)
    perm = np.zeros((GRP * SUB, GRP * SUB), np.float32)
    rr, ss = np.meshgrid(np.arange(GRP), np.arange(SUB), indexing="ij")
    perm[(ss * GRP + rr).ravel(), (rr * SUB + ss).ravel()] = 1.0
    return dict(
        cos=jnp.asarray(cos, F32), rope_a=jnp.asarray(rope_a, F32), rope_b=jnp.asarray(rope_b, F32),
        m1=jnp.asarray(m1, BF16), m2=jnp.asarray(m2, BF16), mc=jnp.asarray(mc, BF16),
        csch=jnp.asarray(csch, BF16), twr=jnp.asarray(twr, F32), twi=jnp.asarray(twi, F32),
        tri=jnp.asarray(tri, BF16), perm=jnp.asarray(perm, BF16), band=jnp.asarray(band, F32))


MOD_TN = 2048


def _mod_kernel(cc_ref, w_ref, b_ref, o_ref):
    a = _silu(cc_ref[...])
    o_ref[...] = jnp.dot(a.astype(BF16), w_ref[...].astype(BF16),
                         preferred_element_type=F32) + b_ref[...]


def _modulation(cc, w_mod, b_mod):
    nl = w_mod.shape[0]
    n6 = w_mod.shape[2]
    return pl.pallas_call(
        _mod_kernel,
        grid=(nl, n6 // MOD_TN),
        in_specs=[pl.BlockSpec((8, D_MODEL), lambda l, j: (0, 0)),
                  pl.BlockSpec((None, D_MODEL, MOD_TN), lambda l, j: (l, 0, j)),
                  pl.BlockSpec((None, 1, MOD_TN), lambda l, j: (l, 0, j))],
        out_specs=pl.BlockSpec((None, 8, MOD_TN), lambda l, j: (l, 0, j)),
        out_shape=jax.ShapeDtypeStruct((nl, 8, n6), F32),
        compiler_params=_cp(("arbitrary", "arbitrary")),
        name="modulation",
    )(cc, w_mod, b_mod.reshape(nl, 1, n6))


def _mod_spec(chunk):
    return pl.BlockSpec((None, None, 1, D_MODEL), lambda i: (i // NT_LAT, chunk, 0, 0))


def _row_spec(n):
    return pl.BlockSpec((1, n), lambda i: (0, 0))


PCW = 256


def _proj_kernel(x_ref, c_ref, g_ref, sc_ref, sh_ref, w_ref, gq_ref, gk_ref, cos_ref, ra_ref, rb_ref,
                 u_ref, q_ref, k_ref, v_ref, hb_ref, *, split):
    x = _stream_tile(x_ref, c_ref, split)
    ms = jnp.mean(x * x, axis=-1, keepdims=True)
    h = x * lax.rsqrt(ms + EPS) * g_ref[...]
    h = h * (1.0 + sc_ref[...]) + sh_ref[...]
    hb_ref[...] = h.astype(BF16)
    cos = cos_ref[...]
    ra = ra_ref[...]
    rb = rb_ref[...]

    def head(t, g):
        m = jnp.mean(t * t, axis=-1, keepdims=True)
        t = t * lax.rsqrt(m + EPS) * g
        return t * cos + pltpu.roll(t, 96, 1) * ra + pltpu.roll(t, 32, 1) * rb

    scale = HEAD_DIM ** -0.5 * LOG2E
    for c in range(D_IN // PCW):
        col = c * PCW
        p = jnp.dot(hb_ref[...], w_ref[:, col:col + PCW], preferred_element_type=F32)
        if col < D_F:
            u_ref[:, col:col + PCW] = p.astype(BF16)
        elif col < D_F + D_ATTN:
            o = col - D_F
            for j in range(PCW // HEAD_DIM):
                t = head(p[:, j * HEAD_DIM:(j + 1) * HEAD_DIM], gq_ref[...]) * scale
                q_ref[:, o + j * HEAD_DIM:o + (j + 1) * HEAD_DIM] = t.astype(BF16)
        elif col < D_F + D_ATTN + D_KV:
            o = col - D_F - D_ATTN
            for j in range(PCW // HEAD_DIM):
                t = head(p[:, j * HEAD_DIM:(j + 1) * HEAD_DIM], gk_ref[...])
                k_ref[:, o + j * HEAD_DIM:o + (j + 1) * HEAD_DIM] = t.astype(BF16)
        else:
            o = col - D_F - D_ATTN - D_KV
            v_ref[:, o:o + PCW] = p.astype(BF16)


def _layer_spec(l, a, b):
    return pl.BlockSpec((None, a, b), lambda i: (l, 0, 0))


def _project(l, xa, ca, c_blk, g1, modt, w_in_bf, gq, gk, tb):
    tile = lambda n: pl.BlockSpec((TM, n), lambda i: (i, 0))
    return pl.pallas_call(
        functools.partial(_proj_kernel, split=xa is not ca),
        grid=(NT,),
        in_specs=_stream_specs(xa is not ca, c_blk) + [
                  _row_spec(D_MODEL), _mod_spec(1), _mod_spec(0),
                  _layer_spec(l, D_MODEL, D_IN),
                  _row_spec(HEAD_DIM), _row_spec(HEAD_DIM),
                  tile(HEAD_DIM), tile(HEAD_DIM), tile(HEAD_DIM)],
        out_specs=[tile(D_F), tile(D_ATTN), tile(D_KV), tile(D_KV)],
        out_shape=[jax.ShapeDtypeStruct((T, D_F), BF16), jax.ShapeDtypeStruct((T, D_ATTN), BF16),
                   jax.ShapeDtypeStruct((T, D_KV), BF16), jax.ShapeDtypeStruct((T, D_KV), BF16)],
        scratch_shapes=[pltpu.VMEM((TM, D_MODEL), BF16)],
        compiler_params=_cp(("arbitrary",)),
        name="norm_proj",
    )(xa, ca, g1, modt, modt, w_in_bf, gq, gk, tb["cos"], tb["rope_a"], tb["rope_b"])


def _attn_kernel(sink_ref, q_ref, kp_ref, kc_ref, kn_ref, vp_ref, vc_ref, vn_ref, kx_ref, vx_ref,
                 ga_ref, band_ref, o_ref, a_scr):
    step = pl.program_id(0)
    for half in range(2):
        k_blocks = ([kp_ref, kc_ref.at[0:QB], kc_ref.at[QB:2 * QB]] if half == 0 else
                    [kc_ref.at[0:QB], kc_ref.at[QB:2 * QB], kn_ref])
        v_blocks = ([vp_ref, vc_ref.at[0:QB], vc_ref.at[QB:2 * QB]] if half == 0 else
                    [vc_ref.at[0:QB], vc_ref.at[QB:2 * QB], vn_ref])
        _attn_block(2 * step + half, sink_ref, q_ref.at[half * QB:(half + 1) * QB], k_blocks, v_blocks,
                    kx_ref, vx_ref, ga_ref, band_ref, o_ref.at[half * QB:(half + 1) * QB], a_scr.at[half])


def _attn_block(b, sink_ref, q_ref, k_blocks, v_blocks, kx_ref, vx_ref, ga_ref, band_ref, o_ref, a_scr):
    kp_ref, kc_ref, kn_ref = k_blocks
    vp_ref, vc_ref, vn_ref = v_blocks
    lat = b < NQB_LAT
    off = lambda ok: jnp.where(ok, 0.0, NEG)
    bias = jnp.concatenate(
        [band_ref[0:QB] + off(jnp.logical_and(lat, b > 0)),
         band_ref[QB:2 * QB] + off(lat),
         band_ref[2 * QB:3 * QB] + off(b < NQB_LAT - 1)], axis=0)
    dn = (((1,), (1,)), ((), ()))
    for hk in range(N_KV):
        ks = slice(hk * HEAD_DIM, (hk + 1) * HEAD_DIM)
        qs = jnp.concatenate(
            [q_ref[:, (hk * Q_PER_KV + g) * HEAD_DIM:(hk * Q_PER_KV + g + 1) * HEAD_DIM]
             for g in range(Q_PER_KV)], axis=0)
        kw = jnp.concatenate([kp_ref[:, ks], kc_ref[:, ks], kn_ref[:, ks]], axis=0)
        vw = jnp.concatenate([vp_ref[:, ks], vc_ref[:, ks], vn_ref[:, ks]], axis=0)
        st = lax.dot_general(kw, qs, dn, preferred_element_type=F32) + bias
        sxt = lax.dot_general(kx_ref[:, ks], qs, dn, preferred_element_type=F32)
        sk = jnp.concatenate(
            [jnp.full((1, QB), sink_ref[hk * Q_PER_KV + g] * LOG2E, F32) for g in range(Q_PER_KV)],
            axis=1)
        m = jnp.maximum(jnp.maximum(jnp.max(st, axis=0, keepdims=True),
                                    jnp.max(sxt, axis=0, keepdims=True)), sk)
        e = jnp.exp2(st - m)
        ex = jnp.exp2(sxt - m)
        den = (jnp.sum(e, axis=0, keepdims=True) + jnp.sum(ex, axis=0, keepdims=True)
               + jnp.exp2(sk - m))
        dt = (((0,), (0,)), ((), ()))
        ot = (lax.dot_general(vw, e.astype(BF16), dt, preferred_element_type=F32)
              + lax.dot_general(vx_ref[:, ks], ex.astype(BF16), dt, preferred_element_type=F32)) / den
        for g in range(Q_PER_KV):
            h = hk * Q_PER_KV + g
            a_scr[h * HEAD_DIM:(h + 1) * HEAD_DIM, :] = ot[:, g * QB:(g + 1) * QB]
    at = a_scr[...]
    ms = jnp.mean(at * at, axis=0, keepdims=True)
    o_ref[...] = (at * lax.rsqrt(ms + EPS) * ga_ref[...]).T.astype(BF16)


def _attention(q, k, v, sink, g_a, band):
    kb = lambda f: pl.BlockSpec((QB, D_KV), lambda j: (f(j), 0))
    prev = lambda j: jnp.maximum(2 * j - 1, 0)
    nxt = lambda j: jnp.minimum(2 * j + 2, NQB - 1)
    pair = pl.BlockSpec((2 * QB, D_KV), lambda j: (j, 0))
    g_cols = jnp.broadcast_to(g_a.reshape(D_ATTN, 1), (D_ATTN, QB))
    return pl.pallas_call(
        _attn_kernel,
        grid=(NQB // 2,),
        in_specs=[pl.BlockSpec(memory_space=pltpu.SMEM),
                  pl.BlockSpec((2 * QB, D_ATTN), lambda j: (j, 0)),
                  kb(prev), pair, kb(nxt), kb(prev), pair, kb(nxt),
                  pl.BlockSpec((CTX, D_KV), lambda b: (SEQ // CTX, 0)),
                  pl.BlockSpec((CTX, D_KV), lambda b: (SEQ // CTX, 0)),
                  pl.BlockSpec((D_ATTN, QB), lambda b: (0, 0)),
                  pl.BlockSpec((3 * QB, Q_PER_KV * QB), lambda b: (0, 0))],
        out_specs=pl.BlockSpec((2 * QB, D_ATTN), lambda j: (j, 0)),
        out_shape=jax.ShapeDtypeStruct((T, D_ATTN), BF16),
        scratch_shapes=[pltpu.VMEM((2, D_ATTN, QB), F32)],
        compiler_params=_cp(("arbitrary",)),
        name="attention",
    )(sink, q, k, k, k, v, v, v, k, v, g_cols, band)


F1_J = 8
F2_K = 4


def _f1_kernel(u_ref, m1_ref, twr_ref, twi_ref, z_ref):
    g = jnp.dot(m1_ref[...], u_ref[...], preferred_element_type=F32)
    for jl in range(F1_J):
        twr = twr_ref[jl]
        twi = twi_ref[jl]
        for lt in range(D_F // 128):
            c0 = jl * D_F + lt * 128
            gr = g[0:N2, c0:c0 + 128]
            gi = g[N2:2 * N2, c0:c0 + 128]
            z_ref[jl, 0:N2, lt * 128:(lt + 1) * 128] = (gr * twr - gi * twi).astype(BF16)
            z_ref[jl, N2:2 * N2, lt * 128:(lt + 1) * 128] = (gr * twi + gi * twr).astype(BF16)


def _f2_kernel(zr_ref, zi_ref, m2_ref, y_ref):
    pr = jnp.dot(m2_ref[...], zr_ref[...], preferred_element_type=F32)
    pi = jnp.dot(m2_ref[...], zi_ref[...], preferred_element_type=F32)
    yr = pr[0:N1] + pi[N1:2 * N1]
    yi = pi[0:N1] - pr[N1:2 * N1]
    for kl in range(F2_K):
        y_ref[:, kl * 2 * D_F:kl * 2 * D_F + D_F] = yr[:, kl * D_F:(kl + 1) * D_F].astype(BF16)
        y_ref[:, kl * 2 * D_F + D_F:(kl + 1) * 2 * D_F] = yi[:, kl * D_F:(kl + 1) * D_F].astype(BF16)


def _f3_kernel(y_ref, uc_ref, cs_ref, mc_ref, wf_ref, gf_ref, o_ref):
    i = pl.program_id(0)

    def finish(parts, scale):
        f = jnp.concatenate(parts, axis=1) * scale
        t = jnp.dot(f.astype(BF16), wf_ref[...], preferred_element_type=F32)
        ms = jnp.mean(t * t, axis=-1, keepdims=True)
        o_ref[...] = (t * lax.rsqrt(ms + EPS) * gf_ref[...]).astype(BF16)

    @pl.when(i < NT_LAT)
    def _():
        parts = []
        for g in range(N_FG):
            lhs = jnp.concatenate([y_ref[:, g * 128:(g + 1) * 128],
                                   y_ref[:, D_F + g * 128:D_F + (g + 1) * 128]], axis=1)
            parts.append(jnp.dot(lhs, cs_ref[...], preferred_element_type=F32))
        finish(parts, float((SEQ * 128) ** -0.5))

    @pl.when(i == NT_LAT)
    def _():
        pq = jnp.dot(mc_ref[...], uc_ref[...], preferred_element_type=F32)
        parts = []
        for g in range(N_FG):
            lhs = jnp.concatenate([pq[0:CTX, g * 128:(g + 1) * 128],
                                   -pq[CTX:2 * CTX, g * 128:(g + 1) * 128]], axis=1)
            parts.append(jnp.dot(lhs.astype(BF16), cs_ref[...], preferred_element_type=F32))
        finish(parts, float((CTX * 128) ** -0.5))


def _fourier(l, u, w_f_bf, g_f, tb):
    u2 = u.reshape(T // N1, N1 * D_F)
    z = pl.pallas_call(
        _f1_kernel,
        grid=(N1 // F1_J,),
        in_specs=[pl.BlockSpec((N2, F1_J * D_F), lambda c: (0, c)),
                  pl.BlockSpec((2 * N2, N2), lambda c: (0, 0)),
                  pl.BlockSpec((F1_J, N2, 128), lambda c: (c, 0, 0)),
                  pl.BlockSpec((F1_J, N2, 128), lambda c: (c, 0, 0))],
        out_specs=pl.BlockSpec((F1_J, 2 * N2, D_F), lambda c: (c, 0, 0)),
        out_shape=jax.ShapeDtypeStruct((N1, 2 * N2, D_F), BF16),
        compiler_params=_cp(("arbitrary",)),
        name="fourier_s1",
    )(u2, tb["m1"], tb["twr"], tb["twi"])
    z2 = z.reshape(N1, 2 * N2 * D_F)
    nk = N2 // F2_K
    y = pl.pallas_call(
        _f2_kernel,
        grid=(nk,),
        in_specs=[pl.BlockSpec((N1, F2_K * D_F), lambda k: (0, k)),
                  pl.BlockSpec((N1, F2_K * D_F), lambda k: (0, nk + k)),
                  pl.BlockSpec((2 * N1, N1), lambda k: (0, 0))],
        out_specs=pl.BlockSpec((N1, F2_K * 2 * D_F), lambda k: (0, k)),
        out_shape=jax.ShapeDtypeStruct((N1, N2 * 2 * D_F), BF16),
        compiler_params=_cp(("arbitrary",)),
        name="fourier_s2",
    )(z2, z2, tb["m2"])
    y2 = y.reshape(SEQ, 2 * D_F)
    return pl.pallas_call(
        _f3_kernel,
        grid=(NT,),
        in_specs=[pl.BlockSpec((TM, 2 * D_F), lambda i: (jnp.minimum(i, NT_LAT - 1), 0)),
                  pl.BlockSpec((CTX, D_F), lambda i: (SEQ // CTX, 0)),
                  pl.BlockSpec((2 * 128, 128), lambda i: (0, 0)),
                  pl.BlockSpec((2 * CTX, CTX), lambda i: (0, 0)),
                  _layer_spec(l, D_F, D_F),
                  _row_spec(D_F)],
        out_specs=pl.BlockSpec((TM, D_F), lambda i: (i, 0)),
        out_shape=jax.ShapeDtypeStruct((T, D_F), BF16),
        compiler_params=_cp(("arbitrary",)),
        name="fourier_s3",
    )(y2, u, tb["csch"], tb["mc"], w_f_bf, g_f)


def _merge_kernel(x_ref, c_ref, fn_ref, an_ref, wo_ref, ga1_ref, g2_ref, sc2_ref, sh2_ref, ga2_ref,
                  wsg_ref, wsu_ref, wsd_ref, wrh_ref, wrl_ref, br_ref, tri_ref,
                  xmid_ref, h2_ref, info_ref, wcol_ref, cnt_ref, carry_scr, *, split):
    i = pl.program_id(0)

    @pl.when(i == 0)
    def _():
        carry_scr[...] = jnp.zeros_like(carry_scr)

    m = (jnp.dot(fn_ref[...], wo_ref[0:D_F, :], preferred_element_type=F32)
         + jnp.dot(an_ref[...], wo_ref[D_F:D_MODEL, :], preferred_element_type=F32))
    x1 = _stream_tile(x_ref, c_ref, split) + ga1_ref[...] * m
    ms = jnp.mean(x1 * x1, axis=-1, keepdims=True)
    h = x1 * lax.rsqrt(ms + EPS) * g2_ref[...]
    h = h * (1.0 + sc2_ref[...]) + sh2_ref[...]
    hb = h.astype(BF16)
    hp = _pack_bf16_pairs(h)
    for s in range(SUB):
        h2_ref[:, s, :] = hp[:, s * PK:(s + 1) * PK]

    a = jnp.dot(hb, wsg_ref[...], preferred_element_type=F32)
    u = jnp.dot(hb, wsu_ref[...], preferred_element_type=F32)
    act = (_silu(a) * u).astype(BF16)
    ys = jnp.dot(act, wsd_ref[...], preferred_element_type=F32)
    xmid_ref[...] = x1 + ga2_ref[...] * ys

    hl = (h - hb.astype(F32)).astype(BF16)
    dn = (((1,), (1,)), ((), ()))
    both = lax.dot_general(jnp.concatenate([wrh_ref[...], wrl_ref[...]], axis=0), hb, dn,
                           preferred_element_type=F32)
    lg = both[0:N_EXPERTS] + (both[N_EXPERTS:2 * N_EXPERTS]
                              + lax.dot_general(wrh_ref[...], hl, dn, preferred_element_type=F32))
    score = jax.nn.sigmoid(lg)
    sel = score + br_ref[...]
    s = [sel[jj * N_GROUPS:(jj + 1) * N_GROUPS] for jj in range(EPG)]
    sr = [score[jj * N_GROUPS:(jj + 1) * N_GROUPS] for jj in range(EPG)]
    hi01, lo01 = jnp.maximum(s[0], s[1]), jnp.minimum(s[0], s[1])
    hi23, lo23 = jnp.maximum(s[2], s[3]), jnp.minimum(s[2], s[3])
    gscore = jnp.maximum(hi01, hi23) + jnp.maximum(jnp.minimum(hi01, hi23), jnp.maximum(lo01, lo23))
    gi = lax.broadcasted_iota(I32, (N_GROUPS, TM), 0)
    gmax = jnp.max(gscore, axis=0, keepdims=True)
    g_idx = jnp.min(jnp.where(gscore == gmax, gi, N_GROUPS), axis=0, keepdims=True)
    gsel = gi == g_idx
    v = [jnp.sum(jnp.where(gsel, s[jj], 0.0), axis=0, keepdims=True) for jj in range(EPG)]
    vr = [jnp.sum(jnp.where(gsel, sr[jj], 0.0), axis=0, keepdims=True) for jj in range(EPG)]

    def first_argmax(vals):
        best = vals[0]
        idx = jnp.zeros((1, TM), I32)
        for jj in range(1, EPG):
            upd = vals[jj] > best
            best = jnp.where(upd, vals[jj], best)
            idx = jnp.where(upd, jj, idx)
        return idx

    i1 = first_argmax(v)
    i2 = first_argmax([jnp.where(i1 == jj, -jnp.inf, v[jj]) for jj in range(EPG)])

    def pick(vals, idx):
        out = vals[0]
        for jj in range(1, EPG):
            out = jnp.where(idx == jj, vals[jj], out)
        return out

    w1 = pick(vr, i1)
    w2 = pick(vr, i2)
    wsum = w1 + w2
    w1 = w1 / wsum
    w2 = w2 / wsum

    ri = lax.broadcasted_iota(I32, (N_EXPERTS, TM), 0)
    oh1 = ri == i1 * N_GROUPS + g_idx
    oh2 = ri == i2 * N_GROUPS + g_idx
    oh = jnp.logical_or(oh1, oh2).astype(F32)
    pre = jnp.dot(oh.astype(BF16), tri_ref[...], preferred_element_type=F32) + carry_scr[:, 0:1]
    r1 = jnp.sum(jnp.where(oh1, pre, 0.0), axis=0, keepdims=True)
    r2 = jnp.sum(jnp.where(oh2, pre, 0.0), axis=0, keepdims=True)
    carry = carry_scr[...] + jnp.sum(oh, axis=1, keepdims=True)
    carry_scr[...] = carry
    cnt_ref[...] = carry

    zi = jnp.zeros((4, TM), I32)
    info_ref[...] = jnp.concatenate(
        [g_idx * EPG + i1, g_idx * EPG + i2, r1.astype(I32), r2.astype(I32), zi], axis=0)
    wrow = jnp.concatenate([w1, w2, jnp.zeros((126, TM), F32)], axis=0)
    wcol_ref[...] = wrow.T


def _stream_specs(split, c_blk):
    if split:
        return [pl.BlockSpec((TM, D_MODEL), lambda i: (jnp.minimum(i, NT_LAT - 1), 0)),
                pl.BlockSpec((CTX, D_MODEL), lambda i: (c_blk, 0))]
    return [pl.BlockSpec((TM, D_MODEL), lambda i: (i, 0)),
            pl.BlockSpec((8, D_MODEL), lambda i: (0, 0))]


def _stream_tile(x_ref, c_ref, split):
    if split:
        return jnp.where(pl.program_id(0) == NT_LAT, c_ref[...], x_ref[...])
    return x_ref[...]


def _merge(l, xa, ca, c_blk, fn, an, w_out_bf, modt, g2, wsg, wsu, wsd, wrh, wrl, br, tb):
    tile = lambda n: pl.BlockSpec((TM, n), lambda i: (i, 0))
    full = lambda a, b: pl.BlockSpec((a, b), lambda i: (0, 0))
    return pl.pallas_call(
        functools.partial(_merge_kernel, split=xa is not ca),
        grid=(NT,),
        in_specs=_stream_specs(xa is not ca, c_blk) + [
                  tile(D_F), tile(D_ATTN), _layer_spec(l, D_MODEL, D_MODEL),
                  _mod_spec(2), _row_spec(D_MODEL), _mod_spec(4), _mod_spec(3), _mod_spec(5),
                  _layer_spec(l, D_MODEL, D_SHARED), _layer_spec(l, D_MODEL, D_SHARED),
                  _layer_spec(l, D_SHARED, D_MODEL),
                  full(N_EXPERTS, D_MODEL), full(N_EXPERTS, D_MODEL), full(N_EXPERTS, TM),
                  full(TM, TM)],
        out_specs=[tile(D_MODEL), pl.BlockSpec((TM, SUB, PK), lambda i: (i, 0, 0)),
                   pl.BlockSpec((8, TM), lambda i: (0, i)),
                   pl.BlockSpec((TM, 128), lambda i: (i, 0)),
                   pl.BlockSpec((N_EXPERTS, 128), lambda i: (0, 0))],
        out_shape=[jax.ShapeDtypeStruct((T, D_MODEL), F32), jax.ShapeDtypeStruct((T, SUB, PK), I32),
                   jax.ShapeDtypeStruct((8, T), I32), jax.ShapeDtypeStruct((T, 128), F32),
                   jax.ShapeDtypeStruct((N_EXPERTS, 128), F32)],
        scratch_shapes=[pltpu.VMEM((N_EXPERTS, 128), F32)],
        compiler_params=_cp(("arbitrary",)),
        name="merge_route",
    )(xa, ca, fn, an, w_out_bf, modt, g2, modt, modt, modt, wsg, wsu, wsd, wrh, wrl, br, tb["tri"])


def _dispatch_kernel(dest_ref, pend_ref, padded_ref, h_ref, xb_hbm, zero_scr, slab, sem, zsem, *, n):
    i = pl.program_id(0)
    slot = i % 2

    def zero_copy(row0):
        return pltpu.make_async_copy(zero_scr, xb_hbm.at[pl.ds(row0, BM)], zsem)

    @pl.when(i == 0)
    def _():
        zero_scr[...] = jnp.zeros_like(zero_scr)
        n_used = pend_ref[N_EXPERTS - 1] // BM
        for e in range(N_EXPERTS):
            @pl.when(padded_ref[e] > 0)
            def _():
                zero_copy(pend_ref[e] - BM).start()
        for blk in range(NB - N_EXPERTS, NB):
            @pl.when(blk >= n_used)
            def _():
                zero_copy(blk * BM).start()
        for e in range(N_EXPERTS):
            @pl.when(padded_ref[e] > 0)
            def _():
                zero_copy(0).wait()
        for blk in range(NB - N_EXPERTS, NB):
            @pl.when(blk >= n_used)
            def _():
                zero_copy(0).wait()

    slab[slot] = h_ref[...]

    def copy(sl, r, d):
        return pltpu.make_async_copy(slab.at[sl, pl.ds(r, 1)], xb_hbm.at[pl.ds(d, 1)], sem.at[sl])

    def drain(sl):
        def body(r, c):
            copy(sl, 0, 0).wait()
            copy(sl, 0, 0).wait()
            return c
        lax.fori_loop(0, TM, body, 0)

    def issue(r, c):
        t = i * TM + r
        copy(slot, r, dest_ref[2 * t]).start()
        copy(slot, r, dest_ref[2 * t + 1]).start(priority=1)
        return c

    lax.fori_loop(0, TM, issue, 0)

    @pl.when(i > 0)
    def _():
        drain(1 - slot)

    @pl.when(i == n - 1)
    def _():
        drain(slot)


def _dispatch(dest, pad_end, padded, h2):
    return pl.pallas_call(
        functools.partial(_dispatch_kernel, n=NT),
        grid_spec=pltpu.PrefetchScalarGridSpec(
            num_scalar_prefetch=3, grid=(NT,),
            in_specs=[pl.BlockSpec((TM, SUB, PK), lambda i, d, pe, pd: (i, 0, 0))],
            out_specs=pl.BlockSpec(memory_space=pl.ANY),
            scratch_shapes=[pltpu.VMEM((BM, SUB, PK), I32), pltpu.VMEM((2, TM, SUB, PK), I32),
                            pltpu.SemaphoreType.DMA((2,)), pltpu.SemaphoreType.DMA(())]),
        out_shape=jax.ShapeDtypeStruct((ROWS, SUB, PK), I32),
        compiler_params=pltpu.CompilerParams(dimension_semantics=("arbitrary",),
                                             has_side_effects=True, vmem_limit_bytes=VMEM_LIMIT),
        name="dispatch",
    )(dest, pad_end, padded, h2)


def _pack_bf16_pairs(h):
    bits = lax.bitcast_convert_type(h.astype(BF16).astype(F32), I32)
    half = h.shape[1] // 2
    return (bits[:, :half] & -65536) | lax.shift_right_logical(bits[:, half:], 16)


GRP = 32


def _expert_kernel(eo_ref, es_ref, n_ref, x_ref, perm_ref, wg_hbm, wu_hbm, wd_hbm, y_ref,
                   wgf, wuf, wdf, wgb, wub, wdb, xs_scr, wsem, *, layer):
    b = pl.program_id(0)
    k = eo_ref[b]
    first = jnp.logical_or(b == 0, k != eo_ref[jnp.maximum(b - 1, 0)])

    def copies(kk, slot):
        e = es_ref[kk]
        return (pltpu.make_async_copy(wg_hbm.at[layer, e], wgf.at[slot], wsem.at[slot, 0]),
                pltpu.make_async_copy(wu_hbm.at[layer, e], wuf.at[slot], wsem.at[slot, 1]),
                pltpu.make_async_copy(wd_hbm.at[layer, e], wdf.at[slot], wsem.at[slot, 2]))

    @pl.when(b == 0)
    def _():
        for c in copies(0, 0):
            c.start()

    @pl.when(jnp.logical_and(first, b < n_ref[0]))
    def _():
        slot = k % 2
        for c in copies(k, slot):
            c.wait()

        @pl.when(k + 1 < n_ref[1])
        def _():
            for c in copies(k + 1, 1 - slot):
                c.start()

        wgb[...] = wgf[slot].astype(BF16)
        wub[...] = wuf[slot].astype(BF16)
        wdb[...] = wdf[slot].astype(BF16)

    @pl.when(b < n_ref[0])
    def _():
        half = D_MODEL // 2
        for grp in range(BM // GRP):
            w = x_ref[grp * GRP:(grp + 1) * GRP].reshape(GRP * SUB, PK)
            hi = lax.bitcast_convert_type(w & -65536, F32).astype(BF16)
            lo = lax.bitcast_convert_type(w << 16, F32).astype(BF16)
            g = jnp.concatenate([hi, lo], axis=1)
            r = jnp.dot(perm_ref[...], g, preferred_element_type=F32).astype(BF16)
            rows = slice(grp * GRP, (grp + 1) * GRP)
            for s in range(SUB):
                xs_scr[rows, s * PK:(s + 1) * PK] = r[s * GRP:(s + 1) * GRP, 0:PK]
                xs_scr[rows, half + s * PK:half + (s + 1) * PK] = r[s * GRP:(s + 1) * GRP, PK:2 * PK]
        x = xs_scr[...]
        a = jnp.dot(x, wgb[...], preferred_element_type=F32)
        u = jnp.dot(x, wub[...], preferred_element_type=F32)
        act = (_silu(a) * u).astype(BF16)
        yp = _pack_bf16_pairs(jnp.dot(act, wdb[...], preferred_element_type=F32))
        for s in range(SUB):
            y_ref[:, s, :] = yp[:, s * PK:(s + 1) * PK]

    @pl.when(b >= n_ref[0])
    def _():
        y_ref[...] = jnp.zeros_like(y_ref)


def _experts(l, eo, es, nn, xb, perm, w_g, w_u, w_d):
    hbm = pl.BlockSpec(memory_space=pl.ANY)
    return pl.pallas_call(
        functools.partial(_expert_kernel, layer=l),
        grid_spec=pltpu.PrefetchScalarGridSpec(
            num_scalar_prefetch=3, grid=(NB,),
            in_specs=[pl.BlockSpec((BM, SUB, PK), lambda b, eo, es, n: (jnp.minimum(b, n[0] - 1), 0, 0)),
                      pl.BlockSpec((GRP * SUB, GRP * SUB), lambda b, eo, es, n: (0, 0)),
                      hbm, hbm, hbm],
            out_specs=pl.BlockSpec((BM, SUB, PK), lambda b, eo, es, n: (b, 0, 0)),
            scratch_shapes=[pltpu.VMEM((2, D_MODEL, D_EXPERT), F32),
                            pltpu.VMEM((2, D_MODEL, D_EXPERT), F32),
                            pltpu.VMEM((2, D_EXPERT, D_MODEL), F32),
                            pltpu.VMEM((D_MODEL, D_EXPERT), BF16),
                            pltpu.VMEM((D_MODEL, D_EXPERT), BF16),
                            pltpu.VMEM((D_EXPERT, D_MODEL), BF16),
                            pltpu.VMEM((BM, D_MODEL), BF16),
                            pltpu.SemaphoreType.DMA((2, 3))]),
        out_shape=jax.ShapeDtypeStruct((ROWS, SUB, PK), I32),
        compiler_params=_cp(("arbitrary",)),
        name="experts",
    )(eo, es, nn, xb, perm, w_g, w_u, w_d)


def _combine_kernel(dest_ref, y_hbm, xmid_ref, wcol_ref, ga2_ref, perm_ref, o_ref, b1, b2, sem, *, n):
    i = pl.program_id(0)

    def copy(d, buf, slot, r):
        return pltpu.make_async_copy(y_hbm.at[pl.ds(d, 1)], buf.at[slot, pl.ds(r, 1)], sem.at[slot])

    def issue(tile, slot):
        def body(r, c):
            t = tile * TM + r
            copy(dest_ref[2 * t], b1, slot, r).start()
            copy(dest_ref[2 * t + 1], b2, slot, r).start(priority=1)
            return c
        lax.fori_loop(0, TM, body, 0)

    @pl.when(i == 0)
    def _():
        issue(0, 0)

    @pl.when(i + 1 < n)
    def _():
        issue(i + 1, (i + 1) % 2)

    slot = i % 2

    def drain(r, c):
        copy(0, b1, slot, 0).wait()
        copy(0, b2, slot, 0).wait()
        return c

    lax.fori_loop(0, TM, drain, 0)

    def unpack_group(buf, grp):
        w = buf[slot, grp * GRP:(grp + 1) * GRP].reshape(GRP * SUB, PK)
        hi = lax.bitcast_convert_type(w & -65536, F32).astype(BF16)
        lo = lax.bitcast_convert_type(w << 16, F32).astype(BF16)
        return jnp.dot(perm_ref[...], jnp.concatenate([hi, lo], axis=1), preferred_element_type=F32)

    half = D_MODEL // 2
    for grp in range(TM // GRP):
        rows = slice(grp * GRP, (grp + 1) * GRP)
        w1 = wcol_ref[rows, 0:1]
        w2 = wcol_ref[rows, 1:2]
        r1 = unpack_group(b1, grp)
        r2 = unpack_group(b2, grp)
        for s in range(SUB):
            sr = slice(s * GRP, (s + 1) * GRP)
            for c0, lanes in ((s * PK, slice(0, PK)), (half + s * PK, slice(PK, 2 * PK))):
                cs = slice(c0, c0 + PK)
                o_ref[rows, cs] = xmid_ref[rows, cs] + ga2_ref[:, cs] * (w1 * r1[sr, lanes] + w2 * r2[sr, lanes])


def _combine(dest, yb, xmid, wcol, modt, perm, n_tiles):
    tile = lambda n: pl.BlockSpec((TM, n), lambda i, d: (i, 0))
    return pl.pallas_call(
        functools.partial(_combine_kernel, n=n_tiles),
        grid_spec=pltpu.PrefetchScalarGridSpec(
            num_scalar_prefetch=1, grid=(n_tiles,),
            in_specs=[pl.BlockSpec(memory_space=pl.ANY), tile(D_MODEL), tile(128),
                      pl.BlockSpec((None, None, 1, D_MODEL), lambda i, d: (i // NT_LAT, 5, 0, 0)),
                      pl.BlockSpec((GRP * SUB, GRP * SUB), lambda i, d: (0, 0))],
            out_specs=tile(D_MODEL),
            scratch_shapes=[pltpu.VMEM((2, TM, SUB, PK), I32), pltpu.VMEM((2, TM, SUB, PK), I32),
                            pltpu.SemaphoreType.DMA((2,))]),
        out_shape=jax.ShapeDtypeStruct((n_tiles * TM, D_MODEL), F32),
        compiler_params=_cp(("arbitrary",)),
        name="combine",
    )(dest, yb, xmid, wcol, modt, perm)


def _perm_experts(a):
    return a.reshape(a.shape[:-1] + (N_GROUPS, EPG)).swapaxes(-1, -2).reshape(a.shape)


def kernel(x, c, ctx, c_ctx, w_mod, b_mod, g_norm1, g_norm2, w_in, w_fourier, g_q, g_k, sink,
           g_branch_f, g_branch_a, w_out, w_router, b_router, w_exp_gate, w_exp_up, w_exp_down,
           w_sh_gate, w_sh_up, w_sh_down):
    assert x.shape == (1, SEQ, D_MODEL) and ctx.shape == (1, CTX, D_MODEL)
    tb = _tables()
    stream = (x[0], ctx[0], 0)

    cc = jnp.zeros((8, D_MODEL), F32).at[0].set(c[0]).at[1].set(c_ctx)
    mods = _modulation(cc, w_mod, b_mod)

    wr_t = _perm_experts(w_router).T
    wrh = wr_t.astype(BF16)
    wrl = (wr_t - wrh.astype(F32)).astype(BF16)
    br = jnp.broadcast_to(_perm_experts(b_router)[:, None], (N_EXPERTS, TM)).astype(F32)

    w_in_b, w_f_b, w_out_b = w_in.astype(BF16), w_fourier.astype(BF16), w_out.astype(BF16)
    w_sg_b, w_su_b, w_sd_b = w_sh_gate.astype(BF16), w_sh_up.astype(BF16), w_sh_down.astype(BF16)

    for l in range(DEPTH):
        last = l == DEPTH - 1
        modt = mods[l, 0:2].reshape(2, 6, 1, D_MODEL)
        row = lambda a: a[l].reshape(1, -1)
        u, q, k, v = _project(l, *stream, row(g_norm1), modt, w_in_b, row(g_q), row(g_k), tb)
        an = _attention(q, k, v, sink[l], row(g_branch_a), tb["band"])
        fn = _fourier(l, u, w_f_b, row(g_branch_f), tb)
        xmid, h2, info, wcol, cnt = _merge(
            l, *stream, fn, an, w_out_b, modt, row(g_norm2), w_sg_b, w_su_b, w_sd_b,
            wrh, wrl, br, tb)

        xs = _routed(l, info, cnt, h2, xmid, wcol, modt, w_exp_gate, w_exp_up, w_exp_down, tb,
                     NT_LAT if last else NT)
        stream = (xs, xs, NT_LAT)
    return xs[None]


def _routed(l, info, cnt, h2, xmid, wcol, modt, w_g, w_u, w_d, tb, n_tiles):
    counts = cnt[:, 0].astype(I32).reshape(EPG, N_GROUPS).T.reshape(N_EXPERTS)
    padded = ((counts + BM - 1) // BM) * BM
    pad_end = jnp.cumsum(padded).astype(I32)
    pad_start = pad_end - padded
    start = jnp.sum(jnp.where(info[0:2, :, None] == jnp.arange(N_EXPERTS, dtype=I32),
                              pad_start, 0), axis=-1)
    dest = (start + info[2:4]).T.reshape(2 * T)
    ids = jnp.arange(N_EXPERTS, dtype=I32)
    block_row0 = jnp.arange(NB, dtype=I32)[:, None] * BM
    block_e = jnp.minimum(jnp.sum((pad_end <= block_row0).astype(I32), axis=-1), N_EXPERTS - 1)
    used = padded > 0
    ordinal = jnp.cumsum(used.astype(I32)) - 1
    eo = jnp.sum(jnp.where(block_e[:, None] == ids, ordinal, 0), axis=-1)
    es = jnp.sum(jnp.where(jnp.logical_and(used, ordinal == ids[:, None]), ids, 0), axis=-1)
    nn = jnp.stack([pad_end[N_EXPERTS - 1] // BM, ordinal[N_EXPERTS - 1] + 1]).astype(I32)
    xb = _dispatch(dest, pad_end, padded, h2)
    yb = _experts(l, eo.astype(I32), es.astype(I32), nn, xb, tb["perm"], w_g, w_u, w_d)
    return _combine(dest, yb, xmid, wcol, modt, tb["perm"], n_tiles)
```

```python
import functools

import numpy as np
import jax
import jax.numpy as jnp
from jax import lax
from jax.experimental import pallas as pl
from jax.experimental.pallas import tpu as pltpu

F32 = jnp.float32
BF16 = jnp.bfloat16
I32 = jnp.int32

D_MODEL = 2048
SEQ = 8192
CTX = 256
T = SEQ + CTX
DEPTH = 4
GRID_W = 64
HEAD_DIM = 128
D_F = 512
N_FG = 4
N_HEADS = 12
N_KV = 4
Q_PER_KV = 3
D_ATTN = N_HEADS * HEAD_DIM
D_KV = N_KV * HEAD_DIM
D_IN = D_F + D_ATTN + 2 * D_KV
ROPE_BASE = 10000.0
N_EXPERTS = 32
N_GROUPS = 8
EPG = 4
D_EXPERT = 512
D_SHARED = 512
EPS = 1e-6
NEG = -1e30
LOG2E = 1.4426950408889634
WINDOW = 128

TM = 256
NT = T // TM
NT_LAT = SEQ // TM
QB = 128
AQ = 3
NQB = T // QB
NQB_LAT = SEQ // QB
BM = 256
NB = (2 * T + BM - 1) // BM + N_EXPERTS
ROWS = NB * BM
N1 = 128
N2 = 64
SUB = 8
RL = D_MODEL // SUB
PK = D_MODEL // 2 // SUB
VMEM_LIMIT = 56 * 1024 * 1024


def _cp(sem):
    return pltpu.CompilerParams(dimension_semantics=sem, vmem_limit_bytes=VMEM_LIMIT)


def _silu(a):
    return a * jax.nn.sigmoid(a)


@functools.lru_cache(maxsize=None)
def _tables():
    t = np.arange(SEQ)
    row = (t // GRID_W).astype(np.float64)
    col = (t % GRID_W).astype(np.float64)
    inv = ROPE_BASE ** (-np.arange(0, HEAD_DIM // 2, 2, dtype=np.float64) / (HEAD_DIM // 2))
    ar = row[:, None] * inv
    ac = col[:, None] * inv
    ang = np.concatenate([ar, ar, ac, ac], axis=-1)
    cos = np.concatenate([np.cos(ang), np.ones((CTX, HEAD_DIM))], axis=0)
    sin = np.concatenate([np.sin(ang), np.zeros((CTX, HEAD_DIM))], axis=0)
    first = (np.arange(HEAD_DIM) % 64) < 32
    rope_a = np.where(first[None, :], -sin, 0.0)
    rope_b = np.where(first[None, :], 0.0, sin)

    def cs(n):
        k = np.arange(n)
        a = 2.0 * np.pi * np.outer(k, k) / n
        return np.cos(a), np.sin(a)

    c64, s64 = cs(N2)
    c128, s128 = cs(N1)
    c256, s256 = cs(CTX)
    m1 = np.concatenate([c64, -s64], axis=0)
    m2 = np.concatenate([c128, s128], axis=0)
    mc = np.concatenate([c256, s256], axis=0)
    csch = np.concatenate([c128, s128], axis=0)
    a = 2.0 * np.pi * np.outer(np.arange(N1), np.arange(N2)) / SEQ
    twr = np.broadcast_to(np.cos(a)[:, :, None], (N1, N2, 128))
    twi = np.broadcast_to(-np.sin(a)[:, :, None], (N1, N2, 128))
    tri = (np.arange(TM)[:, None] < np.arange(TM)[None, :]).astype(np.float32)
    kpos = np.arange(3 * QB)[:, None] - QB
    qpos = (np.arange(Q_PER_KV * QB) % QB)[None, :]
    band = np.where(np.abs(qpos - kpos) <= WINDOW, 0.0, NEG)
    perm = np.zeros((GRP * SUB, GRP * SUB), np.float32)
    rr, ss = np.meshgrid(np.arange(GRP), np.arange(SUB), indexing="ij")
    perm[(ss * GRP + rr).ravel(), (rr * SUB + ss).ravel()] = 1.0
    return dict(
        cos=jnp.asarray(cos, F32), rope_a=jnp.asarray(rope_a, F32), rope_b=jnp.asarray(rope_b, F32),
        m1=jnp.asarray(m1, BF16), m2=jnp.asarray(m2, BF16), mc=jnp.asarray(mc, BF16),
        csch=jnp.asarray(csch, BF16), twr=jnp.asarray(twr, F32), twi=jnp.asarray(twi, F32),
        tri=jnp.asarray(tri, BF16), perm=jnp.asarray(perm, BF16), band=jnp.asarray(band, F32))


MOD_TN = 1024


def _mod_kernel(cc_ref, w_ref, b_ref, o_ref):
    a = _silu(cc_ref[...])
    o_ref[...] = jnp.dot(a.astype(BF16), w_ref[...].astype(BF16),
                         preferred_element_type=F32) + b_ref[...]


def _modulation(cc, w_mod, b_mod):
    nl = w_mod.shape[0]
    n6 = w_mod.shape[2]
    return pl.pallas_call(
        _mod_kernel,
        grid=(nl, n6 // MOD_TN),
        in_specs=[pl.BlockSpec((8, D_MODEL), lambda l, j: (0, 0)),
                  pl.BlockSpec((None, D_MODEL, MOD_TN), lambda l, j: (l, 0, j)),
                  pl.BlockSpec((None, 1, MOD_TN), lambda l, j: (l, 0, j))],
        out_specs=pl.BlockSpec((None, 8, MOD_TN), lambda l, j: (l, 0, j)),
        out_shape=jax.ShapeDtypeStruct((nl, 8, n6), F32),
        compiler_params=_cp(("arbitrary", "arbitrary")),
        name="modulation",
    )(cc, w_mod, b_mod.reshape(nl, 1, n6))


def _mod_spec(chunk):
    return pl.BlockSpec((None, None, 1, D_MODEL), lambda i: (i // NT_LAT, chunk, 0, 0))


def _row_spec(n):
    return pl.BlockSpec((1, n), lambda i: (0, 0))


PCW = 256


def _proj_kernel(x_ref, c_ref, g_ref, sc_ref, sh_ref, w_ref, gq_ref, gk_ref, cos_ref, ra_ref, rb_ref,
                 u_ref, q_ref, k_ref, v_ref, hb_ref, *, split):
    x = _stream_tile(x_ref, c_ref, split)
    ms = jnp.mean(x * x, axis=-1, keepdims=True)
    h = x * lax.rsqrt(ms + EPS) * g_ref[...]
    h = h * (1.0 + sc_ref[...]) + sh_ref[...]
    hb_ref[...] = h.astype(BF16)
    cos = cos_ref[...]
    ra = ra_ref[...]
    rb = rb_ref[...]

    def head(t, g):
        m = jnp.mean(t * t, axis=-1, keepdims=True)
        t = t * lax.rsqrt(m + EPS) * g
        return t * cos + pltpu.roll(t, 96, 1) * ra + pltpu.roll(t, 32, 1) * rb

    scale = HEAD_DIM ** -0.5 * LOG2E
    for c in range(D_IN // PCW):
        col = c * PCW
        p = jnp.dot(hb_ref[...], w_ref[:, col:col + PCW], preferred_element_type=F32)
        if col < D_F:
            u_ref[:, col:col + PCW] = p.astype(BF16)
        elif col < D_F + D_ATTN:
            o = col - D_F
            for j in range(PCW // HEAD_DIM):
                t = head(p[:, j * HEAD_DIM:(j + 1) * HEAD_DIM], gq_ref[...]) * scale
                q_ref[:, o + j * HEAD_DIM:o + (j + 1) * HEAD_DIM] = t.astype(BF16)
        elif col < D_F + D_ATTN + D_KV:
            o = col - D_F - D_ATTN
            for j in range(PCW // HEAD_DIM):
                t = head(p[:, j * HEAD_DIM:(j + 1) * HEAD_DIM], gk_ref[...])
                k_ref[:, o + j * HEAD_DIM:o + (j + 1) * HEAD_DIM] = t.astype(BF16)
        else:
            o = col - D_F - D_ATTN - D_KV
            v_ref[:, o:o + PCW] = p.astype(BF16)


def _layer_spec(l, a, b):
    return pl.BlockSpec((None, a, b), lambda i: (l, 0, 0))


def _project(l, xa, ca, c_blk, g1, modt, w_in_bf, gq, gk, tb):
    tile = lambda n: pl.BlockSpec((TM, n), lambda i: (i, 0))
    return pl.pallas_call(
        functools.partial(_proj_kernel, split=xa is not ca),
        grid=(NT,),
        in_specs=_stream_specs(xa is not ca, c_blk) + [
                  _row_spec(D_MODEL), _mod_spec(1), _mod_spec(0),
                  _layer_spec(l, D_MODEL, D_IN),
                  _row_spec(HEAD_DIM), _row_spec(HEAD_DIM),
                  tile(HEAD_DIM), tile(HEAD_DIM), tile(HEAD_DIM)],
        out_specs=[tile(D_F), tile(D_ATTN), tile(D_KV), tile(D_KV)],
        out_shape=[jax.ShapeDtypeStruct((T, D_F), BF16), jax.ShapeDtypeStruct((T, D_ATTN), BF16),
                   jax.ShapeDtypeStruct((T, D_KV), BF16), jax.ShapeDtypeStruct((T, D_KV), BF16)],
        scratch_shapes=[pltpu.VMEM((TM, D_MODEL), BF16)],
        compiler_params=_cp(("arbitrary",)),
        name="norm_proj",
    )(xa, ca, g1, modt, modt, w_in_bf, gq, gk, tb["cos"], tb["rope_a"], tb["rope_b"])


def _attn_kernel(sink_ref, q_ref, kp_ref, kc_ref, kn_ref, vp_ref, vc_ref, vn_ref, kx_ref, vx_ref,
                 ga_ref, band_ref, o_ref, a_scr):
    step = pl.program_id(0)
    k_all = [kp_ref] + [kc_ref.at[i * QB:(i + 1) * QB] for i in range(AQ)] + [kn_ref]
    v_all = [vp_ref] + [vc_ref.at[i * QB:(i + 1) * QB] for i in range(AQ)] + [vn_ref]
    for part in range(AQ):
        rows = slice(part * QB, (part + 1) * QB)
        _attn_block(AQ * step + part, sink_ref, q_ref.at[rows], k_all[part:part + 3], v_all[part:part + 3],
                    kx_ref, vx_ref, ga_ref, band_ref, o_ref.at[rows], a_scr.at[part])


def _attn_block(b, sink_ref, q_ref, k_blocks, v_blocks, kx_ref, vx_ref, ga_ref, band_ref, o_ref, a_scr):
    kp_ref, kc_ref, kn_ref = k_blocks
    vp_ref, vc_ref, vn_ref = v_blocks
    lat = b < NQB_LAT
    off = lambda ok: jnp.where(ok, 0.0, NEG)
    bias = jnp.concatenate(
        [band_ref[0:QB] + off(jnp.logical_and(lat, b > 0)),
         band_ref[QB:2 * QB] + off(lat),
         band_ref[2 * QB:3 * QB] + off(b < NQB_LAT - 1)], axis=0)
    dn = (((1,), (1,)), ((), ()))
    for hk in range(N_KV):
        ks = slice(hk * HEAD_DIM, (hk + 1) * HEAD_DIM)
        qs = jnp.concatenate(
            [q_ref[:, (hk * Q_PER_KV + g) * HEAD_DIM:(hk * Q_PER_KV + g + 1) * HEAD_DIM]
             for g in range(Q_PER_KV)], axis=0)
        kw = jnp.concatenate([kp_ref[:, ks], kc_ref[:, ks], kn_ref[:, ks]], axis=0)
        vw = jnp.concatenate([vp_ref[:, ks], vc_ref[:, ks], vn_ref[:, ks]], axis=0)
        st = lax.dot_general(kw, qs, dn, preferred_element_type=F32) + bias
        sxt = lax.dot_general(kx_ref[:, ks], qs, dn, preferred_element_type=F32)
        sk = jnp.concatenate(
            [jnp.full((1, QB), sink_ref[hk * Q_PER_KV + g] * LOG2E, F32) for g in range(Q_PER_KV)],
            axis=1)
        m = jnp.maximum(jnp.maximum(jnp.max(st, axis=0, keepdims=True),
                                    jnp.max(sxt, axis=0, keepdims=True)), sk)
        e = jnp.exp2(st - m)
        ex = jnp.exp2(sxt - m)
        den = (jnp.sum(e, axis=0, keepdims=True) + jnp.sum(ex, axis=0, keepdims=True)
               + jnp.exp2(sk - m))
        dt = (((0,), (0,)), ((), ()))
        ot = (lax.dot_general(vw, e.astype(BF16), dt, preferred_element_type=F32)
              + lax.dot_general(vx_ref[:, ks], ex.astype(BF16), dt, preferred_element_type=F32)) / den
        for g in range(Q_PER_KV):
            h = hk * Q_PER_KV + g
            a_scr[h * HEAD_DIM:(h + 1) * HEAD_DIM, :] = ot[:, g * QB:(g + 1) * QB]
    at = a_scr[...]
    ms = jnp.mean(at * at, axis=0, keepdims=True)
    o_ref[...] = (at * lax.rsqrt(ms + EPS) * ga_ref[...]).T.astype(BF16)


def _attention(q, k, v, sink, g_a, band):
    kb = lambda f: pl.BlockSpec((QB, D_KV), lambda j: (f(j), 0))
    prev = lambda j: jnp.maximum(AQ * j - 1, 0)
    nxt = lambda j: jnp.minimum(AQ * j + AQ, NQB - 1)
    pair = pl.BlockSpec((AQ * QB, D_KV), lambda j: (j, 0))
    g_cols = jnp.broadcast_to(g_a.reshape(D_ATTN, 1), (D_ATTN, QB))
    return pl.pallas_call(
        _attn_kernel,
        grid=(NQB // AQ,),
        in_specs=[pl.BlockSpec(memory_space=pltpu.SMEM),
                  pl.BlockSpec((AQ * QB, D_ATTN), lambda j: (j, 0)),
                  kb(prev), pair, kb(nxt), kb(prev), pair, kb(nxt),
                  pl.BlockSpec((CTX, D_KV), lambda b: (SEQ // CTX, 0)),
                  pl.BlockSpec((CTX, D_KV), lambda b: (SEQ // CTX, 0)),
                  pl.BlockSpec((D_ATTN, QB), lambda b: (0, 0)),
                  pl.BlockSpec((3 * QB, Q_PER_KV * QB), lambda b: (0, 0))],
        out_specs=pl.BlockSpec((AQ * QB, D_ATTN), lambda j: (j, 0)),
        out_shape=jax.ShapeDtypeStruct((T, D_ATTN), BF16),
        scratch_shapes=[pltpu.VMEM((AQ, D_ATTN, QB), F32)],
        compiler_params=_cp(("arbitrary",)),
        name="attention",
    )(sink, q, k, k, k, v, v, v, k, v, g_cols, band)


F1_J = 8
F2_K = 4


def _f1_kernel(u_ref, m1_ref, twr_ref, twi_ref, z_ref):
    g = jnp.dot(m1_ref[...], u_ref[...], preferred_element_type=F32)
    for jl in range(F1_J):
        twr = twr_ref[jl]
        twi = twi_ref[jl]
        for lt in range(D_F // 128):
            c0 = jl * D_F + lt * 128
            gr = g[0:N2, c0:c0 + 128]
            gi = g[N2:2 * N2, c0:c0 + 128]
            z_ref[jl, 0:N2, lt * 128:(lt + 1) * 128] = (gr * twr - gi * twi).astype(BF16)
            z_ref[jl, N2:2 * N2, lt * 128:(lt + 1) * 128] = (gr * twi + gi * twr).astype(BF16)


def _f2_kernel(zr_ref, zi_ref, m2_ref, y_ref):
    pr = jnp.dot(m2_ref[...], zr_ref[...], preferred_element_type=F32)
    pi = jnp.dot(m2_ref[...], zi_ref[...], preferred_element_type=F32)
    yr = pr[0:N1] + pi[N1:2 * N1]
    yi = pi[0:N1] - pr[N1:2 * N1]
    for kl in range(F2_K):
        y_ref[:, kl * 2 * D_F:kl * 2 * D_F + D_F] = yr[:, kl * D_F:(kl + 1) * D_F].astype(BF16)
        y_ref[:, kl * 2 * D_F + D_F:(kl + 1) * 2 * D_F] = yi[:, kl * D_F:(kl + 1) * D_F].astype(BF16)


def _f3_kernel(y_ref, uc_ref, cs_ref, mc_ref, wf_ref, gf_ref, o_ref):
    i = pl.program_id(0)

    def finish(parts, scale):
        f = jnp.concatenate(parts, axis=1) * scale
        t = jnp.dot(f.astype(BF16), wf_ref[...], preferred_element_type=F32)
        ms = jnp.mean(t * t, axis=-1, keepdims=True)
        o_ref[...] = (t * lax.rsqrt(ms + EPS) * gf_ref[...]).astype(BF16)

    @pl.when(i < NT_LAT)
    def _():
        parts = []
        for g in range(N_FG):
            lhs = jnp.concatenate([y_ref[:, g * 128:(g + 1) * 128],
                                   y_ref[:, D_F + g * 128:D_F + (g + 1) * 128]], axis=1)
            parts.append(jnp.dot(lhs, cs_ref[...], preferred_element_type=F32))
        finish(parts, float((SEQ * 128) ** -0.5))

    @pl.when(i == NT_LAT)
    def _():
        pq = jnp.dot(mc_ref[...], uc_ref[...], preferred_element_type=F32)
        parts = []
        for g in range(N_FG):
            lhs = jnp.concatenate([pq[0:CTX, g * 128:(g + 1) * 128],
                                   -pq[CTX:2 * CTX, g * 128:(g + 1) * 128]], axis=1)
            parts.append(jnp.dot(lhs.astype(BF16), cs_ref[...], preferred_element_type=F32))
        finish(parts, float((CTX * 128) ** -0.5))


def _fourier(l, u, w_f_bf, g_f, tb):
    u2 = u.reshape(T // N1, N1 * D_F)
    z = pl.pallas_call(
        _f1_kernel,
        grid=(N1 // F1_J,),
        in_specs=[pl.BlockSpec((N2, F1_J * D_F), lambda c: (0, c)),
                  pl.BlockSpec((2 * N2, N2), lambda c: (0, 0)),
                  pl.BlockSpec((F1_J, N2, 128), lambda c: (c, 0, 0)),
                  pl.BlockSpec((F1_J, N2, 128), lambda c: (c, 0, 0))],
        out_specs=pl.BlockSpec((F1_J, 2 * N2, D_F), lambda c: (c, 0, 0)),
        out_shape=jax.ShapeDtypeStruct((N1, 2 * N2, D_F), BF16),
        compiler_params=_cp(("arbitrary",)),
        name="fourier_s1",
    )(u2, tb["m1"], tb["twr"], tb["twi"])
    z2 = z.reshape(N1, 2 * N2 * D_F)
    nk = N2 // F2_K
    y = pl.pallas_call(
        _f2_kernel,
        grid=(nk,),
        in_specs=[pl.BlockSpec((N1, F2_K * D_F), lambda k: (0, k)),
                  pl.BlockSpec((N1, F2_K * D_F), lambda k: (0, nk + k)),
                  pl.BlockSpec((2 * N1, N1), lambda k: (0, 0))],
        out_specs=pl.BlockSpec((N1, F2_K * 2 * D_F), lambda k: (0, k)),
        out_shape=jax.ShapeDtypeStruct((N1, N2 * 2 * D_F), BF16),
        compiler_params=_cp(("arbitrary",)),
        name="fourier_s2",
    )(z2, z2, tb["m2"])
    y2 = y.reshape(SEQ, 2 * D_F)
    return pl.pallas_call(
        _f3_kernel,
        grid=(NT,),
        in_specs=[pl.BlockSpec((TM, 2 * D_F), lambda i: (jnp.minimum(i, NT_LAT - 1), 0)),
                  pl.BlockSpec((CTX, D_F), lambda i: (SEQ // CTX, 0)),
                  pl.BlockSpec((2 * 128, 128), lambda i: (0, 0)),
                  pl.BlockSpec((2 * CTX, CTX), lambda i: (0, 0)),
                  _layer_spec(l, D_F, D_F),
                  _row_spec(D_F)],
        out_specs=pl.BlockSpec((TM, D_F), lambda i: (i, 0)),
        out_shape=jax.ShapeDtypeStruct((T, D_F), BF16),
        compiler_params=_cp(("arbitrary",)),
        name="fourier_s3",
    )(y2, u, tb["csch"], tb["mc"], w_f_bf, g_f)


def _merge_kernel(x_ref, c_ref, fn_ref, an_ref, wo_ref, ga1_ref, g2_ref, sc2_ref, sh2_ref, ga2_ref,
                  wsg_ref, wsu_ref, wsd_ref, wrh_ref, wrl_ref, br_ref, tri_ref,
                  xmid_ref, h2_ref, info_ref, wcol_ref, cnt_ref, carry_scr, *, split):
    i = pl.program_id(0)

    @pl.when(i == 0)
    def _():
        carry_scr[...] = jnp.zeros_like(carry_scr)

    m = (jnp.dot(fn_ref[...], wo_ref[0:D_F, :], preferred_element_type=F32)
         + jnp.dot(an_ref[...], wo_ref[D_F:D_MODEL, :], preferred_element_type=F32))
    x1 = _stream_tile(x_ref, c_ref, split) + ga1_ref[...] * m
    ms = jnp.mean(x1 * x1, axis=-1, keepdims=True)
    h = x1 * lax.rsqrt(ms + EPS) * g2_ref[...]
    h = h * (1.0 + sc2_ref[...]) + sh2_ref[...]
    hb = h.astype(BF16)
    hp = _pack_bf16_pairs(h)
    for s in range(SUB):
        h2_ref[:, s, :] = hp[:, s * PK:(s + 1) * PK]

    a = jnp.dot(hb, wsg_ref[...], preferred_element_type=F32)
    u = jnp.dot(hb, wsu_ref[...], preferred_element_type=F32)
    act = (_silu(a) * u).astype(BF16)
    ys = jnp.dot(act, wsd_ref[...], preferred_element_type=F32)
    xmid_ref[...] = x1 + ga2_ref[...] * ys

    hl = (h - hb.astype(F32)).astype(BF16)
    dn = (((1,), (1,)), ((), ()))
    both = lax.dot_general(jnp.concatenate([wrh_ref[...], wrl_ref[...]], axis=0), hb, dn,
                           preferred_element_type=F32)
    lg = both[0:N_EXPERTS] + (both[N_EXPERTS:2 * N_EXPERTS]
                              + lax.dot_general(wrh_ref[...], hl, dn, preferred_element_type=F32))
    score = jax.nn.sigmoid(lg)
    sel = score + br_ref[...]
    s = [sel[jj * N_GROUPS:(jj + 1) * N_GROUPS] for jj in range(EPG)]
    sr = [score[jj * N_GROUPS:(jj + 1) * N_GROUPS] for jj in range(EPG)]
    hi01, lo01 = jnp.maximum(s[0], s[1]), jnp.minimum(s[0], s[1])
    hi23, lo23 = jnp.maximum(s[2], s[3]), jnp.minimum(s[2], s[3])
    gscore = jnp.maximum(hi01, hi23) + jnp.maximum(jnp.minimum(hi01, hi23), jnp.maximum(lo01, lo23))
    gi = lax.broadcasted_iota(I32, (N_GROUPS, TM), 0)
    gmax = jnp.max(gscore, axis=0, keepdims=True)
    g_idx = jnp.min(jnp.where(gscore == gmax, gi, N_GROUPS), axis=0, keepdims=True)
    gsel = gi == g_idx
    v = [jnp.sum(jnp.where(gsel, s[jj], 0.0), axis=0, keepdims=True) for jj in range(EPG)]
    vr = [jnp.sum(jnp.where(gsel, sr[jj], 0.0), axis=0, keepdims=True) for jj in range(EPG)]

    def first_argmax(vals):
        best = vals[0]
        idx = jnp.zeros((1, TM), I32)
        for jj in range(1, EPG):
            upd = vals[jj] > best
            best = jnp.where(upd, vals[jj], best)
            idx = jnp.where(upd, jj, idx)
        return idx

    i1 = first_argmax(v)
    i2 = first_argmax([jnp.where(i1 == jj, -jnp.inf, v[jj]) for jj in range(EPG)])

    def pick(vals, idx):
        out = vals[0]
        for jj in range(1, EPG):
            out = jnp.where(idx == jj, vals[jj], out)
        return out

    w1 = pick(vr, i1)
    w2 = pick(vr, i2)
    wsum = w1 + w2
    w1 = w1 / wsum
    w2 = w2 / wsum

    ri = lax.broadcasted_iota(I32, (N_EXPERTS, TM), 0)
    oh1 = ri == i1 * N_GROUPS + g_idx
    oh2 = ri == i2 * N_GROUPS + g_idx
    oh = jnp.logical_or(oh1, oh2).astype(F32)
    pre = jnp.dot(oh.astype(BF16), tri_ref[...], preferred_element_type=F32) + carry_scr[:, 0:1]
    r1 = jnp.sum(jnp.where(oh1, pre, 0.0), axis=0, keepdims=True)
    r2 = jnp.sum(jnp.where(oh2, pre, 0.0), axis=0, keepdims=True)
    carry = carry_scr[...] + jnp.sum(oh, axis=1, keepdims=True)
    carry_scr[...] = carry
    cnt_ref[...] = carry

    zi = jnp.zeros((4, TM), I32)
    info_ref[...] = jnp.concatenate(
        [g_idx * EPG + i1, g_idx * EPG + i2, r1.astype(I32), r2.astype(I32), zi], axis=0)
    wrow = jnp.concatenate([w1, w2, jnp.zeros((126, TM), F32)], axis=0)
    wcol_ref[...] = wrow.T


def _stream_specs(split, c_blk):
    if split:
        return [pl.BlockSpec((TM, D_MODEL), lambda i: (jnp.minimum(i, NT_LAT - 1), 0)),
                pl.BlockSpec((CTX, D_MODEL), lambda i: (c_blk, 0))]
    return [pl.BlockSpec((TM, D_MODEL), lambda i: (i, 0)),
            pl.BlockSpec((8, D_MODEL), lambda i: (0, 0))]


def _stream_tile(x_ref, c_ref, split):
    if split:
        return jnp.where(pl.program_id(0) == NT_LAT, c_ref[...], x_ref[...])
    return x_ref[...]


def _merge(l, xa, ca, c_blk, fn, an, w_out_bf, modt, g2, wsg, wsu, wsd, wrh, wrl, br, tb):
    tile = lambda n: pl.BlockSpec((TM, n), lambda i: (i, 0))
    full = lambda a, b: pl.BlockSpec((a, b), lambda i: (0, 0))
    return pl.pallas_call(
        functools.partial(_merge_kernel, split=xa is not ca),
        grid=(NT,),
        in_specs=_stream_specs(xa is not ca, c_blk) + [
                  tile(D_F), tile(D_ATTN), _layer_spec(l, D_MODEL, D_MODEL),
                  _mod_spec(2), _row_spec(D_MODEL), _mod_spec(4), _mod_spec(3), _mod_spec(5),
                  _layer_spec(l, D_MODEL, D_SHARED), _layer_spec(l, D_MODEL, D_SHARED),
                  _layer_spec(l, D_SHARED, D_MODEL),
                  full(N_EXPERTS, D_MODEL), full(N_EXPERTS, D_MODEL), full(N_EXPERTS, TM),
                  full(TM, TM)],
        out_specs=[tile(D_MODEL), pl.BlockSpec((TM, SUB, PK), lambda i: (i, 0, 0)),
                   pl.BlockSpec((8, TM), lambda i: (0, i)),
                   pl.BlockSpec((TM, 128), lambda i: (i, 0)),
                   pl.BlockSpec((N_EXPERTS, 128), lambda i: (0, 0))],
        out_shape=[jax.ShapeDtypeStruct((T, D_MODEL), F32), jax.ShapeDtypeStruct((T, SUB, PK), I32),
                   jax.ShapeDtypeStruct((8, T), I32), jax.ShapeDtypeStruct((T, 128), F32),
                   jax.ShapeDtypeStruct((N_EXPERTS, 128), F32)],
        scratch_shapes=[pltpu.VMEM((N_EXPERTS, 128), F32)],
        compiler_params=_cp(("arbitrary",)),
        name="merge_route",
    )(xa, ca, fn, an, w_out_bf, modt, g2, modt, modt, modt, wsg, wsu, wsd, wrh, wrl, br, tb["tri"])


def _dispatch_kernel(dest_ref, pend_ref, padded_ref, h_ref, xb_hbm, zero_scr, slab, sem, zsem, *, n):
    i = pl.program_id(0)
    slot = i % 2

    def zero_copy(row0):
        return pltpu.make_async_copy(zero_scr, xb_hbm.at[pl.ds(row0, BM)], zsem)

    @pl.when(i == 0)
    def _():
        zero_scr[...] = jnp.zeros_like(zero_scr)
        n_used = pend_ref[N_EXPERTS - 1] // BM
        for e in range(N_EXPERTS):
            @pl.when(padded_ref[e] > 0)
            def _():
                zero_copy(pend_ref[e] - BM).start()
        for blk in range(NB - N_EXPERTS, NB):
            @pl.when(blk >= n_used)
            def _():
                zero_copy(blk * BM).start()
        for e in range(N_EXPERTS):
            @pl.when(padded_ref[e] > 0)
            def _():
                zero_copy(0).wait()
        for blk in range(NB - N_EXPERTS, NB):
            @pl.when(blk >= n_used)
            def _():
                zero_copy(0).wait()

    slab[slot] = h_ref[...]

    def copy(sl, r, d):
        return pltpu.make_async_copy(slab.at[sl, pl.ds(r, 1)], xb_hbm.at[pl.ds(d, 1)], sem.at[sl])

    def drain(sl):
        def body(r, c):
            copy(sl, 0, 0).wait()
            copy(sl, 0, 0).wait()
            return c
        lax.fori_loop(0, TM, body, 0)

    def issue(r, c):
        t = i * TM + r
        copy(slot, r, dest_ref[2 * t]).start()
        copy(slot, r, dest_ref[2 * t + 1]).start(priority=1)
        return c

    lax.fori_loop(0, TM, issue, 0)

    @pl.when(i > 0)
    def _():
        drain(1 - slot)

    @pl.when(i == n - 1)
    def _():
        drain(slot)


def _dispatch(dest, pad_end, padded, h2):
    return pl.pallas_call(
        functools.partial(_dispatch_kernel, n=NT),
        grid_spec=pltpu.PrefetchScalarGridSpec(
            num_scalar_prefetch=3, grid=(NT,),
            in_specs=[pl.BlockSpec((TM, SUB, PK), lambda i, d, pe, pd: (i, 0, 0))],
            out_specs=pl.BlockSpec(memory_space=pl.ANY),
            scratch_shapes=[pltpu.VMEM((BM, SUB, PK), I32), pltpu.VMEM((2, TM, SUB, PK), I32),
                            pltpu.SemaphoreType.DMA((2,)), pltpu.SemaphoreType.DMA(())]),
        out_shape=jax.ShapeDtypeStruct((ROWS, SUB, PK), I32),
        compiler_params=pltpu.CompilerParams(dimension_semantics=("arbitrary",),
                                             has_side_effects=True, vmem_limit_bytes=VMEM_LIMIT),
        name="dispatch",
    )(dest, pad_end, padded, h2)


def _pack_bf16_pairs(h):
    bits = lax.bitcast_convert_type(h.astype(BF16).astype(F32), I32)
    half = h.shape[1] // 2
    return (bits[:, :half] & -65536) | lax.shift_right_logical(bits[:, half:], 16)


GRP = 32


def _expert_kernel(eo_ref, es_ref, n_ref, x_ref, perm_ref, wg_hbm, wu_hbm, wd_hbm, y_ref,
                   wgf, wuf, wdf, wgb, wub, wdb, xs_scr, wsem, *, layer):
    b = pl.program_id(0)
    k = eo_ref[b]
    first = jnp.logical_or(b == 0, k != eo_ref[jnp.maximum(b - 1, 0)])

    def copies(kk, slot):
        e = es_ref[kk]
        return (pltpu.make_async_copy(wg_hbm.at[layer, e], wgf.at[slot], wsem.at[slot, 0]),
                pltpu.make_async_copy(wu_hbm.at[layer, e], wuf.at[slot], wsem.at[slot, 1]),
                pltpu.make_async_copy(wd_hbm.at[layer, e], wdf.at[slot], wsem.at[slot, 2]))

    @pl.when(b == 0)
    def _():
        for c in copies(0, 0):
            c.start()

    @pl.when(jnp.logical_and(first, b < n_ref[0]))
    def _():
        slot = k % 2
        for c in copies(k, slot):
            c.wait()

        @pl.when(k + 1 < n_ref[1])
        def _():
            for c in copies(k + 1, 1 - slot):
                c.start()

        wgb[...] = wgf[slot].astype(BF16)
        wub[...] = wuf[slot].astype(BF16)
        wdb[...] = wdf[slot].astype(BF16)

    @pl.when(b < n_ref[0])
    def _():
        half = D_MODEL // 2
        for grp in range(BM // GRP):
            w = x_ref[grp * GRP:(grp + 1) * GRP].reshape(GRP * SUB, PK)
            hi = lax.bitcast_convert_type(w & -65536, F32).astype(BF16)
            lo = lax.bitcast_convert_type(w << 16, F32).astype(BF16)
            g = jnp.concatenate([hi, lo], axis=1)
            r = jnp.dot(perm_ref[...], g, preferred_element_type=F32).astype(BF16)
            rows = slice(grp * GRP, (grp + 1) * GRP)
            for s in range(SUB):
                xs_scr[rows, s * PK:(s + 1) * PK] = r[s * GRP:(s + 1) * GRP, 0:PK]
                xs_scr[rows, half + s * PK:half + (s + 1) * PK] = r[s * GRP:(s + 1) * GRP, PK:2 * PK]
        x = xs_scr[...]
        a = jnp.dot(x, wgb[...], preferred_element_type=F32)
        u = jnp.dot(x, wub[...], preferred_element_type=F32)
        act = (_silu(a) * u).astype(BF16)
        yp = _pack_bf16_pairs(jnp.dot(act, wdb[...], preferred_element_type=F32))
        for s in range(SUB):
            y_ref[:, s, :] = yp[:, s * PK:(s + 1) * PK]

    @pl.when(b >= n_ref[0])
    def _():
        y_ref[...] = jnp.zeros_like(y_ref)


def _experts(l, eo, es, nn, xb, perm, w_g, w_u, w_d):
    hbm = pl.BlockSpec(memory_space=pl.ANY)
    return pl.pallas_call(
        functools.partial(_expert_kernel, layer=l),
        grid_spec=pltpu.PrefetchScalarGridSpec(
            num_scalar_prefetch=3, grid=(NB,),
            in_specs=[pl.BlockSpec((BM, SUB, PK), lambda b, eo, es, n: (jnp.minimum(b, n[0] - 1), 0, 0)),
                      pl.BlockSpec((GRP * SUB, GRP * SUB), lambda b, eo, es, n: (0, 0)),
                      hbm, hbm, hbm],
            out_specs=pl.BlockSpec((BM, SUB, PK), lambda b, eo, es, n: (b, 0, 0)),
            scratch_shapes=[pltpu.VMEM((2, D_MODEL, D_EXPERT), F32),
                            pltpu.VMEM((2, D_MODEL, D_EXPERT), F32),
                            pltpu.VMEM((2, D_EXPERT, D_MODEL), F32),
                            pltpu.VMEM((D_MODEL, D_EXPERT), BF16),
                            pltpu.VMEM((D_MODEL, D_EXPERT), BF16),
                            pltpu.VMEM((D_EXPERT, D_MODEL), BF16),
                            pltpu.VMEM((BM, D_MODEL), BF16),
                            pltpu.SemaphoreType.DMA((2, 3))]),
        out_shape=jax.ShapeDtypeStruct((ROWS, SUB, PK), I32),
        compiler_params=_cp(("arbitrary",)),
        name="experts",
    )(eo, es, nn, xb, perm, w_g, w_u, w_d)


def _combine_kernel(dest_ref, y_hbm, xmid_ref, wcol_ref, ga2_ref, perm_ref, o_ref, b1, b2, sem, *, n):
    i = pl.program_id(0)

    def copy(d, buf, slot, r):
        return pltpu.make_async_copy(y_hbm.at[pl.ds(d, 1)], buf.at[slot, pl.ds(r, 1)], sem.at[slot])

    def issue(tile, slot):
        def body(r, c):
            t = tile * TM + r
            copy(dest_ref[2 * t], b1, slot, r).start()
            copy(dest_ref[2 * t + 1], b2, slot, r).start(priority=1)
            return c
        lax.fori_loop(0, TM, body, 0)

    @pl.when(i == 0)
    def _():
        issue(0, 0)

    @pl.when(i + 1 < n)
    def _():
        issue(i + 1, (i + 1) % 2)

    slot = i % 2

    def drain(r, c):
        copy(0, b1, slot, 0).wait()
        copy(0, b2, slot, 0).wait()
        return c

    lax.fori_loop(0, TM, drain, 0)

    def unpack_group(buf, grp):
        w = buf[slot, grp * GRP:(grp + 1) * GRP].reshape(GRP * SUB, PK)
        hi = lax.bitcast_convert_type(w & -65536, F32).astype(BF16)
        lo = lax.bitcast_convert_type(w << 16, F32).astype(BF16)
        return jnp.dot(perm_ref[...], jnp.concatenate([hi, lo], axis=1), preferred_element_type=F32)

    half = D_MODEL // 2
    for grp in range(TM // GRP):
        rows = slice(grp * GRP, (grp + 1) * GRP)
        w1 = wcol_ref[rows, 0:1]
        w2 = wcol_ref[rows, 1:2]
        r1 = unpack_group(b1, grp)
        r2 = unpack_group(b2, grp)
        for s in range(SUB):
            sr = slice(s * GRP, (s + 1) * GRP)
            for c0, lanes in ((s * PK, slice(0, PK)), (half + s * PK, slice(PK, 2 * PK))):
                cs = slice(c0, c0 + PK)
                o_ref[rows, cs] = xmid_ref[rows, cs] + ga2_ref[:, cs] * (w1 * r1[sr, lanes] + w2 * r2[sr, lanes])


def _combine(dest, yb, xmid, wcol, modt, perm, n_tiles):
    tile = lambda n: pl.BlockSpec((TM, n), lambda i, d: (i, 0))
    return pl.pallas_call(
        functools.partial(_combine_kernel, n=n_tiles),
        grid_spec=pltpu.PrefetchScalarGridSpec(
            num_scalar_prefetch=1, grid=(n_tiles,),
            in_specs=[pl.BlockSpec(memory_space=pl.ANY), tile(D_MODEL), tile(128),
                      pl.BlockSpec((None, None, 1, D_MODEL), lambda i, d: (i // NT_LAT, 5, 0, 0)),
                      pl.BlockSpec((GRP * SUB, GRP * SUB), lambda i, d: (0, 0))],
            out_specs=tile(D_MODEL),
            scratch_shapes=[pltpu.VMEM((2, TM, SUB, PK), I32), pltpu.VMEM((2, TM, SUB, PK), I32),
                            pltpu.SemaphoreType.DMA((2,))]),
        out_shape=jax.ShapeDtypeStruct((n_tiles * TM, D_MODEL), F32),
        compiler_params=_cp(("arbitrary",)),
        name="combine",
    )(dest, yb, xmid, wcol, modt, perm)


def _perm_experts(a):
    return a.reshape(a.shape[:-1] + (N_GROUPS, EPG)).swapaxes(-1, -2).reshape(a.shape)


def kernel(x, c, ctx, c_ctx, w_mod, b_mod, g_norm1, g_norm2, w_in, w_fourier, g_q, g_k, sink,
           g_branch_f, g_branch_a, w_out, w_router, b_router, w_exp_gate, w_exp_up, w_exp_down,
           w_sh_gate, w_sh_up, w_sh_down):
    assert x.shape == (1, SEQ, D_MODEL) and ctx.shape == (1, CTX, D_MODEL)
    tb = _tables()
    stream = (x[0], ctx[0], 0)

    cc = jnp.zeros((8, D_MODEL), F32).at[0].set(c[0]).at[1].set(c_ctx)
    mods = _modulation(cc, w_mod, b_mod)

    wr_t = _perm_experts(w_router).T
    wrh = wr_t.astype(BF16)
    wrl = (wr_t - wrh.astype(F32)).astype(BF16)
    br = jnp.broadcast_to(_perm_experts(b_router)[:, None], (N_EXPERTS, TM)).astype(F32)

    w_in_b, w_f_b, w_out_b = w_in.astype(BF16), w_fourier.astype(BF16), w_out.astype(BF16)
    w_sg_b, w_su_b, w_sd_b = w_sh_gate.astype(BF16), w_sh_up.astype(BF16), w_sh_down.astype(BF16)

    for l in range(DEPTH):
        last = l == DEPTH - 1
        modt = mods[l, 0:2].reshape(2, 6, 1, D_MODEL)
        row = lambda a: a[l].reshape(1, -1)
        u, q, k, v = _project(l, *stream, row(g_norm1), modt, w_in_b, row(g_q), row(g_k), tb)
        an = _attention(q, k, v, sink[l], row(g_branch_a), tb["band"])
        fn = _fourier(l, u, w_f_b, row(g_branch_f), tb)
        xmid, h2, info, wcol, cnt = _merge(
            l, *stream, fn, an, w_out_b, modt, row(g_norm2), w_sg_b, w_su_b, w_sd_b,
            wrh, wrl, br, tb)

        xs = _routed(l, info, cnt, h2, xmid, wcol, modt, w_exp_gate, w_exp_up, w_exp_down, tb,
                     NT_LAT if last else NT)
        stream = (xs, xs, NT_LAT)
    return xs[None]


def _routed(l, info, cnt, h2, xmid, wcol, modt, w_g, w_u, w_d, tb, n_tiles):
    counts = cnt[:, 0].astype(I32).reshape(EPG, N_GROUPS).T.reshape(N_EXPERTS)
    padded = ((counts + BM - 1) // BM) * BM
    pad_end = jnp.cumsum(padded).astype(I32)
    pad_start = pad_end - padded
    start = jnp.sum(jnp.where(info[0:2, :, None] == jnp.arange(N_EXPERTS, dtype=I32),
                              pad_start, 0), axis=-1)
    dest = (start + info[2:4]).T.reshape(2 * T)
    ids = jnp.arange(N_EXPERTS, dtype=I32)
    block_row0 = jnp.arange(NB, dtype=I32)[:, None] * BM
    block_e = jnp.minimum(jnp.sum((pad_end <= block_row0).astype(I32), axis=-1), N_EXPERTS - 1)
    used = padded > 0
    ordinal = jnp.cumsum(used.astype(I32)) - 1
    eo = jnp.sum(jnp.where(block_e[:, None] == ids, ordinal, 0), axis=-1)
    es = jnp.sum(jnp.where(jnp.logical_and(used, ordinal == ids[:, None]), ids, 0), axis=-1)
    nn = jnp.stack([pad_end[N_EXPERTS - 1] // BM, ordinal[N_EXPERTS - 1] + 1]).astype(I32)
    xb = _dispatch(dest, pad_end, padded, h2)
    yb = _experts(l, eo.astype(I32), es.astype(I32), nn, xb, tb["perm"], w_g, w_u, w_d)
    return _combine(dest, yb, xmid, wcol, modt, tb["perm"], n_tiles)
```

```python
import functools

import numpy as np
import jax
import jax.numpy as jnp
from jax import lax
from jax.experimental import pallas as pl
from jax.experimental.pallas import tpu as pltpu

F32 = jnp.float32
BF16 = jnp.bfloat16
I32 = jnp.int32

D_MODEL = 2048
SEQ = 8192
CTX = 256
T = SEQ + CTX
DEPTH = 4
GRID_W = 64
HEAD_DIM = 128
D_F = 512
N_FG = 4
N_HEADS = 12
N_KV = 4
Q_PER_KV = 3
D_ATTN = N_HEADS * HEAD_DIM
D_KV = N_KV * HEAD_DIM
D_IN = D_F + D_ATTN + 2 * D_KV
ROPE_BASE = 10000.0
N_EXPERTS = 32
N_GROUPS = 8
EPG = 4
D_EXPERT = 512
D_SHARED = 512
EPS = 1e-6
NEG = -1e30
LOG2E = 1.4426950408889634
WINDOW = 128

TM = 256
NT = T // TM
NT_LAT = SEQ // TM
QB = 128
AQ = 6
NQB = T // QB
NQB_LAT = SEQ // QB
BM = 256
NB = (2 * T + BM - 1) // BM + N_EXPERTS
ROWS = NB * BM
N1 = 128
N2 = 64
SUB = 8
RL = D_MODEL // SUB
PK = D_MODEL // 2 // SUB
VMEM_LIMIT = 56 * 1024 * 1024


def _cp(sem):
    return pltpu.CompilerParams(dimension_semantics=sem, vmem_limit_bytes=VMEM_LIMIT)


def _silu(a):
    return a * jax.nn.sigmoid(a)


@functools.lru_cache(maxsize=None)
def _tables():
    t = np.arange(SEQ)
    row = (t // GRID_W).astype(np.float64)
    col = (t % GRID_W).astype(np.float64)
    inv = ROPE_BASE ** (-np.arange(0, HEAD_DIM // 2, 2, dtype=np.float64) / (HEAD_DIM // 2))
    ar = row[:, None] * inv
    ac = col[:, None] * inv
    ang = np.concatenate([ar, ar, ac, ac], axis=-1)
    cos = np.concatenate([np.cos(ang), np.ones((CTX, HEAD_DIM))], axis=0)
    sin = np.concatenate([np.sin(ang), np.zeros((CTX, HEAD_DIM))], axis=0)
    first = (np.arange(HEAD_DIM) % 64) < 32
    rope_a = np.where(first[None, :], -sin, 0.0)
    rope_b = np.where(first[None, :], 0.0, sin)

    def cs(n):
        k = np.arange(n)
        a = 2.0 * np.pi * np.outer(k, k) / n
        return np.cos(a), np.sin(a)

    c64, s64 = cs(N2)
    c128, s128 = cs(N1)
    c256, s256 = cs(CTX)
    m1 = np.concatenate([c64, -s64], axis=0)
    m2 = np.concatenate([c128, s128], axis=0)
    mc = np.concatenate([c256, s256], axis=0)
    csch = np.concatenate([c128, s128], axis=0)
    a = 2.0 * np.pi * np.outer(np.arange(N1), np.arange(N2)) / SEQ
    twr = np.broadcast_to(np.cos(a)[:, :, None], (N1, N2, 128))
    twi = np.broadcast_to(-np.sin(a)[:, :, None], (N1, N2, 128))
    tri = (np.arange(TM)[:, None] < np.arange(TM)[None, :]).astype(np.float32)
    kpos = np.arange(3 * QB)[:, None] - QB
    qpos = (np.arange(Q_PER_KV * QB) % QB)[None, :]
    band = np.where(np.abs(qpos - kpos) <= WINDOW, 0.0, NEG)
    perm = np.zeros((GRP * SUB, GRP * SUB), np.float32)
    rr, ss = np.meshgrid(np.arange(GRP), np.arange(SUB), indexing="ij")
    perm[(ss * GRP + rr).ravel(), (rr * SUB + ss).ravel()] = 1.0
    return dict(
        cos=jnp.asarray(cos, F32), rope_a=jnp.asarray(rope_a, F32), rope_b=jnp.asarray(rope_b, F32),
        m1=jnp.asarray(m1, BF16), m2=jnp.asarray(m2, BF16), mc=jnp.asarray(mc, BF16),
        csch=jnp.asarray(csch, BF16), twr=jnp.asarray(twr, F32), twi=jnp.asarray(twi, F32),
        tri=jnp.asarray(tri, BF16), perm=jnp.asarray(perm, BF16), band=jnp.asarray(band, F32))


MOD_TN = 1024


def _mod_kernel(cc_ref, w_ref, b_ref, o_ref):
    a = _silu(cc_ref[...])
    o_ref[...] = jnp.dot(a.astype(BF16), w_ref[...].astype(BF16),
                         preferred_element_type=F32) + b_ref[...]


def _modulation(cc, w_mod, b_mod):
    nl = w_mod.shape[0]
    n6 = w_mod.shape[2]
    return pl.pallas_call(
        _mod_kernel,
        grid=(nl, n6 // MOD_TN),
        in_specs=[pl.BlockSpec((8, D_MODEL), lambda l, j: (0, 0)),
                  pl.BlockSpec((None, D_MODEL, MOD_TN), lambda l, j: (l, 0, j)),
                  pl.BlockSpec((None, 1, MOD_TN), lambda l, j: (l, 0, j))],
        out_specs=pl.BlockSpec((None, 8, MOD_TN), lambda l, j: (l, 0, j)),
        out_shape=jax.ShapeDtypeStruct((nl, 8, n6), F32),
        compiler_params=_cp(("arbitrary", "arbitrary")),
        name="modulation",
    )(cc, w_mod, b_mod.reshape(nl, 1, n6))


def _mod_spec(chunk):
    return pl.BlockSpec((None, None, 1, D_MODEL), lambda i: (i // NT_LAT, chunk, 0, 0))


def _row_spec(n):
    return pl.BlockSpec((1, n), lambda i: (0, 0))


PCW = 256


def _proj_kernel(x_ref, c_ref, g_ref, sc_ref, sh_ref, w_ref, gq_ref, gk_ref, cos_ref, ra_ref, rb_ref,
                 u_ref, q_ref, k_ref, v_ref, hb_ref, *, split):
    x = _stream_tile(x_ref, c_ref, split)
    ms = jnp.mean(x * x, axis=-1, keepdims=True)
    h = x * lax.rsqrt(ms + EPS) * g_ref[...]
    h = h * (1.0 + sc_ref[...]) + sh_ref[...]
    hb_ref[...] = h.astype(BF16)
    cos = cos_ref[...]
    ra = ra_ref[...]
    rb = rb_ref[...]

    def head(t, g):
        m = jnp.mean(t * t, axis=-1, keepdims=True)
        t = t * lax.rsqrt(m + EPS) * g
        return t * cos + pltpu.roll(t, 96, 1) * ra + pltpu.roll(t, 32, 1) * rb

    scale = HEAD_DIM ** -0.5 * LOG2E
    for c in range(D_IN // PCW):
        col = c * PCW
        p = jnp.dot(hb_ref[...], w_ref[:, col:col + PCW], preferred_element_type=F32)
        if col < D_F:
            u_ref[:, col:col + PCW] = p.astype(BF16)
        elif col < D_F + D_ATTN:
            o = col - D_F
            for j in range(PCW // HEAD_DIM):
                t = head(p[:, j * HEAD_DIM:(j + 1) * HEAD_DIM], gq_ref[...]) * scale
                q_ref[:, o + j * HEAD_DIM:o + (j + 1) * HEAD_DIM] = t.astype(BF16)
        elif col < D_F + D_ATTN + D_KV:
            o = col - D_F - D_ATTN
            for j in range(PCW // HEAD_DIM):
                t = head(p[:, j * HEAD_DIM:(j + 1) * HEAD_DIM], gk_ref[...])
                k_ref[:, o + j * HEAD_DIM:o + (j + 1) * HEAD_DIM] = t.astype(BF16)
        else:
            o = col - D_F - D_ATTN - D_KV
            v_ref[:, o:o + PCW] = p.astype(BF16)


def _layer_spec(l, a, b):
    return pl.BlockSpec((None, a, b), lambda i: (l, 0, 0))


def _project(l, xa, ca, c_blk, g1, modt, w_in_bf, gq, gk, tb):
    tile = lambda n: pl.BlockSpec((TM, n), lambda i: (i, 0))
    return pl.pallas_call(
        functools.partial(_proj_kernel, split=xa is not ca),
        grid=(NT,),
        in_specs=_stream_specs(xa is not ca, c_blk) + [
                  _row_spec(D_MODEL), _mod_spec(1), _mod_spec(0),
                  _layer_spec(l, D_MODEL, D_IN),
                  _row_spec(HEAD_DIM), _row_spec(HEAD_DIM),
                  tile(HEAD_DIM), tile(HEAD_DIM), tile(HEAD_DIM)],
        out_specs=[tile(D_F), tile(D_ATTN), tile(D_KV), tile(D_KV)],
        out_shape=[jax.ShapeDtypeStruct((T, D_F), BF16), jax.ShapeDtypeStruct((T, D_ATTN), BF16),
                   jax.ShapeDtypeStruct((T, D_KV), BF16), jax.ShapeDtypeStruct((T, D_KV), BF16)],
        scratch_shapes=[pltpu.VMEM((TM, D_MODEL), BF16)],
        compiler_params=_cp(("arbitrary",)),
        name="norm_proj",
    )(xa, ca, g1, modt, modt, w_in_bf, gq, gk, tb["cos"], tb["rope_a"], tb["rope_b"])


def _attn_kernel(sink_ref, q_ref, kp_ref, kc_ref, kn_ref, vp_ref, vc_ref, vn_ref, kx_ref, vx_ref,
                 ga_ref, band_ref, o_ref, a_scr):
    step = pl.program_id(0)
    k_all = [kp_ref] + [kc_ref.at[i * QB:(i + 1) * QB] for i in range(AQ)] + [kn_ref]
    v_all = [vp_ref] + [vc_ref.at[i * QB:(i + 1) * QB] for i in range(AQ)] + [vn_ref]
    for part in range(AQ):
        rows = slice(part * QB, (part + 1) * QB)
        _attn_block(AQ * step + part, sink_ref, q_ref.at[rows], k_all[part:part + 3], v_all[part:part + 3],
                    kx_ref, vx_ref, ga_ref, band_ref, o_ref.at[rows], a_scr.at[part])


def _attn_block(b, sink_ref, q_ref, k_blocks, v_blocks, kx_ref, vx_ref, ga_ref, band_ref, o_ref, a_scr):
    kp_ref, kc_ref, kn_ref = k_blocks
    vp_ref, vc_ref, vn_ref = v_blocks
    lat = b < NQB_LAT
    off = lambda ok: jnp.where(ok, 0.0, NEG)
    bias = jnp.concatenate(
        [band_ref[0:QB] + off(jnp.logical_and(lat, b > 0)),
         band_ref[QB:2 * QB] + off(lat),
         band_ref[2 * QB:3 * QB] + off(b < NQB_LAT - 1)], axis=0)
    dn = (((1,), (1,)), ((), ()))
    for hk in range(N_KV):
        ks = slice(hk * HEAD_DIM, (hk + 1) * HEAD_DIM)
        qs = jnp.concatenate(
            [q_ref[:, (hk * Q_PER_KV + g) * HEAD_DIM:(hk * Q_PER_KV + g + 1) * HEAD_DIM]
             for g in range(Q_PER_KV)], axis=0)
        kw = jnp.concatenate([kp_ref[:, ks], kc_ref[:, ks], kn_ref[:, ks]], axis=0)
        vw = jnp.concatenate([vp_ref[:, ks], vc_ref[:, ks], vn_ref[:, ks]], axis=0)
        st = lax.dot_general(kw, qs, dn, preferred_element_type=F32) + bias
        sxt = lax.dot_general(kx_ref[:, ks], qs, dn, preferred_element_type=F32)
        sk = jnp.concatenate(
            [jnp.full((1, QB), sink_ref[hk * Q_PER_KV + g] * LOG2E, F32) for g in range(Q_PER_KV)],
            axis=1)
        m = jnp.maximum(jnp.maximum(jnp.max(st, axis=0, keepdims=True),
                                    jnp.max(sxt, axis=0, keepdims=True)), sk)
        e = jnp.exp2(st - m)
        ex = jnp.exp2(sxt - m)
        den = (jnp.sum(e, axis=0, keepdims=True) + jnp.sum(ex, axis=0, keepdims=True)
               + jnp.exp2(sk - m))
        dt = (((0,), (0,)), ((), ()))
        ot = (lax.dot_general(vw, e.astype(BF16), dt, preferred_element_type=F32)
              + lax.dot_general(vx_ref[:, ks], ex.astype(BF16), dt, preferred_element_type=F32)) / den
        for g in range(Q_PER_KV):
            h = hk * Q_PER_KV + g
            a_scr[h * HEAD_DIM:(h + 1) * HEAD_DIM, :] = ot[:, g * QB:(g + 1) * QB]
    at = a_scr[...]
    ms = jnp.mean(at * at, axis=0, keepdims=True)
    o_ref[...] = (at * lax.rsqrt(ms + EPS) * ga_ref[...]).T.astype(BF16)


def _attention(q, k, v, sink, g_a, band):
    kb = lambda f: pl.BlockSpec((QB, D_KV), lambda j: (f(j), 0))
    prev = lambda j: jnp.maximum(AQ * j - 1, 0)
    nxt = lambda j: jnp.minimum(AQ * j + AQ, NQB - 1)
    pair = pl.BlockSpec((AQ * QB, D_KV), lambda j: (j, 0))
    g_cols = jnp.broadcast_to(g_a.reshape(D_ATTN, 1), (D_ATTN, QB))
    return pl.pallas_call(
        _attn_kernel,
        grid=(NQB // AQ,),
        in_specs=[pl.BlockSpec(memory_space=pltpu.SMEM),
                  pl.BlockSpec((AQ * QB, D_ATTN), lambda j: (j, 0)),
                  kb(prev), pair, kb(nxt), kb(prev), pair, kb(nxt),
                  pl.BlockSpec((CTX, D_KV), lambda b: (SEQ // CTX, 0)),
                  pl.BlockSpec((CTX, D_KV), lambda b: (SEQ // CTX, 0)),
                  pl.BlockSpec((D_ATTN, QB), lambda b: (0, 0)),
                  pl.BlockSpec((3 * QB, Q_PER_KV * QB), lambda b: (0, 0))],
        out_specs=pl.BlockSpec((AQ * QB, D_ATTN), lambda j: (j, 0)),
        out_shape=jax.ShapeDtypeStruct((T, D_ATTN), BF16),
        scratch_shapes=[pltpu.VMEM((AQ, D_ATTN, QB), F32)],
        compiler_params=_cp(("arbitrary",)),
        name="attention",
    )(sink, q, k, k, k, v, v, v, k, v, g_cols, band)


F1_J = 8
F2_K = 4


def _f1_kernel(u_ref, m1_ref, twr_ref, twi_ref, z_ref):
    g = jnp.dot(m1_ref[...], u_ref[...], preferred_element_type=F32)
    for jl in range(F1_J):
        twr = twr_ref[jl]
        twi = twi_ref[jl]
        for lt in range(D_F // 128):
            c0 = jl * D_F + lt * 128
            gr = g[0:N2, c0:c0 + 128]
            gi = g[N2:2 * N2, c0:c0 + 128]
            z_ref[jl, 0:N2, lt * 128:(lt + 1) * 128] = (gr * twr - gi * twi).astype(BF16)
            z_ref[jl, N2:2 * N2, lt * 128:(lt + 1) * 128] = (gr * twi + gi * twr).astype(BF16)


def _f2_kernel(zr_ref, zi_ref, m2_ref, y_ref):
    pr = jnp.dot(m2_ref[...], zr_ref[...], preferred_element_type=F32)
    pi = jnp.dot(m2_ref[...], zi_ref[...], preferred_element_type=F32)
    yr = pr[0:N1] + pi[N1:2 * N1]
    yi = pi[0:N1] - pr[N1:2 * N1]
    for kl in range(F2_K):
        y_ref[:, kl * 2 * D_F:kl * 2 * D_F + D_F] = yr[:, kl * D_F:(kl + 1) * D_F].astype(BF16)
        y_ref[:, kl * 2 * D_F + D_F:(kl + 1) * 2 * D_F] = yi[:, kl * D_F:(kl + 1) * D_F].astype(BF16)


def _f3_kernel(y_ref, uc_ref, cs_ref, mc_ref, wf_ref, gf_ref, o_ref):
    i = pl.program_id(0)

    def finish(parts, scale):
        f = jnp.concatenate(parts, axis=1) * scale
        t = jnp.dot(f.astype(BF16), wf_ref[...], preferred_element_type=F32)
        ms = jnp.mean(t * t, axis=-1, keepdims=True)
        o_ref[...] = (t * lax.rsqrt(ms + EPS) * gf_ref[...]).astype(BF16)

    @pl.when(i < NT_LAT)
    def _():
        parts = []
        for g in range(N_FG):
            lhs = jnp.concatenate([y_ref[:, g * 128:(g + 1) * 128],
                                   y_ref[:, D_F + g * 128:D_F + (g + 1) * 128]], axis=1)
            parts.append(jnp.dot(lhs, cs_ref[...], preferred_element_type=F32))
        finish(parts, float((SEQ * 128) ** -0.5))

    @pl.when(i == NT_LAT)
    def _():
        pq = jnp.dot(mc_ref[...], uc_ref[...], preferred_element_type=F32)
        parts = []
        for g in range(N_FG):
            lhs = jnp.concatenate([pq[0:CTX, g * 128:(g + 1) * 128],
                                   -pq[CTX:2 * CTX, g * 128:(g + 1) * 128]], axis=1)
            parts.append(jnp.dot(lhs.astype(BF16), cs_ref[...], preferred_element_type=F32))
        finish(parts, float((CTX * 128) ** -0.5))


def _fourier(l, u, w_f_bf, g_f, tb):
    u2 = u.reshape(T // N1, N1 * D_F)
    z = pl.pallas_call(
        _f1_kernel,
        grid=(N1 // F1_J,),
        in_specs=[pl.BlockSpec((N2, F1_J * D_F), lambda c: (0, c)),
                  pl.BlockSpec((2 * N2, N2), lambda c: (0, 0)),
                  pl.BlockSpec((F1_J, N2, 128), lambda c: (c, 0, 0)),
                  pl.BlockSpec((F1_J, N2, 128), lambda c: (c, 0, 0))],
        out_specs=pl.BlockSpec((F1_J, 2 * N2, D_F), lambda c: (c, 0, 0)),
        out_shape=jax.ShapeDtypeStruct((N1, 2 * N2, D_F), BF16),
        compiler_params=_cp(("arbitrary",)),
        name="fourier_s1",
    )(u2, tb["m1"], tb["twr"], tb["twi"])
    z2 = z.reshape(N1, 2 * N2 * D_F)
    nk = N2 // F2_K
    y = pl.pallas_call(
        _f2_kernel,
        grid=(nk,),
        in_specs=[pl.BlockSpec((N1, F2_K * D_F), lambda k: (0, k)),
                  pl.BlockSpec((N1, F2_K * D_F), lambda k: (0, nk + k)),
                  pl.BlockSpec((2 * N1, N1), lambda k: (0, 0))],
        out_specs=pl.BlockSpec((N1, F2_K * 2 * D_F), lambda k: (0, k)),
        out_shape=jax.ShapeDtypeStruct((N1, N2 * 2 * D_F), BF16),
        compiler_params=_cp(("arbitrary",)),
        name="fourier_s2",
    )(z2, z2, tb["m2"])
    y2 = y.reshape(SEQ, 2 * D_F)
    return pl.pallas_call(
        _f3_kernel,
        grid=(NT,),
        in_specs=[pl.BlockSpec((TM, 2 * D_F), lambda i: (jnp.minimum(i, NT_LAT - 1), 0)),
                  pl.BlockSpec((CTX, D_F), lambda i: (SEQ // CTX, 0)),
                  pl.BlockSpec((2 * 128, 128), lambda i: (0, 0)),
                  pl.BlockSpec((2 * CTX, CTX), lambda i: (0, 0)),
                  _layer_spec(l, D_F, D_F),
                  _row_spec(D_F)],
        out_specs=pl.BlockSpec((TM, D_F), lambda i: (i, 0)),
        out_shape=jax.ShapeDtypeStruct((T, D_F), BF16),
        compiler_params=_cp(("arbitrary",)),
        name="fourier_s3",
    )(y2, u, tb["csch"], tb["mc"], w_f_bf, g_f)


def _merge_kernel(x_ref, c_ref, fn_ref, an_ref, wo_ref, ga1_ref, g2_ref, sc2_ref, sh2_ref, ga2_ref,
                  wsg_ref, wsu_ref, wsd_ref, wrh_ref, wrl_ref, br_ref, tri_ref,
                  xmid_ref, h2_ref, info_ref, wcol_ref, cnt_ref, carry_scr, *, split):
    i = pl.program_id(0)

    @pl.when(i == 0)
    def _():
        carry_scr[...] = jnp.zeros_like(carry_scr)

    m = (jnp.dot(fn_ref[...], wo_ref[0:D_F, :], preferred_element_type=F32)
         + jnp.dot(an_ref[...], wo_ref[D_F:D_MODEL, :], preferred_element_type=F32))
    x1 = _stream_tile(x_ref, c_ref, split) + ga1_ref[...] * m
    ms = jnp.mean(x1 * x1, axis=-1, keepdims=True)
    h = x1 * lax.rsqrt(ms + EPS) * g2_ref[...]
    h = h * (1.0 + sc2_ref[...]) + sh2_ref[...]
    hb = h.astype(BF16)
    hp = _pack_bf16_pairs(h)
    for s in range(SUB):
        h2_ref[:, s, :] = hp[:, s * PK:(s + 1) * PK]

    a = jnp.dot(hb, wsg_ref[...], preferred_element_type=F32)
    u = jnp.dot(hb, wsu_ref[...], preferred_element_type=F32)
    act = (_silu(a) * u).astype(BF16)
    ys = jnp.dot(act, wsd_ref[...], preferred_element_type=F32)
    xmid_ref[...] = x1 + ga2_ref[...] * ys

    hl = (h - hb.astype(F32)).astype(BF16)
    dn = (((1,), (1,)), ((), ()))
    both = lax.dot_general(jnp.concatenate([wrh_ref[...], wrl_ref[...]], axis=0), hb, dn,
                           preferred_element_type=F32)
    lg = both[0:N_EXPERTS] + (both[N_EXPERTS:2 * N_EXPERTS]
                              + lax.dot_general(wrh_ref[...], hl, dn, preferred_element_type=F32))
    score = jax.nn.sigmoid(lg)
    sel = score + br_ref[...]
    s = [sel[jj * N_GROUPS:(jj + 1) * N_GROUPS] for jj in range(EPG)]
    sr = [score[jj * N_GROUPS:(jj + 1) * N_GROUPS] for jj in range(EPG)]
    hi01, lo01 = jnp.maximum(s[0], s[1]), jnp.minimum(s[0], s[1])
    hi23, lo23 = jnp.maximum(s[2], s[3]), jnp.minimum(s[2], s[3])
    gscore = jnp.maximum(hi01, hi23) + jnp.maximum(jnp.minimum(hi01, hi23), jnp.maximum(lo01, lo23))
    gi = lax.broadcasted_iota(I32, (N_GROUPS, TM), 0)
    gmax = jnp.max(gscore, axis=0, keepdims=True)
    g_idx = jnp.min(jnp.where(gscore == gmax, gi, N_GROUPS), axis=0, keepdims=True)
    gsel = gi == g_idx
    v = [jnp.sum(jnp.where(gsel, s[jj], 0.0), axis=0, keepdims=True) for jj in range(EPG)]
    vr = [jnp.sum(jnp.where(gsel, sr[jj], 0.0), axis=0, keepdims=True) for jj in range(EPG)]

    def first_argmax(vals):
        best = vals[0]
        idx = jnp.zeros((1, TM), I32)
        for jj in range(1, EPG):
            upd = vals[jj] > best
            best = jnp.where(upd, vals[jj], best)
            idx = jnp.where(upd, jj, idx)
        return idx

    i1 = first_argmax(v)
    i2 = first_argmax([jnp.where(i1 == jj, -jnp.inf, v[jj]) for jj in range(EPG)])

    def pick(vals, idx):
        out = vals[0]
        for jj in range(1, EPG):
            out = jnp.where(idx == jj, vals[jj], out)
        return out

    w1 = pick(vr, i1)
    w2 = pick(vr, i2)
    wsum = w1 + w2
    w1 = w1 / wsum
    w2 = w2 / wsum

    ri = lax.broadcasted_iota(I32, (N_EXPERTS, TM), 0)
    oh1 = ri == i1 * N_GROUPS + g_idx
    oh2 = ri == i2 * N_GROUPS + g_idx
    oh = jnp.logical_or(oh1, oh2).astype(F32)
    pre = jnp.dot(oh.astype(BF16), tri_ref[...], preferred_element_type=F32) + carry_scr[:, 0:1]
    r1 = jnp.sum(jnp.where(oh1, pre, 0.0), axis=0, keepdims=True)
    r2 = jnp.sum(jnp.where(oh2, pre, 0.0), axis=0, keepdims=True)
    carry = carry_scr[...] + jnp.sum(oh, axis=1, keepdims=True)
    carry_scr[...] = carry
    cnt_ref[...] = carry

    zi = jnp.zeros((4, TM), I32)
    info_ref[...] = jnp.concatenate(
        [g_idx * EPG + i1, g_idx * EPG + i2, r1.astype(I32), r2.astype(I32), zi], axis=0)
    wrow = jnp.concatenate([w1, w2, jnp.zeros((126, TM), F32)], axis=0)
    wcol_ref[...] = wrow.T


def _stream_specs(split, c_blk):
    if split:
        return [pl.BlockSpec((TM, D_MODEL), lambda i: (jnp.minimum(i, NT_LAT - 1), 0)),
                pl.BlockSpec((CTX, D_MODEL), lambda i: (c_blk, 0))]
    return [pl.BlockSpec((TM, D_MODEL), lambda i: (i, 0)),
            pl.BlockSpec((8, D_MODEL), lambda i: (0, 0))]


def _stream_tile(x_ref, c_ref, split):
    if split:
        return jnp.where(pl.program_id(0) == NT_LAT, c_ref[...], x_ref[...])
    return x_ref[...]


def _merge(l, xa, ca, c_blk, fn, an, w_out_bf, modt, g2, wsg, wsu, wsd, wrh, wrl, br, tb):
    tile = lambda n: pl.BlockSpec((TM, n), lambda i: (i, 0))
    full = lambda a, b: pl.BlockSpec((a, b), lambda i: (0, 0))
    return pl.pallas_call(
        functools.partial(_merge_kernel, split=xa is not ca),
        grid=(NT,),
        in_specs=_stream_specs(xa is not ca, c_blk) + [
                  tile(D_F), tile(D_ATTN), _layer_spec(l, D_MODEL, D_MODEL),
                  _mod_spec(2), _row_spec(D_MODEL), _mod_spec(4), _mod_spec(3), _mod_spec(5),
                  _layer_spec(l, D_MODEL, D_SHARED), _layer_spec(l, D_MODEL, D_SHARED),
                  _layer_spec(l, D_SHARED, D_MODEL),
                  full(N_EXPERTS, D_MODEL), full(N_EXPERTS, D_MODEL), full(N_EXPERTS, TM),
                  full(TM, TM)],
        out_specs=[tile(D_MODEL), pl.BlockSpec((TM, SUB, PK), lambda i: (i, 0, 0)),
                   pl.BlockSpec((8, TM), lambda i: (0, i)),
                   pl.BlockSpec((TM, 128), lambda i: (i, 0)),
                   pl.BlockSpec((N_EXPERTS, 128), lambda i: (0, 0))],
        out_shape=[jax.ShapeDtypeStruct((T, D_MODEL), F32), jax.ShapeDtypeStruct((T, SUB, PK), I32),
                   jax.ShapeDtypeStruct((8, T), I32), jax.ShapeDtypeStruct((T, 128), F32),
                   jax.ShapeDtypeStruct((N_EXPERTS, 128), F32)],
        scratch_shapes=[pltpu.VMEM((N_EXPERTS, 128), F32)],
        compiler_params=_cp(("arbitrary",)),
        name="merge_route",
    )(xa, ca, fn, an, w_out_bf, modt, g2, modt, modt, modt, wsg, wsu, wsd, wrh, wrl, br, tb["tri"])


def _dispatch_kernel(dest_ref, pend_ref, padded_ref, h_ref, xb_hbm, zero_scr, slab, sem, zsem, *, n):
    i = pl.program_id(0)
    slot = i % 2

    def zero_copy(row0):
        return pltpu.make_async_copy(zero_scr, xb_hbm.at[pl.ds(row0, BM)], zsem)

    @pl.when(i == 0)
    def _():
        zero_scr[...] = jnp.zeros_like(zero_scr)
        n_used = pend_ref[N_EXPERTS - 1] // BM
        for e in range(N_EXPERTS):
            @pl.when(padded_ref[e] > 0)
            def _():
                zero_copy(pend_ref[e] - BM).start()
        for blk in range(NB - N_EXPERTS, NB):
            @pl.when(blk >= n_used)
            def _():
                zero_copy(blk * BM).start()
        for e in range(N_EXPERTS):
            @pl.when(padded_ref[e] > 0)
            def _():
                zero_copy(0).wait()
        for blk in range(NB - N_EXPERTS, NB):
            @pl.when(blk >= n_used)
            def _():
                zero_copy(0).wait()

    slab[slot] = h_ref[...]

    def copy(sl, r, d):
        return pltpu.make_async_copy(slab.at[sl, pl.ds(r, 1)], xb_hbm.at[pl.ds(d, 1)], sem.at[sl])

    def drain(sl):
        def body(r, c):
            copy(sl, 0, 0).wait()
            copy(sl, 0, 0).wait()
            return c
        lax.fori_loop(0, TM, body, 0)

    def issue(r, c):
        t = i * TM + r
        copy(slot, r, dest_ref[2 * t]).start()
        copy(slot, r, dest_ref[2 * t + 1]).start(priority=1)
        return c

    lax.fori_loop(0, TM, issue, 0)

    @pl.when(i > 0)
    def _():
        drain(1 - slot)

    @pl.when(i == n - 1)
    def _():
        drain(slot)


def _dispatch(dest, pad_end, padded, h2):
    return pl.pallas_call(
        functools.partial(_dispatch_kernel, n=NT),
        grid_spec=pltpu.PrefetchScalarGridSpec(
            num_scalar_prefetch=3, grid=(NT,),
            in_specs=[pl.BlockSpec((TM, SUB, PK), lambda i, d, pe, pd: (i, 0, 0))],
            out_specs=pl.BlockSpec(memory_space=pl.ANY),
            scratch_shapes=[pltpu.VMEM((BM, SUB, PK), I32), pltpu.VMEM((2, TM, SUB, PK), I32),
                            pltpu.SemaphoreType.DMA((2,)), pltpu.SemaphoreType.DMA(())]),
        out_shape=jax.ShapeDtypeStruct((ROWS, SUB, PK), I32),
        compiler_params=pltpu.CompilerParams(dimension_semantics=("arbitrary",),
                                             has_side_effects=True, vmem_limit_bytes=VMEM_LIMIT),
        name="dispatch",
    )(dest, pad_end, padded, h2)


def _pack_bf16_pairs(h):
    bits = lax.bitcast_convert_type(h.astype(BF16).astype(F32), I32)
    half = h.shape[1] // 2
    return (bits[:, :half] & -65536) | lax.shift_right_logical(bits[:, half:], 16)


GRP = 32


def _expert_kernel(eo_ref, es_ref, n_ref, x_ref, perm_ref, wg_hbm, wu_hbm, wd_hbm, y_ref,
                   wgf, wuf, wdf, wgb, wub, wdb, xs_scr, wsem, *, layer):
    b = pl.program_id(0)
    k = eo_ref[b]
    first = jnp.logical_or(b == 0, k != eo_ref[jnp.maximum(b - 1, 0)])

    def copies(kk, slot):
        e = es_ref[kk]
        return (pltpu.make_async_copy(wg_hbm.at[layer, e], wgf.at[slot], wsem.at[slot, 0]),
                pltpu.make_async_copy(wu_hbm.at[layer, e], wuf.at[slot], wsem.at[slot, 1]),
                pltpu.make_async_copy(wd_hbm.at[layer, e], wdf.at[slot], wsem.at[slot, 2]))

    @pl.when(b == 0)
    def _():
        for c in copies(0, 0):
            c.start()

    @pl.when(jnp.logical_and(first, b < n_ref[0]))
    def _():
        slot = k % 2
        for c in copies(k, slot):
            c.wait()

        @pl.when(k + 1 < n_ref[1])
        def _():
            for c in copies(k + 1, 1 - slot):
                c.start()

        wgb[...] = wgf[slot].astype(BF16)
        wub[...] = wuf[slot].astype(BF16)
        wdb[...] = wdf[slot].astype(BF16)

    @pl.when(b < n_ref[0])
    def _():
        half = D_MODEL // 2
        for grp in range(BM // GRP):
            w = x_ref[grp * GRP:(grp + 1) * GRP].reshape(GRP * SUB, PK)
            hi = lax.bitcast_convert_type(w & -65536, F32).astype(BF16)
            lo = lax.bitcast_convert_type(w << 16, F32).astype(BF16)
            g = jnp.concatenate([hi, lo], axis=1)
            r = jnp.dot(perm_ref[...], g, preferred_element_type=F32).astype(BF16)
            rows = slice(grp * GRP, (grp + 1) * GRP)
            for s in range(SUB):
                xs_scr[rows, s * PK:(s + 1) * PK] = r[s * GRP:(s + 1) * GRP, 0:PK]
                xs_scr[rows, half + s * PK:half + (s + 1) * PK] = r[s * GRP:(s + 1) * GRP, PK:2 * PK]
        x = xs_scr[...]
        a = jnp.dot(x, wgb[...], preferred_element_type=F32)
        u = jnp.dot(x, wub[...], preferred_element_type=F32)
        act = (_silu(a) * u).astype(BF16)
        yp = _pack_bf16_pairs(jnp.dot(act, wdb[...], preferred_element_type=F32))
        for s in range(SUB):
            y_ref[:, s, :] = yp[:, s * PK:(s + 1) * PK]

    @pl.when(b >= n_ref[0])
    def _():
        y_ref[...] = jnp.zeros_like(y_ref)


def _experts(l, eo, es, nn, xb, perm, w_g, w_u, w_d):
    hbm = pl.BlockSpec(memory_space=pl.ANY)
    return pl.pallas_call(
        functools.partial(_expert_kernel, layer=l),
        grid_spec=pltpu.PrefetchScalarGridSpec(
            num_scalar_prefetch=3, grid=(NB,),
            in_specs=[pl.BlockSpec((BM, SUB, PK), lambda b, eo, es, n: (jnp.minimum(b, n[0] - 1), 0, 0)),
                      pl.BlockSpec((GRP * SUB, GRP * SUB), lambda b, eo, es, n: (0, 0)),
                      hbm, hbm, hbm],
            out_specs=pl.BlockSpec((BM, SUB, PK), lambda b, eo, es, n: (b, 0, 0)),
            scratch_shapes=[pltpu.VMEM((2, D_MODEL, D_EXPERT), F32),
                            pltpu.VMEM((2, D_MODEL, D_EXPERT), F32),
                            pltpu.VMEM((2, D_EXPERT, D_MODEL), F32),
                            pltpu.VMEM((D_MODEL, D_EXPERT), BF16),
                            pltpu.VMEM((D_MODEL, D_EXPERT), BF16),
                            pltpu.VMEM((D_EXPERT, D_MODEL), BF16),
                            pltpu.VMEM((BM, D_MODEL), BF16),
                            pltpu.SemaphoreType.DMA((2, 3))]),
        out_shape=jax.ShapeDtypeStruct((ROWS, SUB, PK), I32),
        compiler_params=_cp(("arbitrary",)),
        name="experts",
    )(eo, es, nn, xb, perm, w_g, w_u, w_d)


def _combine_kernel(dest_ref, y_hbm, xmid_ref, wcol_ref, ga2_ref, perm_ref, o_ref, b1, b2, sem, *, n):
    i = pl.program_id(0)

    def copy(d, buf, slot, r):
        return pltpu.make_async_copy(y_hbm.at[pl.ds(d, 1)], buf.at[slot, pl.ds(r, 1)], sem.at[slot])

    def issue(tile, slot):
        def body(r, c):
            t = tile * TM + r
            copy(dest_ref[2 * t], b1, slot, r).start()
            copy(dest_ref[2 * t + 1], b2, slot, r).start(priority=1)
            return c
        lax.fori_loop(0, TM, body, 0)

    @pl.when(i == 0)
    def _():
        issue(0, 0)

    @pl.when(i + 1 < n)
    def _():
        issue(i + 1, (i + 1) % 2)

    slot = i % 2

    def drain(r, c):
        copy(0, b1, slot, 0).wait()
        copy(0, b2, slot, 0).wait()
        return c

    lax.fori_loop(0, TM, drain, 0)

    def unpack_group(buf, grp):
        w = buf[slot, grp * GRP:(grp + 1) * GRP].reshape(GRP * SUB, PK)
        hi = lax.bitcast_convert_type(w & -65536, F32).astype(BF16)
        lo = lax.bitcast_convert_type(w << 16, F32).astype(BF16)
        return jnp.dot(perm_ref[...], jnp.concatenate([hi, lo], axis=1), preferred_element_type=F32)

    half = D_MODEL // 2
    for grp in range(TM // GRP):
        rows = slice(grp * GRP, (grp + 1) * GRP)
        w1 = wcol_ref[rows, 0:1]
        w2 = wcol_ref[rows, 1:2]
        r1 = unpack_group(b1, grp)
        r2 = unpack_group(b2, grp)
        for s in range(SUB):
            sr = slice(s * GRP, (s + 1) * GRP)
            for c0, lanes in ((s * PK, slice(0, PK)), (half + s * PK, slice(PK, 2 * PK))):
                cs = slice(c0, c0 + PK)
                o_ref[rows, cs] = xmid_ref[rows, cs] + ga2_ref[:, cs] * (w1 * r1[sr, lanes] + w2 * r2[sr, lanes])


def _combine(dest, yb, xmid, wcol, modt, perm, n_tiles):
    tile = lambda n: pl.BlockSpec((TM, n), lambda i, d: (i, 0))
    return pl.pallas_call(
        functools.partial(_combine_kernel, n=n_tiles),
        grid_spec=pltpu.PrefetchScalarGridSpec(
            num_scalar_prefetch=1, grid=(n_tiles,),
            in_specs=[pl.BlockSpec(memory_space=pl.ANY), tile(D_MODEL), tile(128),
                      pl.BlockSpec((None, None, 1, D_MODEL), lambda i, d: (i // NT_LAT, 5, 0, 0)),
                      pl.BlockSpec((GRP * SUB, GRP * SUB), lambda i, d: (0, 0))],
            out_specs=tile(D_MODEL),
            scratch_shapes=[pltpu.VMEM((2, TM, SUB, PK), I32), pltpu.VMEM((2, TM, SUB, PK), I32),
                            pltpu.SemaphoreType.DMA((2,))]),
        out_shape=jax.ShapeDtypeStruct((n_tiles * TM, D_MODEL), F32),
        compiler_params=_cp(("arbitrary",)),
        name="combine",
    )(dest, yb, xmid, wcol, modt, perm)


def _perm_experts(a):
    return a.reshape(a.shape[:-1] + (N_GROUPS, EPG)).swapaxes(-1, -2).reshape(a.shape)


def kernel(x, c, ctx, c_ctx, w_mod, b_mod, g_norm1, g_norm2, w_in, w_fourier, g_q, g_k, sink,
           g_branch_f, g_branch_a, w_out, w_router, b_router, w_exp_gate, w_exp_up, w_exp_down,
           w_sh_gate, w_sh_up, w_sh_down):
    assert x.shape == (1, SEQ, D_MODEL) and ctx.shape == (1, CTX, D_MODEL)
    tb = _tables()
    stream = (x[0], ctx[0], 0)

    cc = jnp.zeros((8, D_MODEL), F32).at[0].set(c[0]).at[1].set(c_ctx)
    mods = _modulation(cc, w_mod, b_mod)

    wr_t = _perm_experts(w_router).T
    wrh = wr_t.astype(BF16)
    wrl = (wr_t - wrh.astype(F32)).astype(BF16)
    br = jnp.broadcast_to(_perm_experts(b_router)[:, None], (N_EXPERTS, TM)).astype(F32)

    w_in_b, w_f_b, w_out_b = w_in.astype(BF16), w_fourier.astype(BF16), w_out.astype(BF16)
    w_sg_b, w_su_b, w_sd_b = w_sh_gate.astype(BF16), w_sh_up.astype(BF16), w_sh_down.astype(BF16)

    for l in range(DEPTH):
        last = l == DEPTH - 1
        modt = mods[l, 0:2].reshape(2, 6, 1, D_MODEL)
        row = lambda a: a[l].reshape(1, -1)
        u, q, k, v = _project(l, *stream, row(g_norm1), modt, w_in_b, row(g_q), row(g_k), tb)
        an = _attention(q, k, v, sink[l], row(g_branch_a), tb["band"])
        fn = _fourier(l, u, w_f_b, row(g_branch_f), tb)
        xmid, h2, info, wcol, cnt = _merge(
            l, *stream, fn, an, w_out_b, modt, row(g_norm2), w_sg_b, w_su_b, w_sd_b,
            wrh, wrl, br, tb)

        xs = _routed(l, info, cnt, h2, xmid, wcol, modt, w_exp_gate, w_exp_up, w_exp_down, tb,
                     NT_LAT if last else NT)
        stream = (xs, xs, NT_LAT)
    return xs[None]


def _routed(l, info, cnt, h2, xmid, wcol, modt, w_g, w_u, w_d, tb, n_tiles):
    counts = cnt[:, 0].astype(I32).reshape(EPG, N_GROUPS).T.reshape(N_EXPERTS)
    padded = ((counts + BM - 1) // BM) * BM
    pad_end = jnp.cumsum(padded).astype(I32)
    pad_start = pad_end - padded
    start = jnp.sum(jnp.where(info[0:2, :, None] == jnp.arange(N_EXPERTS, dtype=I32),
                              pad_start, 0), axis=-1)
    dest = (start + info[2:4]).T.reshape(2 * T)
    ids = jnp.arange(N_EXPERTS, dtype=I32)
    block_row0 = jnp.arange(NB, dtype=I32)[:, None] * BM
    block_e = jnp.minimum(jnp.sum((pad_end <= block_row0).astype(I32), axis=-1), N_EXPERTS - 1)
    used = padded > 0
    ordinal = jnp.cumsum(used.astype(I32)) - 1
    eo = jnp.sum(jnp.where(block_e[:, None] == ids, ordinal, 0), axis=-1)
    es = jnp.sum(jnp.where(jnp.logical_and(used, ordinal == ids[:, None]), ids, 0), axis=-1)
    nn = jnp.stack([pad_end[N_EXPERTS - 1] // BM, ordinal[N_EXPERTS - 1] + 1]).astype(I32)
    xb = _dispatch(dest, pad_end, padded, h2)
    yb = _experts(l, eo.astype(I32), es.astype(I32), nn, xb, tb["perm"], w_g, w_u, w_d)
    return _combine(dest, yb, xmid, wcol, modt, tb["perm"], n_tiles)
```
